```python
import math
import jax, jax.numpy as jnp
from jax import lax
import numpy as np

D_MODEL = 1024
BATCH = 8
SEQ = 4096
DEPTH = 1
DEC_BATCH = 32
DEC_SEQ = 16
PAST_LEN = 2048

CHUNK = 64
Q_BLOCK = 128
EPS = 1e-6
GLA_HEADS = 4
GLA_DK = 64
GLA_DV = 128
GLA_RANK = 16
GLA_TAU = 16.0
DIFF_HEADS = 4
DIFF_DH = 64
DIFF_DV = 128
D_FF = 4 * D_MODEL
W_Q_G = GLA_HEADS * GLA_DK
W_K_G = GLA_HEADS * GLA_DK
W_V_G = GLA_HEADS * GLA_DV
W_R_G = GLA_HEADS * GLA_DV
W_A_G = GLA_RANK
W_Q_D = DIFF_HEADS * 2 * DIFF_DH
W_K_D = DIFF_HEADS * 2 * DIFF_DH
W_V_D = DIFF_HEADS * DIFF_DV
SPLITS = tuple(np.cumsum([W_Q_G, W_K_G, W_V_G, W_R_G, W_A_G, W_Q_D, W_K_D]).tolist())
D_IN = W_Q_G + W_K_G + W_V_G + W_R_G + W_A_G + W_Q_D + W_K_D + W_V_D
D_MIX = W_V_G + W_V_D

kernel_name = "hybrid_gla_diffattn_streaming_step"


def rms(x, g):
    xf = x.astype(jnp.float32)
    y = xf * lax.rsqrt(jnp.mean(xf * xf, axis=-1, keepdims=True) + EPS)
    return (y * g.astype(jnp.float32)).astype(x.dtype)


def alibi_slopes():
    return jnp.exp2(-8.0 / DIFF_HEADS * jnp.arange(1, DIFF_HEADS + 1, dtype=jnp.float32))


def gla_chunk(S, q, k, v, g):
    L = q.shape[2]
    b = jnp.cumsum(g, axis=2)
    mask = jnp.tril(jnp.ones((L, L), dtype=bool))
    diff = b[:, :, :, None, :] - b[:, :, None, :, :]
    decay = jnp.exp(jnp.where(mask[None, None, :, :, None], diff, -jnp.inf))
    A = jnp.einsum('bhtd,bhsd,bhtsd->bhts', q, k, decay)
    o = jnp.einsum('bhts,bhsv->bhtv', A, v) + jnp.einsum('bhtd,bhdv->bhtv', q * jnp.exp(b), S)
    bL = b[:, :, -1]
    S_new = jnp.exp(bL)[..., None] * S + jnp.einsum('bhsd,bhsv->bhdv', k * jnp.exp(bL[:, :, None, :] - b), v)
    return S_new, o


def gla_scan(S0, q, k, v, g):
    B, T, H, _ = q.shape
    L = min(CHUNK, T)
    n = T // L

    def to_chunks(a):
        return a.reshape(B, n, L, H, a.shape[-1]).transpose(1, 0, 3, 2, 4)

    def step(S, c):
        return gla_chunk(S, *c)

    S, o = lax.scan(step, S0, (to_chunks(q), to_chunks(k), to_chunks(v), to_chunks(g)))
    o = o.transpose(1, 0, 3, 2, 4).reshape(B, T, H, o.shape[-1])
    return S, o


def diff_attend(q, k, v, q_pos, k_pos, lam, slopes):
    q = q.astype(jnp.float32)
    k = k.astype(jnp.float32)
    scale = DIFF_DH ** -0.5
    s1 = jnp.einsum('bqhd,bkhd->bhqk', q[..., :DIFF_DH], k[..., :DIFF_DH]) * scale
    s2 = jnp.einsum('bqhd,bkhd->bhqk', q[..., DIFF_DH:], k[..., DIFF_DH:]) * scale
    mask = (k_pos[None, :] // CHUNK) <= (q_pos[:, None] // CHUNK)
    dist = jnp.abs(q_pos[:, None] - k_pos[None, :]).astype(jnp.float32)
    bias = jnp.where(mask[None], -slopes[:, None, None] * dist[None], -jnp.inf)
    p = jax.nn.softmax(s1 + bias, axis=-1) - lam * jax.nn.softmax(s2 + bias, axis=-1)
    return jnp.einsum('bhqk,bkhv->bqhv', p, v.astype(jnp.float32))


def layer_forward(x, S0, past_k, past_v, start, lam_init, w_in, w_gate_up, b_gate, g_gla_out,
                  lam_q1, lam_k1, lam_q2, lam_k2, g_subln, w_out, g_pre_mix, g_post_mix,
                  g_pre_ffn, g_post_ffn, w_ff_up, w_ff_down):
    B, T, _ = x.shape
    h = rms(x, g_pre_mix)
    z = h @ w_in
    q_g, k_g, v_g, r_g, a_g, q_d, k_d, v_d = jnp.split(z, SPLITS, axis=-1)
    gate = jax.nn.log_sigmoid((a_g @ w_gate_up + b_gate).astype(jnp.float32)) / GLA_TAU
    qg = q_g.reshape(B, T, GLA_HEADS, GLA_DK).astype(jnp.float32) * (GLA_DK ** -0.5)
    kg = k_g.reshape(B, T, GLA_HEADS, GLA_DK).astype(jnp.float32)
    vg = v_g.reshape(B, T, GLA_HEADS, GLA_DV).astype(jnp.float32)
    gg = gate.reshape(B, T, GLA_HEADS, GLA_DK)
    S_new, o_g = gla_scan(S0.astype(jnp.float32), qg, kg, vg, gg)
    rg = r_g.reshape(B, T, GLA_HEADS, GLA_DV).astype(jnp.float32)
    o_g = rms(o_g, g_gla_out) * jax.nn.silu(rg)
    qd = q_d.reshape(B, T, DIFF_HEADS, 2 * DIFF_DH)
    kd = k_d.reshape(B, T, DIFF_HEADS, 2 * DIFF_DH)
    vd = v_d.reshape(B, T, DIFF_HEADS, DIFF_DV)
    lam = (jnp.exp(jnp.sum(lam_q1.astype(jnp.float32) * lam_k1.astype(jnp.float32)))
           - jnp.exp(jnp.sum(lam_q2.astype(jnp.float32) * lam_k2.astype(jnp.float32))) + lam_init)
    slopes = alibi_slopes()
    if past_k is None:
        k_pos = jnp.arange(T, dtype=jnp.int32)
        nq = T // Q_BLOCK
        qb = qd.reshape(B, nq, Q_BLOCK, DIFF_HEADS, 2 * DIFF_DH).transpose(1, 0, 2, 3, 4)
        pb = k_pos.reshape(nq, Q_BLOCK)
        ob = lax.map(lambda a: diff_attend(a[0], kd, vd, a[1], k_pos, lam, slopes), (qb, pb))
        o_d = ob.transpose(1, 0, 2, 3, 4).reshape(B, T, DIFF_HEADS, DIFF_DV)
    else:
        keys = jnp.concatenate([past_k.astype(kd.dtype), kd], axis=1)
        vals = jnp.concatenate([past_v.astype(vd.dtype), vd], axis=1)
        k_pos = jnp.arange(keys.shape[1], dtype=jnp.int32)
        q_pos = start + jnp.arange(T, dtype=jnp.int32)
        o_d = diff_attend(qd, keys, vals, q_pos, k_pos, lam, slopes)
    o_d = rms(o_d, g_subln) * (1.0 - lam_init)
    o = jnp.concatenate([o_g.reshape(B, T, W_V_G), o_d.reshape(B, T, W_V_D)], axis=-1).astype(x.dtype)
    x = x + rms(o @ w_out, g_post_mix)
    f = rms(x, g_pre_ffn)
    f = jnp.square(jax.nn.relu(f @ w_ff_up)) @ w_ff_down
    x = x + rms(f, g_post_ffn)
    return x, kd, vd, S_new.astype(x.dtype)


def setup_inputs(seed: int = 0) -> dict:
    key = jax.random.key(seed)
    ks = jax.random.split(key, 24)
    f32 = jnp.float32
    nrm = lambda k, s, sc: jax.random.normal(k, s, f32) * sc
    gain = lambda k, n: 1.0 + 0.01 * jax.random.normal(k, (DEPTH, n), f32)
    return {
        "x_prompt": nrm(ks[0], (BATCH, SEQ, D_MODEL), 1.0),
        "x_sample": nrm(ks[1], (DEC_BATCH, DEC_SEQ, D_MODEL), 1.0),
        "cache_k": nrm(ks[2], (DEPTH, DEC_BATCH, PAST_LEN, DIFF_HEADS, 2 * DIFF_DH), 1.0),
        "cache_v": nrm(ks[3], (DEPTH, DEC_BATCH, PAST_LEN, DIFF_HEADS, DIFF_DV), 1.0),
        "state_gla": nrm(ks[4], (DEPTH, DEC_BATCH, GLA_HEADS, GLA_DK, GLA_DV), 1.0),
        "w_in": nrm(ks[5], (DEPTH, D_MODEL, D_IN), D_MODEL ** -0.5),
        "w_gate_up": nrm(ks[6], (DEPTH, GLA_RANK, GLA_HEADS * GLA_DK), GLA_RANK ** -0.5),
        "b_gate": nrm(ks[7], (DEPTH, GLA_HEADS * GLA_DK), 0.01),
        "g_gla_out": gain(ks[8], GLA_DV),
        "lam_q1": nrm(ks[9], (DEPTH, DIFF_DH), 0.1),
        "lam_k1": nrm(ks[10], (DEPTH, DIFF_DH), 0.1),
        "lam_q2": nrm(ks[11], (DEPTH, DIFF_DH), 0.1),
        "lam_k2": nrm(ks[12], (DEPTH, DIFF_DH), 0.1),
        "g_subln": gain(ks[13], DIFF_DV),
        "w_out": nrm(ks[14], (DEPTH, D_MIX, D_MODEL), D_MIX ** -0.5),
        "g_pre_mix": gain(ks[15], D_MODEL),
        "g_post_mix": gain(ks[16], D_MODEL),
        "g_pre_ffn": gain(ks[17], D_MODEL),
        "g_post_ffn": gain(ks[18], D_MODEL),
        "w_ff_up": nrm(ks[19], (DEPTH, D_MODEL, D_FF), D_MODEL ** -0.5),
        "w_ff_down": nrm(ks[20], (DEPTH, D_FF, D_MODEL), D_FF ** -0.5),
    }


def reference(x_prompt, x_sample, cache_k, cache_v, state_gla, w_in, w_gate_up, b_gate, g_gla_out,
              lam_q1, lam_k1, lam_q2, lam_k2, g_subln, w_out, g_pre_mix, g_post_mix,
              g_pre_ffn, g_post_ffn, w_ff_up, w_ff_down):
    yp = x_prompt
    ys = x_sample
    kp_l, vp_l, sp_l, ksm_l, vsm_l, ss_l = [], [], [], [], [], []
    for l in range(DEPTH):
        lam_init = 0.8 - 0.6 * math.exp(-0.3 * l)
        params = (w_in[l], w_gate_up[l], b_gate[l], g_gla_out[l], lam_q1[l], lam_k1[l], lam_q2[l],
                  lam_k2[l], g_subln[l], w_out[l], g_pre_mix[l], g_post_mix[l], g_pre_ffn[l],
                  g_post_ffn[l], w_ff_up[l], w_ff_down[l])
        S0 = jnp.zeros((yp.shape[0], GLA_HEADS, GLA_DK, GLA_DV), jnp.float32)
        yp, kp, vp, sp = layer_forward(yp, S0, None, None, 0, lam_init, *params)
        ys, ksm, vsm, ss = layer_forward(ys, state_gla[l], cache_k[l], cache_v[l], cache_k.shape[2],
                                         lam_init, *params)
        kp_l.append(kp); vp_l.append(vp); sp_l.append(sp)
        ksm_l.append(ksm); vsm_l.append(vsm); ss_l.append(ss)
    k_prompt = jnp.stack(kp_l, axis=0)
    v_prompt = jnp.stack(vp_l, axis=0)
    gla_prompt = jnp.stack(sp_l, axis=0)
    k_sample = jnp.stack(ksm_l, axis=0)
    v_sample = jnp.stack(vsm_l, axis=0)
    gla_sample = jnp.stack(ss_l, axis=0)
    return (yp, ys, k_prompt, v_prompt, gla_prompt, k_sample, v_sample, gla_sample)
```

```python
import functools
import math

import jax
import jax.numpy as jnp
from jax import lax
from jax.experimental import pallas as pl
from jax.experimental.pallas import tpu as pltpu

F32 = jnp.float32
BF16 = jnp.bfloat16

EPS = 1e-6
CHUNK = 64
GLA_HEADS = 4
GLA_DK = 64
GLA_DV = 128
GLA_RANK = 16
GLA_TAU = 16.0
DIFF_HEADS = 4
DIFF_DH = 64
DIFF_DV = 128

W_G = GLA_HEADS * GLA_DK
W_V = GLA_HEADS * GLA_DV
W_D = DIFF_HEADS * 2 * DIFF_DH
LANES = 128
OFF_QG = 0
OFF_KG = OFF_QG + W_G
OFF_VG = OFF_KG + W_G
OFF_RG = OFF_VG + W_V
OFF_QD = OFF_RG + W_V
OFF_KD = OFF_QD + W_D
OFF_VD = OFF_KD + W_D
OFF_AG = OFF_VD + W_D
W_IN_PADDED = OFF_AG + LANES

VMEM_LIMIT_BYTES = 56 * 1024 * 1024
GLA_SAFE_LOG_DECAY = -60.0


def _rms(x, g):
    ms = jnp.mean(x * x, axis=-1, keepdims=True)
    return x * lax.rsqrt(ms + EPS) * g


def _log_sigmoid(x):
    return jnp.minimum(x, 0.0) - jnp.log(1.0 + jnp.exp(-jnp.abs(x)))


def _div_pow2(x, d):
    assert d & (d - 1) == 0
    return lax.shift_right_arithmetic(x, d.bit_length() - 1)


def _const_spec(shape):
    zeros = (0,) * len(shape)
    return pl.BlockSpec(shape, lambda *_: zeros, pipeline_mode=pl.Buffered(1))


def _in_proj_kernel(x_ref, g_ref, w_ref, wg_ref, bg_ref,
                    qg_ref, kg_ref, vg_ref, rg_ref, gate_ref,
                    qd_ref, kd_ref, kdb_ref, vd_ref, vdb_ref):
    h = _rms(x_ref[...], g_ref[...]).astype(BF16)

    def proj(off, width):
        return jnp.dot(h, w_ref[:, off:off + width], preferred_element_type=F32)

    qg_ref[...] = (proj(OFF_QG, W_G) * (GLA_DK ** -0.5)).astype(BF16)
    kg_ref[...] = proj(OFF_KG, W_G).astype(BF16)
    vg_ref[...] = proj(OFF_VG, W_V).astype(BF16)
    rg_ref[...] = proj(OFF_RG, W_V).astype(BF16)
    qd = proj(OFF_QD, W_D) * (DIFF_DH ** -0.5)
    kd = proj(OFF_KD, W_D)
    vd = proj(OFF_VD, W_D)
    kd_ref[...] = kd
    vd_ref[...] = vd
    for hd in range(DIFF_HEADS):
        sl = slice(hd * LANES, (hd + 1) * LANES)
        qd_ref[0, hd] = qd[:, sl].astype(BF16)
        kdb_ref[0, hd] = kd[:, sl].astype(BF16)
        vdb_ref[0, hd] = vd[:, sl].astype(BF16)
    ag = proj(OFF_AG, LANES).astype(BF16)
    pre = jnp.dot(ag, wg_ref[...], preferred_element_type=F32) + bg_ref[...]
    gate_ref[...] = _log_sigmoid(pre) * (1.0 / GLA_TAU)


def _in_proj(x2d, g_pre, w_in_r, wg_pad, b_gate, *, nb, t, tm):
    n, d = x2d.shape
    steps_per_b = t // tm
    row = lambda i: (i, 0)
    hm = lambda i: (i // steps_per_b, 0, i % steps_per_b, 0)
    hm_shape = jax.ShapeDtypeStruct((nb, DIFF_HEADS, t, LANES), BF16)
    hm_spec = pl.BlockSpec((1, DIFF_HEADS, tm, LANES), hm)
    out_shape = (
        jax.ShapeDtypeStruct((n, W_G), BF16), jax.ShapeDtypeStruct((n, W_G), BF16),
        jax.ShapeDtypeStruct((n, W_V), BF16), jax.ShapeDtypeStruct((n, W_V), BF16),
        jax.ShapeDtypeStruct((n, W_G), F32),
        hm_shape,
        jax.ShapeDtypeStruct((n, W_D), F32), hm_shape,
        jax.ShapeDtypeStruct((n, W_D), F32), hm_shape,
    )
    out_specs = (
        pl.BlockSpec((tm, W_G), row), pl.BlockSpec((tm, W_G), row),
        pl.BlockSpec((tm, W_V), row), pl.BlockSpec((tm, W_V), row),
        pl.BlockSpec((tm, W_G), row),
        hm_spec,
        pl.BlockSpec((tm, W_D), row), hm_spec,
        pl.BlockSpec((tm, W_D), row), hm_spec,
    )
    return pl.pallas_call(
        _in_proj_kernel,
        grid=(n // tm,),
        in_specs=[pl.BlockSpec((tm, d), row), _const_spec((1, d)),
                  _const_spec(w_in_r.shape), _const_spec(wg_pad.shape), _const_spec((1, W_G))],
        out_specs=out_specs,
        out_shape=out_shape,
        compiler_params=pltpu.CompilerParams(
            dimension_semantics=("arbitrary",), vmem_limit_bytes=VMEM_LIMIT_BYTES),
        name="in_proj",
    )(x2d, g_pre, w_in_r, wg_pad, b_gate)


def _gla_kernel(q_ref, k_ref, v_ref, r_ref, gate_ref, s0_ref, gout_ref,
                o_ref, sout_ref, s_scr, oi_scr, qf_scr, kf_scr, b_scr, *, L, CB):
    H, DK, DV = GLA_HEADS, GLA_DK, GLA_DV
    HL = H * L
    step = pl.program_id(1)

    @pl.when(step == 0)
    def _():
        s_scr[...] = s0_ref[0].reshape(H * DK, DV)

    tri = (lax.broadcasted_iota(jnp.int32, (L, L), 0)
           >= lax.broadcasted_iota(jnp.int32, (L, L), 1)).astype(BF16)
    ones_l = jnp.ones((L, DV), BF16)
    lane_head = _div_pow2(lax.broadcasted_iota(jnp.int32, (L, W_G), 1), DK)
    row_i = lax.broadcasted_iota(jnp.int32, (HL, HL), 0)
    col_i = lax.broadcasted_iota(jnp.int32, (HL, HL), 1)
    causal = (col_i <= row_i) & (col_i >= (row_i & ~(L - 1)))
    row_t = lax.broadcasted_iota(jnp.int32, (L, 1), 0)
    gout = gout_ref[...]

    def stack_heads(a):
        return jnp.concatenate(
            [jnp.where(lane_head == hd, a, 0.0) for hd in range(H)], axis=0)

    def chunk(c, carry):
        r0 = pl.multiple_of(c * L, L)
        rows = pl.ds(r0, L)
        g = gate_ref[rows, :]
        g_hi = g.astype(BF16)
        g_lo = (g - g_hi.astype(F32)).astype(BF16)
        b = (jnp.dot(tri, g_hi, preferred_element_type=F32)
             + jnp.dot(tri, g_lo, preferred_element_type=F32))
        b_last = b[L - 1:L, :]
        q = q_ref[rows, :].astype(F32)
        k = k_ref[rows, :].astype(F32)
        v = v_ref[rows, :]
        qs = stack_heads(q * jnp.exp(b)).astype(BF16)
        kends = stack_heads(k * jnp.exp(b_last - b)).astype(BF16)
        vs = jnp.concatenate([v[:, hd * DV:(hd + 1) * DV] for hd in range(H)], axis=0)
        s_old = s_scr[...]
        o_inter = jnp.dot(qs, s_old.astype(BF16), preferred_element_type=F32)

        safe = jnp.min(b_last) >= GLA_SAFE_LOG_DECAY

        @pl.when(safe)
        def _():
            ks = stack_heads(k * jnp.exp(-b)).astype(BF16)
            a = lax.dot_general(qs, ks, (((1,), (1,)), ((), ())), preferred_element_type=F32)
            a = jnp.where(causal, a, 0.0).astype(BF16)
            oi_scr[...] = jnp.dot(a, vs, preferred_element_type=F32)

        @pl.when(jnp.logical_not(safe))
        def _():
            qf_scr[...] = q
            kf_scr[...] = k
            b_scr[...] = b

            def tok(t, carry_t):
                b_all = b_scr[...]
                w = (qf_scr[pl.ds(t, 1), :] * kf_scr[...]
                     * jnp.exp(jnp.minimum(b_scr[pl.ds(t, 1), :] - b_all, 0.0)))
                for hd in range(H):
                    a_col = jnp.sum(jnp.where(lane_head == hd, w, 0.0), axis=-1, keepdims=True)
                    a_col = jnp.where(row_t <= t, a_col, 0.0)
                    vh = v_ref[rows, hd * DV:(hd + 1) * DV].astype(F32)
                    oi_scr[pl.ds(hd * L + t, 1), :] = jnp.sum(a_col * vh, axis=0, keepdims=True)
                return carry_t

            lax.fori_loop(0, L, tok, 0)

        o = o_inter + oi_scr[...]
        o = _rms(o, gout)
        r = r_ref[rows, :].astype(F32)
        for hd in range(H):
            rh = r[:, hd * DV:(hd + 1) * DV]
            o_ref[rows, hd * DV:(hd + 1) * DV] = (
                o[hd * L:(hd + 1) * L, :] * (rh * jax.nn.sigmoid(rh))).astype(BF16)

        tdot = (((0,), (0,)), ((), ()))
        kv = lax.dot_general(kends, vs, tdot, preferred_element_type=F32)
        dec = (lax.dot_general(g_hi, ones_l, tdot, preferred_element_type=F32)
               + lax.dot_general(g_lo, ones_l, tdot, preferred_element_type=F32))
        s_scr[...] = jnp.exp(dec) * s_old + kv
        return carry

    lax.fori_loop(0, CB, chunk, 0)

    @pl.when(step == pl.num_programs(1) - 1)
    def _():
        sout_ref[0] = s_scr[...].reshape(H, DK, DV)


def _gla(qg, kg, vg, rg, gate, s0, g_out, *, nb, t, L, CB):
    n = qg.shape[0]
    tm = L * CB
    steps = t // tm
    row = lambda b, s: (b * steps + s, 0)
    st = lambda b, s: (b, 0, 0, 0)
    state_shape = (1, GLA_HEADS, GLA_DK, GLA_DV)
    return pl.pallas_call(
        functools.partial(_gla_kernel, L=L, CB=CB),
        grid=(nb, steps),
        in_specs=[pl.BlockSpec((tm, W_G), row), pl.BlockSpec((tm, W_G), row),
                  pl.BlockSpec((tm, W_V), row), pl.BlockSpec((tm, W_V), row),
                  pl.BlockSpec((tm, W_G), row), pl.BlockSpec(state_shape, st),
                  pl.BlockSpec((1, GLA_DV), lambda b, s: (0, 0))],
        out_specs=(pl.BlockSpec((tm, W_V), row), pl.BlockSpec(state_shape, st)),
        out_shape=(jax.ShapeDtypeStruct((n, W_V), BF16),
                   jax.ShapeDtypeStruct((nb,) + state_shape[1:], F32)),
        scratch_shapes=[pltpu.VMEM((GLA_HEADS * GLA_DK, GLA_DV), F32),
                        pltpu.VMEM((GLA_HEADS * L, GLA_DV), F32),
                        pltpu.VMEM((L, W_G), F32), pltpu.VMEM((L, W_G), F32),
                        pltpu.VMEM((L, W_G), F32)],
        compiler_params=pltpu.CompilerParams(
            dimension_semantics=("arbitrary", "arbitrary"), vmem_limit_bytes=VMEM_LIMIT_BYTES),
        name="gla",
    )(qg, kg, vg, rg, gate, s0, g_out)


def _lambda(lam_ref, lam_init):
    lp = lam_ref[...]
    s1 = jnp.sum(lp[0:1, :] * lp[1:2, :], axis=-1, keepdims=True)
    s2 = jnp.sum(lp[2:3, :] * lp[3:4, :], axis=-1, keepdims=True)
    return jnp.exp(s1) - jnp.exp(s2) + lam_init


def _split_halves(q):
    lane = lax.broadcasted_iota(jnp.int32, q.shape, 1)
    zero = jnp.zeros_like(q)
    return jnp.concatenate(
        [jnp.where(lane < DIFF_DH, q, zero), jnp.where(lane >= DIFF_DH, q, zero)], axis=0)


def _finish_attention(acc, l, lam, gsub, lam_init, tq):
    o = acc[:tq] / l[:tq] - lam * (acc[tq:] / l[tq:])
    return _rms(o, gsub) * (1.0 - lam_init)


def _attn_prompt_kernel(slopes_ref, lam_ref, q_ref, k_ref, v_ref, gsub_ref, o_ref, *, tq, lam_init):
    hd = pl.program_id(1)
    qi = pl.program_id(2)
    tk = tq
    slope = slopes_ref[hd]
    qq = _split_halves(q_ref[0, 0])
    k_local = lax.broadcasted_iota(jnp.int32, (1, tk), 1).astype(F32)
    nt = (((1,), (1,)), ((), ()))

    def online(carry, s, vj):
        m, l, acc = carry
        m_new = jnp.maximum(m, jnp.max(s, axis=-1, keepdims=True))
        p = jnp.exp(s - m_new)
        alpha = jnp.exp(m - m_new)
        l = alpha * l + jnp.sum(p, axis=-1, keepdims=True)
        acc = alpha * acc + jnp.dot(p.astype(BF16), vj, preferred_element_type=F32)
        return m_new, l, acc

    def blk(j, carry):
        ks = pl.ds(pl.multiple_of(j * tk, tk), tk)
        s = lax.dot_general(qq, k_ref[0, 0, ks, :], nt, preferred_element_type=F32)
        s = s + slope * (k_local + (j * tk).astype(F32))
        return online(carry, s, v_ref[0, 0, ks, :])

    init = (jnp.full((2 * tq, 1), -jnp.inf, F32), jnp.zeros((2 * tq, 1), F32),
            jnp.zeros((2 * tq, DIFF_DV), F32))
    carry = lax.fori_loop(0, qi, blk, init)

    ql = lax.broadcasted_iota(jnp.int32, (tq, tk), 0)
    kl = lax.broadcasted_iota(jnp.int32, (tq, tk), 1)
    rel = (jnp.minimum(kl, 2 * ql - kl) + qi * tq).astype(F32)
    bias = jnp.where(_div_pow2(kl, CHUNK) <= _div_pow2(ql, CHUNK), slope * rel, -jnp.inf)
    ks = pl.ds(pl.multiple_of(qi * tk, tk), tk)
    s = lax.dot_general(qq, k_ref[0, 0, ks, :], nt, preferred_element_type=F32)
    s = s + jnp.concatenate([bias, bias], axis=0)
    _, l, acc = online(carry, s, v_ref[0, 0, ks, :])

    lam = _lambda(lam_ref, lam_init)
    o_ref[...] = _finish_attention(acc, l, lam, gsub_ref[...], lam_init, tq).astype(BF16)


def _attn_prompt(slopes, lam_p, qd, kdb, vdb, g_sub, *, tq, lam_init):
    nb, nh, t, _ = qd.shape
    nq = t // tq
    kv_spec = pl.BlockSpec((1, 1, t, LANES), lambda b, h, i, *_: (b, h, 0, 0))
    grid_spec = pltpu.PrefetchScalarGridSpec(
        num_scalar_prefetch=1,
        grid=(nb, nh, nq),
        in_specs=[pl.BlockSpec((4, DIFF_DH), lambda b, h, i, *_: (0, 0)),
                  pl.BlockSpec((1, 1, tq, LANES), lambda b, h, i, *_: (b, h, i, 0)),
                  kv_spec, kv_spec,
                  pl.BlockSpec((1, DIFF_DV), lambda b, h, i, *_: (0, 0))],
        out_specs=pl.BlockSpec((tq, DIFF_DV), lambda b, h, i, *_: (b * nq + i, h)),
    )
    return pl.pallas_call(
        functools.partial(_attn_prompt_kernel, tq=tq, lam_init=lam_init),
        grid_spec=grid_spec,
        out_shape=jax.ShapeDtypeStruct((nb * t, nh * DIFF_DV), BF16),
        compiler_params=pltpu.CompilerParams(
            dimension_semantics=("arbitrary", "arbitrary", "arbitrary"),
            vmem_limit_bytes=VMEM_LIMIT_BYTES),
        name="attn_prompt",
    )(slopes, lam_p, qd, kdb, vdb, g_sub)


def _attn_sample_kernel(slopes_ref, lam_ref, q_ref, ck_ref, cv_ref, kn_ref, vn_ref, gsub_ref, o_ref,
                        *, tq, past, lam_init):
    nt = (((1,), (1,)), ((), ()))
    k_pos = lax.broadcasted_iota(jnp.int32, (1, past), 1).astype(F32)
    qi = lax.broadcasted_iota(jnp.int32, (tq, tq), 0)
    kj = lax.broadcasted_iota(jnp.int32, (tq, tq), 1)
    rel_new = (past + qi - jnp.abs(qi - kj)).astype(F32)
    lam = _lambda(lam_ref, lam_init)
    gsub = gsub_ref[...]
    for hd in range(DIFF_HEADS):
        sl = slice(hd * LANES, (hd + 1) * LANES)
        slope = slopes_ref[hd]
        qq = _split_halves(q_ref[0, hd])
        s_c = lax.dot_general(qq, ck_ref[0, :, sl].astype(BF16), nt, preferred_element_type=F32)
        s_c = s_c + slope * k_pos
        s_n = lax.dot_general(qq, kn_ref[:, sl].astype(BF16), nt, preferred_element_type=F32)
        bias_n = slope * rel_new
        s_n = s_n + jnp.concatenate([bias_n, bias_n], axis=0)
        m = jnp.maximum(jnp.max(s_c, axis=-1, keepdims=True), jnp.max(s_n, axis=-1, keepdims=True))
        p_c = jnp.exp(s_c - m)
        p_n = jnp.exp(s_n - m)
        l = jnp.sum(p_c, axis=-1, keepdims=True) + jnp.sum(p_n, axis=-1, keepdims=True)
        acc = (jnp.dot(p_c.astype(BF16), cv_ref[0, :, sl].astype(BF16), preferred_element_type=F32)
               + jnp.dot(p_n.astype(BF16), vn_ref[:, sl].astype(BF16), preferred_element_type=F32))
        o_ref[:, sl] = _finish_attention(acc, l, lam, gsub, lam_init, tq).astype(BF16)


def _attn_sample(slopes, lam_p, qd, cache_k, cache_v, k_new, v_new, g_sub, *, lam_init):
    nb, past, width = cache_k.shape
    tq = k_new.shape[0] // nb
    nh = DIFF_HEADS
    grid_spec = pltpu.PrefetchScalarGridSpec(
        num_scalar_prefetch=1,
        grid=(nb,),
        in_specs=[pl.BlockSpec((4, DIFF_DH), lambda b, *_: (0, 0)),
                  pl.BlockSpec((1, nh, tq, LANES), lambda b, *_: (0, 0, b, 0)),
                  pl.BlockSpec((1, past, width), lambda b, *_: (b, 0, 0)),
                  pl.BlockSpec((1, past, width), lambda b, *_: (b, 0, 0)),
                  pl.BlockSpec((tq, width), lambda b, *_: (b, 0)),
                  pl.BlockSpec((tq, width), lambda b, *_: (b, 0)),
                  pl.BlockSpec((1, DIFF_DV), lambda b, *_: (0, 0))],
        out_specs=pl.BlockSpec((tq, nh * DIFF_DV), lambda b, *_: (b, 0)),
    )
    return pl.pallas_call(
        functools.partial(_attn_sample_kernel, tq=tq, past=past, lam_init=lam_init),
        grid_spec=grid_spec,
        out_shape=jax.ShapeDtypeStruct((nb * tq, nh * DIFF_DV), BF16),
        compiler_params=pltpu.CompilerParams(
            dimension_semantics=("arbitrary",), vmem_limit_bytes=VMEM_LIMIT_BYTES),
        name="attn_sample",
    )(slopes, lam_p, qd, cache_k, cache_v, k_new, v_new, g_sub)


def _out_ffn_kernel(og_ref, od_ref, x_ref, wo_ref, wu_ref, wd_ref,
                    gpm_ref, gpf_ref, gqf_ref, y_ref, *, ff_chunk):
    mix = (jnp.dot(og_ref[...], wo_ref[:W_V, :], preferred_element_type=F32)
           + jnp.dot(od_ref[...], wo_ref[W_V:, :], preferred_element_type=F32))
    x1 = x_ref[...] + _rms(mix, gpm_ref[...])
    f = _rms(x1, gpf_ref[...]).astype(BF16)
    d_ff = wu_ref.shape[1]
    acc = jnp.zeros(x1.shape, F32)
    for c in range(d_ff // ff_chunk):
        sl = slice(c * ff_chunk, (c + 1) * ff_chunk)
        hid = jnp.dot(f, wu_ref[:, sl], preferred_element_type=F32)
        hid = jnp.square(jnp.maximum(hid, 0.0)).astype(BF16)
        acc = acc + jnp.dot(hid, wd_ref[sl, :], preferred_element_type=F32)
    y_ref[...] = x1 + _rms(acc, gqf_ref[...])


def _out_ffn(og, od, x2d, w_out, w_up, w_down, g_post_mix, g_pre_ffn, g_post_ffn, *, tm):
    n, d = x2d.shape
    row = lambda i: (i, 0)
    return pl.pallas_call(
        functools.partial(_out_ffn_kernel, ff_chunk=1024),
        grid=(n // tm,),
        in_specs=[pl.BlockSpec((tm, W_V), row), pl.BlockSpec((tm, W_V), row),
                  pl.BlockSpec((tm, d), row),
                  _const_spec(w_out.shape), _const_spec(w_up.shape), _const_spec(w_down.shape),
                  _const_spec((1, d)), _const_spec((1, d)), _const_spec((1, d))],
        out_specs=pl.BlockSpec((tm, d), row),
        out_shape=jax.ShapeDtypeStruct((n, d), F32),
        compiler_params=pltpu.CompilerParams(
            dimension_semantics=("arbitrary",), vmem_limit_bytes=VMEM_LIMIT_BYTES),
        name="out_ffn",
    )(og, od, x2d, w_out, w_up, w_down, g_post_mix, g_pre_ffn, g_post_ffn)


def _reorder_w_in(w):
    d = w.shape[0]
    a0 = 2 * W_G + 2 * W_V
    a1 = a0 + GLA_RANK
    return jnp.concatenate(
        [w[:, :a0], w[:, a1:], w[:, a0:a1], jnp.zeros((d, LANES - GLA_RANK), w.dtype)],
        axis=1).astype(BF16)


def kernel(x_prompt, x_sample, cache_k, cache_v, state_gla, w_in, w_gate_up, b_gate, g_gla_out,
           lam_q1, lam_k1, lam_q2, lam_k2, g_subln, w_out, g_pre_mix, g_post_mix,
           g_pre_ffn, g_post_ffn, w_ff_up, w_ff_down):
    depth = w_in.shape[0]
    nb_p, t_p, d = x_prompt.shape
    nb_s, t_s, _ = x_sample.shape
    past = cache_k.shape[2]
    slopes = jnp.exp2(-8.0 / DIFF_HEADS * jnp.arange(1, DIFF_HEADS + 1, dtype=F32))
    yp = x_prompt.reshape(nb_p * t_p, d)
    ys = x_sample.reshape(nb_s * t_s, d)
    outs = [[] for _ in range(6)]
    for l in range(depth):
        lam_init = 0.8 - 0.6 * math.exp(-0.3 * l)
        w_in_r = _reorder_w_in(w_in[l])
        wg_pad = jnp.concatenate(
            [w_gate_up[l], jnp.zeros((LANES - GLA_RANK, W_G), F32)], axis=0).astype(BF16)
        bg = b_gate[l][None, :]
        lam_p = jnp.stack([lam_q1[l], lam_k1[l], lam_q2[l], lam_k2[l]], axis=0)
        g_out = g_gla_out[l][None, :]
        g_sub = g_subln[l][None, :]
        wo = w_out[l].astype(BF16)
        wu = w_ff_up[l].astype(BF16)
        wd = w_ff_down[l].astype(BF16)
        gains = (g_post_mix[l][None, :], g_pre_ffn[l][None, :], g_post_ffn[l][None, :])
        g_pre = g_pre_mix[l][None, :]

        qg, kg, vg, rg, gate, qd, kd, kdb, vd, vdb = _in_proj(
            yp, g_pre, w_in_r, wg_pad, bg, nb=nb_p, t=t_p, tm=512)
        s0 = jnp.zeros((nb_p, GLA_HEADS, GLA_DK, GLA_DV), F32)
        og, s_p = _gla(qg, kg, vg, rg, gate, s0, g_out, nb=nb_p, t=t_p, L=CHUNK, CB=8)
        od = _attn_prompt(slopes, lam_p, qd, kdb, vdb, g_sub, tq=256, lam_init=lam_init)
        yp = _out_ffn(og, od, yp, wo, wu, wd, *gains, tm=512)
        outs[0].append(kd.reshape(nb_p, t_p, DIFF_HEADS, 2 * DIFF_DH))
        outs[1].append(vd.reshape(nb_p, t_p, DIFF_HEADS, DIFF_DV))
        outs[2].append(s_p)

        n_s = nb_s * t_s
        qg, kg, vg, rg, gate, qd, kd, _, vd, _ = _in_proj(
            ys, g_pre, w_in_r, wg_pad, bg, nb=1, t=n_s, tm=n_s)
        og, s_s = _gla(qg, kg, vg, rg, gate, state_gla[l], g_out, nb=nb_s, t=t_s, L=t_s, CB=1)
        od = _attn_sample(slopes, lam_p, qd,
                          cache_k[l].reshape(nb_s, past, W_D), cache_v[l].reshape(nb_s, past, W_D),
                          kd, vd, g_sub, lam_init=lam_init)
        ys = _out_ffn(og, od, ys, wo, wu, wd, *gains, tm=n_s)
        outs[3].append(kd.reshape(nb_s, t_s, DIFF_HEADS, 2 * DIFF_DH))
        outs[4].append(vd.reshape(nb_s, t_s, DIFF_HEADS, DIFF_DV))
        outs[5].append(s_s)

    stack = lambda xs: jnp.stack(xs, axis=0)
    return (yp.reshape(nb_p, t_p, d), ys.reshape(nb_s, t_s, d),
            stack(outs[0]), stack(outs[1]), stack(outs[2]),
            stack(outs[3]), stack(outs[4]), stack(outs[5]))
```

```python
import functools
import math

import jax
import jax.numpy as jnp
from jax import lax
from jax.experimental import pallas as pl
from jax.experimental.pallas import tpu as pltpu

F32 = jnp.float32
BF16 = jnp.bfloat16

EPS = 1e-6
CHUNK = 64
GLA_HEADS = 4
GLA_DK = 64
GLA_DV = 128
GLA_RANK = 16
GLA_TAU = 16.0
DIFF_HEADS = 4
DIFF_DH = 64
DIFF_DV = 128

W_G = GLA_HEADS * GLA_DK
W_V = GLA_HEADS * GLA_DV
W_D = DIFF_HEADS * 2 * DIFF_DH
LANES = 128
OFF_QG = 0
OFF_KG = OFF_QG + W_G
OFF_VG = OFF_KG + W_G
OFF_RG = OFF_VG + W_V
OFF_QD = OFF_RG + W_V
OFF_KD = OFF_QD + W_D
OFF_VD = OFF_KD + W_D
OFF_AG = OFF_VD + W_D
W_IN_PADDED = OFF_AG + LANES

BF16_SUBLANES = 16
V_AUG_ROWS = DIFF_DV + BF16_SUBLANES
LOG2E = math.log2(math.e)

VMEM_LIMIT_BYTES = 56 * 1024 * 1024
GLA_SAFE_LOG_DECAY = -60.0


def _rms(x, g):
    ms = jnp.mean(x * x, axis=-1, keepdims=True)
    return x * lax.rsqrt(ms + EPS) * g


def _log_sigmoid(x):
    return jnp.minimum(x, 0.0) - jnp.log(1.0 + jnp.exp(-jnp.abs(x)))


def _div_pow2(x, d):
    assert d & (d - 1) == 0
    return lax.shift_right_arithmetic(x, d.bit_length() - 1)


def _const_spec(shape):
    zeros = (0,) * len(shape)
    return pl.BlockSpec(shape, lambda *_: zeros, pipeline_mode=pl.Buffered(1))


def _in_proj_kernel(x_ref, g_ref, w_ref, wg_ref, bg_ref,
                    qg_ref, kg_ref, vg_ref, rg_ref, gate_ref,
                    qd_ref, kd_ref, kdb_ref, vd_ref, vdb_ref, *, transposed):
    h = _rms(x_ref[...], g_ref[...]).astype(BF16)

    def proj(off, width):
        return jnp.dot(h, w_ref[:, off:off + width], preferred_element_type=F32)

    qg_ref[...] = (proj(OFF_QG, W_G) * (GLA_DK ** -0.5)).astype(BF16)
    kg_ref[...] = proj(OFF_KG, W_G).astype(BF16)
    vg_ref[...] = proj(OFF_VG, W_V).astype(BF16)
    rg_ref[...] = proj(OFF_RG, W_V).astype(BF16)
    qd = proj(OFF_QD, W_D) * (DIFF_DH ** -0.5)
    kd = proj(OFF_KD, W_D)
    vd = proj(OFF_VD, W_D)
    kd_ref[...] = kd
    vd_ref[...] = vd
    tm = qd.shape[0]
    ones_rows = (lax.broadcasted_iota(jnp.int32, (V_AUG_ROWS - DIFF_DV, tm), 0) == 0).astype(BF16)
    for hd in range(DIFF_HEADS):
        sl = slice(hd * LANES, (hd + 1) * LANES)
        kdb_ref[0, hd] = kd[:, sl].astype(BF16)
        if transposed:
            qd_ref[0, hd] = (qd[:, sl] * LOG2E).T.astype(BF16)
            vdb_ref[0, hd, 0:DIFF_DV, :] = vd[:, sl].T.astype(BF16)
            vdb_ref[0, hd, DIFF_DV:V_AUG_ROWS, :] = ones_rows
        else:
            qd_ref[0, hd] = qd[:, sl].astype(BF16)
            vdb_ref[0, hd] = vd[:, sl].astype(BF16)
    ag = proj(OFF_AG, LANES).astype(BF16)
    pre = jnp.dot(ag, wg_ref[...], preferred_element_type=F32) + bg_ref[...]
    gate_ref[...] = _log_sigmoid(pre) * (1.0 / GLA_TAU)


def _in_proj(x2d, g_pre, w_in_r, wg_pad, b_gate, *, nb, t, tm, transposed):
    n, d = x2d.shape
    steps_per_b = t // tm
    row = lambda i: (i, 0)
    hm = lambda i: (i // steps_per_b, 0, i % steps_per_b, 0)
    hm_t = lambda i: (i // steps_per_b, 0, 0, i % steps_per_b)
    hm_shape = jax.ShapeDtypeStruct((nb, DIFF_HEADS, t, LANES), BF16)
    hm_spec = pl.BlockSpec((1, DIFF_HEADS, tm, LANES), hm)
    if transposed:
        q_shape = jax.ShapeDtypeStruct((nb, DIFF_HEADS, LANES, t), BF16)
        q_spec = pl.BlockSpec((1, DIFF_HEADS, LANES, tm), hm_t)
        v_shape = jax.ShapeDtypeStruct((nb, DIFF_HEADS, V_AUG_ROWS, t), BF16)
        v_spec = pl.BlockSpec((1, DIFF_HEADS, V_AUG_ROWS, tm), hm_t)
    else:
        q_shape, q_spec, v_shape, v_spec = hm_shape, hm_spec, hm_shape, hm_spec
    out_shape = (
        jax.ShapeDtypeStruct((n, W_G), BF16), jax.ShapeDtypeStruct((n, W_G), BF16),
        jax.ShapeDtypeStruct((n, W_V), BF16), jax.ShapeDtypeStruct((n, W_V), BF16),
        jax.ShapeDtypeStruct((n, W_G), F32),
        q_shape,
        jax.ShapeDtypeStruct((n, W_D), F32), hm_shape,
        jax.ShapeDtypeStruct((n, W_D), F32), v_shape,
    )
    out_specs = (
        pl.BlockSpec((tm, W_G), row), pl.BlockSpec((tm, W_G), row),
        pl.BlockSpec((tm, W_V), row), pl.BlockSpec((tm, W_V), row),
        pl.BlockSpec((tm, W_G), row),
        q_spec,
        pl.BlockSpec((tm, W_D), row), hm_spec,
        pl.BlockSpec((tm, W_D), row), v_spec,
    )
    return pl.pallas_call(
        functools.partial(_in_proj_kernel, transposed=transposed),
        grid=(n // tm,),
        in_specs=[pl.BlockSpec((tm, d), row), _const_spec((1, d)),
                  _const_spec(w_in_r.shape), _const_spec(wg_pad.shape), _const_spec((1, W_G))],
        out_specs=out_specs,
        out_shape=out_shape,
        compiler_params=pltpu.CompilerParams(
            dimension_semantics=("arbitrary",), vmem_limit_bytes=VMEM_LIMIT_BYTES),
        name="in_proj",
    )(x2d, g_pre, w_in_r, wg_pad, b_gate)


def _gla_kernel(q_ref, k_ref, v_ref, r_ref, gate_ref, s0_ref, gout_ref,
                o_ref, sout_ref, s_scr, oi_scr, qf_scr, kf_scr, b_scr, *, L, CB):
    H, DK, DV = GLA_HEADS, GLA_DK, GLA_DV
    HL = H * L
    step = pl.program_id(1)

    @pl.when(step == 0)
    def _():
        s_scr[...] = s0_ref[0].reshape(H * DK, DV)

    tri = (lax.broadcasted_iota(jnp.int32, (L, L), 0)
           >= lax.broadcasted_iota(jnp.int32, (L, L), 1)).astype(BF16)
    ones_l = jnp.ones((L, DV), BF16)
    lane_head = _div_pow2(lax.broadcasted_iota(jnp.int32, (L, W_G), 1), DK)
    row_i = lax.broadcasted_iota(jnp.int32, (HL, HL), 0)
    col_i = lax.broadcasted_iota(jnp.int32, (HL, HL), 1)
    causal = (col_i <= row_i) & (col_i >= (row_i & ~(L - 1)))
    row_t = lax.broadcasted_iota(jnp.int32, (L, 1), 0)
    gout = gout_ref[...]

    def stack_heads(a):
        return jnp.concatenate(
            [jnp.where(lane_head == hd, a, 0.0) for hd in range(H)], axis=0)

    def chunk(c, carry):
        r0 = pl.multiple_of(c * L, L)
        rows = pl.ds(r0, L)
        g = gate_ref[rows, :]
        g_hi = g.astype(BF16)
        g_lo = (g - g_hi.astype(F32)).astype(BF16)
        b = (jnp.dot(tri, g_hi, preferred_element_type=F32)
             + jnp.dot(tri, g_lo, preferred_element_type=F32))
        b_last = b[L - 1:L, :]
        q = q_ref[rows, :].astype(F32)
        k = k_ref[rows, :].astype(F32)
        v = v_ref[rows, :]
        qs = stack_heads(q * jnp.exp(b)).astype(BF16)
        kends = stack_heads(k * jnp.exp(b_last - b)).astype(BF16)
        vs = jnp.concatenate([v[:, hd * DV:(hd + 1) * DV] for hd in range(H)], axis=0)
        s_old = s_scr[...]
        o_inter = jnp.dot(qs, s_old.astype(BF16), preferred_element_type=F32)

        safe = jnp.min(b_last) >= GLA_SAFE_LOG_DECAY

        @pl.when(safe)
        def _():
            ks = stack_heads(k * jnp.exp(-b)).astype(BF16)
            a = lax.dot_general(qs, ks, (((1,), (1,)), ((), ())), preferred_element_type=F32)
            a = jnp.where(causal, a, 0.0).astype(BF16)
            oi_scr[...] = jnp.dot(a, vs, preferred_element_type=F32)

        @pl.when(jnp.logical_not(safe))
        def _():
            qf_scr[...] = q
            kf_scr[...] = k
            b_scr[...] = b

            def tok(t, carry_t):
                b_all = b_scr[...]
                w = (qf_scr[pl.ds(t, 1), :] * kf_scr[...]
                     * jnp.exp(jnp.minimum(b_scr[pl.ds(t, 1), :] - b_all, 0.0)))
                for hd in range(H):
                    a_col = jnp.sum(jnp.where(lane_head == hd, w, 0.0), axis=-1, keepdims=True)
                    a_col = jnp.where(row_t <= t, a_col, 0.0)
                    vh = v_ref[rows, hd * DV:(hd + 1) * DV].astype(F32)
                    oi_scr[pl.ds(hd * L + t, 1), :] = jnp.sum(a_col * vh, axis=0, keepdims=True)
                return carry_t

            lax.fori_loop(0, L, tok, 0)

        o = o_inter + oi_scr[...]
        o = _rms(o, gout)
        r = r_ref[rows, :].astype(F32)
        for hd in range(H):
            rh = r[:, hd * DV:(hd + 1) * DV]
            o_ref[rows, hd * DV:(hd + 1) * DV] = (
                o[hd * L:(hd + 1) * L, :] * (rh * jax.nn.sigmoid(rh))).astype(BF16)

        tdot = (((0,), (0,)), ((), ()))
        kv = lax.dot_general(kends, vs, tdot, preferred_element_type=F32)
        dec = (lax.dot_general(g_hi, ones_l, tdot, preferred_element_type=F32)
               + lax.dot_general(g_lo, ones_l, tdot, preferred_element_type=F32))
        s_scr[...] = jnp.exp(dec) * s_old + kv
        return carry

    lax.fori_loop(0, CB, chunk, 0)

    @pl.when(step == pl.num_programs(1) - 1)
    def _():
        sout_ref[0] = s_scr[...].reshape(H, DK, DV)


def _gla(qg, kg, vg, rg, gate, s0, g_out, *, nb, t, L, CB):
    n = qg.shape[0]
    tm = L * CB
    steps = t // tm
    row = lambda b, s: (b * steps + s, 0)
    st = lambda b, s: (b, 0, 0, 0)
    state_shape = (1, GLA_HEADS, GLA_DK, GLA_DV)
    return pl.pallas_call(
        functools.partial(_gla_kernel, L=L, CB=CB),
        grid=(nb, steps),
        in_specs=[pl.BlockSpec((tm, W_G), row), pl.BlockSpec((tm, W_G), row),
                  pl.BlockSpec((tm, W_V), row), pl.BlockSpec((tm, W_V), row),
                  pl.BlockSpec((tm, W_G), row), pl.BlockSpec(state_shape, st),
                  pl.BlockSpec((1, GLA_DV), lambda b, s: (0, 0))],
        out_specs=(pl.BlockSpec((tm, W_V), row), pl.BlockSpec(state_shape, st)),
        out_shape=(jax.ShapeDtypeStruct((n, W_V), BF16),
                   jax.ShapeDtypeStruct((nb,) + state_shape[1:], F32)),
        scratch_shapes=[pltpu.VMEM((GLA_HEADS * GLA_DK, GLA_DV), F32),
                        pltpu.VMEM((GLA_HEADS * L, GLA_DV), F32),
                        pltpu.VMEM((L, W_G), F32), pltpu.VMEM((L, W_G), F32),
                        pltpu.VMEM((L, W_G), F32)],
        compiler_params=pltpu.CompilerParams(
            dimension_semantics=("arbitrary", "arbitrary"), vmem_limit_bytes=VMEM_LIMIT_BYTES),
        name="gla",
    )(qg, kg, vg, rg, gate, s0, g_out)


def _lambda(lam_ref, lam_init):
    lp = lam_ref[...]
    s1 = jnp.sum(lp[0:1, :] * lp[1:2, :], axis=-1, keepdims=True)
    s2 = jnp.sum(lp[2:3, :] * lp[3:4, :], axis=-1, keepdims=True)
    return jnp.exp(s1) - jnp.exp(s2) + lam_init


def _split_halves(q):
    lane = lax.broadcasted_iota(jnp.int32, q.shape, 1)
    zero = jnp.zeros_like(q)
    return jnp.concatenate(
        [jnp.where(lane < DIFF_DH, q, zero), jnp.where(lane >= DIFF_DH, q, zero)], axis=0)


def _finish_attention(acc, l, lam, gsub, lam_init, tq):
    o = acc[:tq] / l[:tq] - lam * (acc[tq:] / l[tq:])
    return _rms(o, gsub) * (1.0 - lam_init)


def _split3_bf16(x):
    hi = x.astype(BF16).astype(F32)
    r = x - hi
    mid = r.astype(BF16).astype(F32)
    lo = (r - mid).astype(BF16).astype(F32)
    return hi, mid, lo


def _attn_prompt_kernel(slopes_ref, lam_ref, qt_ref, k_ref, vt_ref, gsub_ref, o_ref,
                        aug_scr, qt_scr, corr_scr, acc_scr, m_scr, *, tq, lam_init):
    nh = DIFF_HEADS
    tk = tq
    t = k_ref.shape[2]
    nq = t // tq
    slope2 = [slopes_ref[hd] * LOG2E for hd in range(nh)]
    j_lane = 3 * nh

    lane = lax.broadcasted_iota(jnp.int32, (tk, LANES), 1)
    k_loc = lax.broadcasted_iota(jnp.int32, (tk, LANES), 0).astype(F32)
    base = jnp.zeros((tk, LANES), F32)
    for hd in range(nh):
        for i, part in enumerate(_split3_bf16(slope2[hd] * k_loc)):
            base = jnp.where(lane == 3 * hd + i, part, base)
    blk_lanes = (lane >= j_lane) & (lane < j_lane + 3)
    for j in range(t // tk):
        aug_scr[j * tk:(j + 1) * tk, :] = jnp.where(blk_lanes, float(j), base).astype(BF16)

    row = lax.broadcasted_iota(jnp.int32, (LANES, 2 * tq), 0)
    kl = lax.broadcasted_iota(jnp.int32, (tk, tq), 0)
    ql = lax.broadcasted_iota(jnp.int32, (tk, tq), 1)
    same_chunk_or_earlier = _div_pow2(kl, CHUNK) <= _div_pow2(ql, CHUNK)
    ahead = jnp.maximum(kl - ql, 0).astype(F32)
    for hd in range(nh):
        rows = jnp.where((row >= 3 * hd) & (row < 3 * hd + 3), 1.0, 0.0)
        for i, part in enumerate(_split3_bf16(jnp.full((LANES, 2 * tq), slope2[hd] * tk, F32))):
            rows = jnp.where(row == j_lane + i, part, rows)
        qt_scr[hd, LANES:2 * LANES, :] = rows.astype(BF16)
        qt_scr[hd, DIFF_DH:2 * DIFF_DH, 0:tq] = jnp.zeros((DIFF_DH, tq), BF16)
        qt_scr[hd, 0:DIFF_DH, tq:2 * tq] = jnp.zeros((DIFF_DH, tq), BF16)
        corr_scr[hd] = jnp.where(same_chunk_or_earlier, (-2.0 * slope2[hd]) * ahead, -jnp.inf)

    lam = _lambda(lam_ref, lam_init)
    gsub = gsub_ref[...]

    def kblock(j, diag):
        ks = pl.ds(pl.multiple_of(j * tk, tk), tk)
        aug = aug_scr[ks, :]

        def scores(hd):
            k_aug = jnp.concatenate([k_ref[0, hd, ks, :], aug], axis=1)
            return jnp.dot(k_aug, qt_scr[hd], preferred_element_type=F32)

        def softmax(hd, s):
            if diag:
                corr = corr_scr[hd]
                s = s + jnp.concatenate([corr, corr], axis=1)
            m_old = m_scr[hd:hd + 1, :]
            m_new = jnp.maximum(m_old, jnp.max(s, axis=0, keepdims=True))
            m_scr[hd:hd + 1, :] = m_new
            return jnp.exp2(s - m_new).astype(BF16), jnp.exp2(m_old - m_new)

        def accumulate(hd, p, alpha):
            acc_scr[hd] = alpha * acc_scr[hd] + jnp.dot(
                vt_ref[0, hd, :, ks], p, preferred_element_type=F32)

        s_next = scores(0)
        pending = None
        for hd in range(nh):
            s_cur = s_next
            if hd + 1 < nh:
                s_next = scores(hd + 1)
            p_alpha = softmax(hd, s_cur)
            if pending is not None:
                accumulate(hd - 1, *pending)
            pending = p_alpha
        accumulate(nh - 1, *pending)

    def qblock(qi, carry):
        qs = pl.ds(pl.multiple_of(qi * tq, tq), tq)
        for hd in range(nh):
            qt = qt_ref[0, hd, :, qs]
            qt_scr[hd, 0:DIFF_DH, 0:tq] = qt[0:DIFF_DH, :]
            qt_scr[hd, DIFF_DH:2 * DIFF_DH, tq:2 * tq] = qt[DIFF_DH:2 * DIFF_DH, :]
        m_scr[...] = jnp.full(m_scr.shape, -jnp.inf, F32)
        acc_scr[...] = jnp.zeros(acc_scr.shape, F32)

        def off_diag(j, c):
            kblock(j, False)
            return c

        lax.fori_loop(0, qi, off_diag, 0)
        kblock(qi, True)

        for hd in range(nh):
            acc = acc_scr[hd]
            inv_l = 1.0 / acc[DIFF_DV:DIFF_DV + 1, :]
            o_t = (acc[0:DIFF_DV, 0:tq] * inv_l[:, 0:tq]
                   - lam * (acc[0:DIFF_DV, tq:2 * tq] * inv_l[:, tq:2 * tq]))
            ms = jnp.mean(o_t * o_t, axis=0, keepdims=True)
            o_t = o_t * lax.rsqrt(ms + EPS)
            o_ref[qs, hd * DIFF_DV:(hd + 1) * DIFF_DV] = (
                o_t.T * gsub * (1.0 - lam_init)).astype(BF16)
        return carry

    lax.fori_loop(0, nq, qblock, 0)


def _attn_prompt(slopes, lam_p, qdt, kdb, vdt, g_sub, *, tq, lam_init):
    nb, nh, t, _ = kdb.shape
    per_stream = lambda b, *_: (b, 0, 0, 0)
    grid_spec = pltpu.PrefetchScalarGridSpec(
        num_scalar_prefetch=1,
        grid=(nb,),
        in_specs=[pl.BlockSpec((4, DIFF_DH), lambda b, *_: (0, 0)),
                  pl.BlockSpec((1, nh, LANES, t), per_stream),
                  pl.BlockSpec((1, nh, t, LANES), per_stream),
                  pl.BlockSpec((1, nh, V_AUG_ROWS, t), per_stream),
                  pl.BlockSpec((1, DIFF_DV), lambda b, *_: (0, 0))],
        out_specs=pl.BlockSpec((t, nh * DIFF_DV), lambda b, *_: (b, 0)),
        scratch_shapes=[pltpu.VMEM((t, LANES), BF16),
                        pltpu.VMEM((nh, 2 * LANES, 2 * tq), BF16),
                        pltpu.VMEM((nh, tq, tq), F32),
                        pltpu.VMEM((nh, V_AUG_ROWS, 2 * tq), F32),
                        pltpu.VMEM((8, 2 * tq), F32)],
    )
    return pl.pallas_call(
        functools.partial(_attn_prompt_kernel, tq=tq, lam_init=lam_init),
        grid_spec=grid_spec,
        out_shape=jax.ShapeDtypeStruct((nb * t, nh * DIFF_DV), BF16),
        compiler_params=pltpu.CompilerParams(
            dimension_semantics=("arbitrary",), vmem_limit_bytes=VMEM_LIMIT_BYTES),
        name="attn_prompt",
    )(slopes, lam_p, qdt, kdb, vdt, g_sub)


def _attn_sample_kernel(slopes_ref, lam_ref, q_ref, ck_ref, cv_ref, kn_ref, vn_ref, gsub_ref, o_ref,
                        *, tq, past, lam_init):
    nt = (((1,), (1,)), ((), ()))
    k_pos = lax.broadcasted_iota(jnp.int32, (1, past), 1).astype(F32)
    qi = lax.broadcasted_iota(jnp.int32, (tq, tq), 0)
    kj = lax.broadcasted_iota(jnp.int32, (tq, tq), 1)
    rel_new = (past + qi - jnp.abs(qi - kj)).astype(F32)
    lam = _lambda(lam_ref, lam_init)
    gsub = gsub_ref[...]
    for hd in range(DIFF_HEADS):
        sl = slice(hd * LANES, (hd + 1) * LANES)
        slope = slopes_ref[hd]
        qq = _split_halves(q_ref[0, hd])
        s_c = lax.dot_general(qq, ck_ref[0, :, sl].astype(BF16), nt, preferred_element_type=F32)
        s_c = s_c + slope * k_pos
        s_n = lax.dot_general(qq, kn_ref[:, sl].astype(BF16), nt, preferred_element_type=F32)
        bias_n = slope * rel_new
        s_n = s_n + jnp.concatenate([bias_n, bias_n], axis=0)
        m = jnp.maximum(jnp.max(s_c, axis=-1, keepdims=True), jnp.max(s_n, axis=-1, keepdims=True))
        p_c = jnp.exp(s_c - m)
        p_n = jnp.exp(s_n - m)
        l = jnp.sum(p_c, axis=-1, keepdims=True) + jnp.sum(p_n, axis=-1, keepdims=True)
        acc = (jnp.dot(p_c.astype(BF16), cv_ref[0, :, sl].astype(BF16), preferred_element_type=F32)
               + jnp.dot(p_n.astype(BF16), vn_ref[:, sl].astype(BF16), preferred_element_type=F32))
        o_ref[:, sl] = _finish_attention(acc, l, lam, gsub, lam_init, tq).astype(BF16)


def _attn_sample(slopes, lam_p, qd, cache_k, cache_v, k_new, v_new, g_sub, *, lam_init):
    nb, past, width = cache_k.shape
    tq = k_new.shape[0] // nb
    nh = DIFF_HEADS
    grid_spec = pltpu.PrefetchScalarGridSpec(
        num_scalar_prefetch=1,
        grid=(nb,),
        in_specs=[pl.BlockSpec((4, DIFF_DH), lambda b, *_: (0, 0)),
                  pl.BlockSpec((1, nh, tq, LANES), lambda b, *_: (0, 0, b, 0)),
                  pl.BlockSpec((1, past, width), lambda b, *_: (b, 0, 0)),
                  pl.BlockSpec((1, past, width), lambda b, *_: (b, 0, 0)),
                  pl.BlockSpec((tq, width), lambda b, *_: (b, 0)),
                  pl.BlockSpec((tq, width), lambda b, *_: (b, 0)),
                  pl.BlockSpec((1, DIFF_DV), lambda b, *_: (0, 0))],
        out_specs=pl.BlockSpec((tq, nh * DIFF_DV), lambda b, *_: (b, 0)),
    )
    return pl.pallas_call(
        functools.partial(_attn_sample_kernel, tq=tq, past=past, lam_init=lam_init),
        grid_spec=grid_spec,
        out_shape=jax.ShapeDtypeStruct((nb * tq, nh * DIFF_DV), BF16),
        compiler_params=pltpu.CompilerParams(
            dimension_semantics=("arbitrary",), vmem_limit_bytes=VMEM_LIMIT_BYTES),
        name="attn_sample",
    )(slopes, lam_p, qd, cache_k, cache_v, k_new, v_new, g_sub)


def _out_ffn_kernel(og_ref, od_ref, x_ref, wo_ref, wu_ref, wd_ref,
                    gpm_ref, gpf_ref, gqf_ref, y_ref, *, ff_chunk):
    mix = (jnp.dot(og_ref[...], wo_ref[:W_V, :], preferred_element_type=F32)
           + jnp.dot(od_ref[...], wo_ref[W_V:, :], preferred_element_type=F32))
    x1 = x_ref[...] + _rms(mix, gpm_ref[...])
    f = _rms(x1, gpf_ref[...]).astype(BF16)
    d_ff = wu_ref.shape[1]
    acc = jnp.zeros(x1.shape, F32)
    for c in range(d_ff // ff_chunk):
        sl = slice(c * ff_chunk, (c + 1) * ff_chunk)
        hid = jnp.dot(f, wu_ref[:, sl], preferred_element_type=F32)
        hid = jnp.square(jnp.maximum(hid, 0.0)).astype(BF16)
        acc = acc + jnp.dot(hid, wd_ref[sl, :], preferred_element_type=F32)
    y_ref[...] = x1 + _rms(acc, gqf_ref[...])


def _out_ffn(og, od, x2d, w_out, w_up, w_down, g_post_mix, g_pre_ffn, g_post_ffn, *, tm):
    n, d = x2d.shape
    row = lambda i: (i, 0)
    return pl.pallas_call(
        functools.partial(_out_ffn_kernel, ff_chunk=1024),
        grid=(n // tm,),
        in_specs=[pl.BlockSpec((tm, W_V), row), pl.BlockSpec((tm, W_V), row),
                  pl.BlockSpec((tm, d), row),
                  _const_spec(w_out.shape), _const_spec(w_up.shape), _const_spec(w_down.shape),
                  _const_spec((1, d)), _const_spec((1, d)), _const_spec((1, d))],
        out_specs=pl.BlockSpec((tm, d), row),
        out_shape=jax.ShapeDtypeStruct((n, d), F32),
        compiler_params=pltpu.CompilerParams(
            dimension_semantics=("arbitrary",), vmem_limit_bytes=VMEM_LIMIT_BYTES),
        name="out_ffn",
    )(og, od, x2d, w_out, w_up, w_down, g_post_mix, g_pre_ffn, g_post_ffn)


def _reorder_w_in(w):
    d = w.shape[0]
    a0 = 2 * W_G + 2 * W_V
    a1 = a0 + GLA_RANK
    return jnp.concatenate(
        [w[:, :a0], w[:, a1:], w[:, a0:a1], jnp.zeros((d, LANES - GLA_RANK), w.dtype)],
        axis=1).astype(BF16)


def kernel(x_prompt, x_sample, cache_k, cache_v, state_gla, w_in, w_gate_up, b_gate, g_gla_out,
           lam_q1, lam_k1, lam_q2, lam_k2, g_subln, w_out, g_pre_mix, g_post_mix,
           g_pre_ffn, g_post_ffn, w_ff_up, w_ff_down):
    depth = w_in.shape[0]
    nb_p, t_p, d = x_prompt.shape
    nb_s, t_s, _ = x_sample.shape
    past = cache_k.shape[2]
    slopes = jnp.exp2(-8.0 / DIFF_HEADS * jnp.arange(1, DIFF_HEADS + 1, dtype=F32))
    yp = x_prompt.reshape(nb_p * t_p, d)
    ys = x_sample.reshape(nb_s * t_s, d)
    outs = [[] for _ in range(6)]
    for l in range(depth):
        lam_init = 0.8 - 0.6 * math.exp(-0.3 * l)
        w_in_r = _reorder_w_in(w_in[l])
        wg_pad = jnp.concatenate(
            [w_gate_up[l], jnp.zeros((LANES - GLA_RANK, W_G), F32)], axis=0).astype(BF16)
        bg = b_gate[l][None, :]
        lam_p = jnp.stack([lam_q1[l], lam_k1[l], lam_q2[l], lam_k2[l]], axis=0)
        g_out = g_gla_out[l][None, :]
        g_sub = g_subln[l][None, :]
        wo = w_out[l].astype(BF16)
        wu = w_ff_up[l].astype(BF16)
        wd = w_ff_down[l].astype(BF16)
        gains = (g_post_mix[l][None, :], g_pre_ffn[l][None, :], g_post_ffn[l][None, :])
        g_pre = g_pre_mix[l][None, :]

        qg, kg, vg, rg, gate, qd, kd, kdb, vd, vdb = _in_proj(
            yp, g_pre, w_in_r, wg_pad, bg, nb=nb_p, t=t_p, tm=512, transposed=True)
        s0 = jnp.zeros((nb_p, GLA_HEADS, GLA_DK, GLA_DV), F32)
        og, s_p = _gla(qg, kg, vg, rg, gate, s0, g_out, nb=nb_p, t=t_p, L=CHUNK, CB=8)
        od = _attn_prompt(slopes, lam_p, qd, kdb, vdb, g_sub, tq=256, lam_init=lam_init)
        yp = _out_ffn(og, od, yp, wo, wu, wd, *gains, tm=512)
        outs[0].append(kd.reshape(nb_p, t_p, DIFF_HEADS, 2 * DIFF_DH))
        outs[1].append(vd.reshape(nb_p, t_p, DIFF_HEADS, DIFF_DV))
        outs[2].append(s_p)

        n_s = nb_s * t_s
        qg, kg, vg, rg, gate, qd, kd, _, vd, _ = _in_proj(
            ys, g_pre, w_in_r, wg_pad, bg, nb=1, t=n_s, tm=n_s, transposed=False)
        og, s_s = _gla(qg, kg, vg, rg, gate, state_gla[l], g_out, nb=nb_s, t=t_s, L=t_s, CB=1)
        od = _attn_sample(slopes, lam_p, qd,
                          cache_k[l].reshape(nb_s, past, W_D), cache_v[l].reshape(nb_s, past, W_D),
                          kd, vd, g_sub, lam_init=lam_init)
        ys = _out_ffn(og, od, ys, wo, wu, wd, *gains, tm=n_s)
        outs[3].append(kd.reshape(nb_s, t_s, DIFF_HEADS, 2 * DIFF_DH))
        outs[4].append(vd.reshape(nb_s, t_s, DIFF_HEADS, DIFF_DV))
        outs[5].append(s_s)

    stack = lambda xs: jnp.stack(xs, axis=0)
    return (yp.reshape(nb_p, t_p, d), ys.reshape(nb_s, t_s, d),
            stack(outs[0]), stack(outs[1]), stack(outs[2]),
            stack(outs[3]), stack(outs[4]), stack(outs[5]))
```

```python
import functools
import math

import jax
import jax.numpy as jnp
from jax import lax
from jax.experimental import pallas as pl
from jax.experimental.pallas import tpu as pltpu

F32 = jnp.float32
BF16 = jnp.bfloat16

EPS = 1e-6
CHUNK = 64
GLA_HEADS = 4
GLA_DK = 64
GLA_DV = 128
GLA_RANK = 16
GLA_TAU = 16.0
DIFF_HEADS = 4
DIFF_DH = 64
DIFF_DV = 128

W_G = GLA_HEADS * GLA_DK
W_V = GLA_HEADS * GLA_DV
W_D = DIFF_HEADS * 2 * DIFF_DH
LANES = 128
OFF_QG = 0
OFF_KG = OFF_QG + W_G
OFF_VG = OFF_KG + W_G
OFF_RG = OFF_VG + W_V
OFF_QD = OFF_RG + W_V
OFF_KD = OFF_QD + W_D
OFF_VD = OFF_KD + W_D
OFF_AG = OFF_VD + W_D
W_IN_PADDED = OFF_AG + LANES

BF16_SUBLANES = 16
V_AUG_ROWS = DIFF_DV + BF16_SUBLANES
LOG2E = math.log2(math.e)

VMEM_LIMIT_BYTES = 56 * 1024 * 1024
GLA_SAFE_LOG_DECAY = -60.0


def _rms(x, g):
    ms = jnp.mean(x * x, axis=-1, keepdims=True)
    return x * lax.rsqrt(ms + EPS) * g


def _log_sigmoid(x):
    return jnp.minimum(x, 0.0) - jnp.log(1.0 + jnp.exp(-jnp.abs(x)))


def _div_pow2(x, d):
    assert d & (d - 1) == 0
    return lax.shift_right_arithmetic(x, d.bit_length() - 1)


def _const_spec(shape):
    zeros = (0,) * len(shape)
    return pl.BlockSpec(shape, lambda *_: zeros, pipeline_mode=pl.Buffered(1))


def _in_proj_kernel(x_ref, g_ref, w_ref, wg_ref, bg_ref,
                    qg_ref, kg_ref, vg_ref, rg_ref, gate_ref,
                    qd_ref, kd_ref, kdb_ref, vd_ref, vdb_ref, *, transposed):
    h = _rms(x_ref[...], g_ref[...]).astype(BF16)

    def proj(off, width):
        return jnp.dot(h, w_ref[:, off:off + width], preferred_element_type=F32)

    qg_ref[...] = (proj(OFF_QG, W_G) * (GLA_DK ** -0.5)).astype(BF16)
    kg_ref[...] = proj(OFF_KG, W_G).astype(BF16)
    vg_ref[...] = proj(OFF_VG, W_V).astype(BF16)
    rg_ref[...] = proj(OFF_RG, W_V).astype(BF16)
    qd = proj(OFF_QD, W_D) * (DIFF_DH ** -0.5)
    kd = proj(OFF_KD, W_D)
    vd = proj(OFF_VD, W_D)
    tm = qd.shape[0]
    for hd in range(DIFF_HEADS):
        head_rows = pl.ds(hd, tm, stride=DIFF_HEADS)
        kd_ref[0, head_rows, :] = kd[:, hd * LANES:(hd + 1) * LANES]
        vd_ref[0, head_rows, :] = vd[:, hd * LANES:(hd + 1) * LANES]
    ones_rows = (lax.broadcasted_iota(jnp.int32, (V_AUG_ROWS - DIFF_DV, tm), 0) == 0).astype(BF16)
    for hd in range(DIFF_HEADS):
        sl = slice(hd * LANES, (hd + 1) * LANES)
        kdb_ref[0, hd] = kd[:, sl].astype(BF16)
        if transposed:
            qd_ref[0, hd] = (qd[:, sl] * LOG2E).T.astype(BF16)
            vdb_ref[0, hd, 0:DIFF_DV, :] = vd[:, sl].T.astype(BF16)
            vdb_ref[0, hd, DIFF_DV:V_AUG_ROWS, :] = ones_rows
        else:
            qd_ref[0, hd] = qd[:, sl].astype(BF16)
            vdb_ref[0, hd] = vd[:, sl].astype(BF16)
    ag = proj(OFF_AG, LANES).astype(BF16)
    pre = jnp.dot(ag, wg_ref[...], preferred_element_type=F32) + bg_ref[...]
    gate_ref[...] = _log_sigmoid(pre) * (1.0 / GLA_TAU)


def _in_proj(x2d, g_pre, w_in_r, wg_pad, b_gate, *, nb, t, tm, transposed):
    n, d = x2d.shape
    steps_per_b = t // tm
    row = lambda i: (i, 0)
    hm = lambda i: (i // steps_per_b, 0, i % steps_per_b, 0)
    hm_t = lambda i: (i // steps_per_b, 0, 0, i % steps_per_b)
    hm_shape = jax.ShapeDtypeStruct((nb, DIFF_HEADS, t, LANES), BF16)
    hm_spec = pl.BlockSpec((1, DIFF_HEADS, tm, LANES), hm)
    kv_shape = jax.ShapeDtypeStruct((nb, t * DIFF_HEADS, LANES), F32)
    kv_spec = pl.BlockSpec((1, tm * DIFF_HEADS, LANES),
                           lambda i: (i // steps_per_b, i % steps_per_b, 0))
    if transposed:
        q_shape = jax.ShapeDtypeStruct((nb, DIFF_HEADS, LANES, t), BF16)
        q_spec = pl.BlockSpec((1, DIFF_HEADS, LANES, tm), hm_t)
        v_shape = jax.ShapeDtypeStruct((nb, DIFF_HEADS, V_AUG_ROWS, t), BF16)
        v_spec = pl.BlockSpec((1, DIFF_HEADS, V_AUG_ROWS, tm), hm_t)
    else:
        q_shape, q_spec, v_shape, v_spec = hm_shape, hm_spec, hm_shape, hm_spec
    out_shape = (
        jax.ShapeDtypeStruct((n, W_G), BF16), jax.ShapeDtypeStruct((n, W_G), BF16),
        jax.ShapeDtypeStruct((n, W_V), BF16), jax.ShapeDtypeStruct((n, W_V), BF16),
        jax.ShapeDtypeStruct((n, W_G), F32),
        q_shape,
        kv_shape, hm_shape,
        kv_shape, v_shape,
    )
    out_specs = (
        pl.BlockSpec((tm, W_G), row), pl.BlockSpec((tm, W_G), row),
        pl.BlockSpec((tm, W_V), row), pl.BlockSpec((tm, W_V), row),
        pl.BlockSpec((tm, W_G), row),
        q_spec,
        kv_spec, hm_spec,
        kv_spec, v_spec,
    )
    return pl.pallas_call(
        functools.partial(_in_proj_kernel, transposed=transposed),
        grid=(n // tm,),
        in_specs=[pl.BlockSpec((tm, d), row), _const_spec((1, d)),
                  _const_spec(w_in_r.shape), _const_spec(wg_pad.shape), _const_spec((1, W_G))],
        out_specs=out_specs,
        out_shape=out_shape,
        compiler_params=pltpu.CompilerParams(
            dimension_semantics=("arbitrary",), vmem_limit_bytes=VMEM_LIMIT_BYTES),
        name="in_proj",
    )(x2d, g_pre, w_in_r, wg_pad, b_gate)


def _gla_kernel(q_ref, k_ref, v_ref, r_ref, gate_ref, s0_ref, gout_ref,
                o_ref, sout_ref, s_scr, oi_scr, qf_scr, kf_scr, b_scr, *, L, CB):
    H, DK, DV = GLA_HEADS, GLA_DK, GLA_DV
    HL = H * L
    step = pl.program_id(1)

    @pl.when(step == 0)
    def _():
        s_scr[...] = s0_ref[0].reshape(H * DK, DV)

    tri = (lax.broadcasted_iota(jnp.int32, (L, L), 0)
           >= lax.broadcasted_iota(jnp.int32, (L, L), 1)).astype(BF16)
    ones_l = jnp.ones((L, DV), BF16)
    lane_head = _div_pow2(lax.broadcasted_iota(jnp.int32, (L, W_G), 1), DK)
    row_i = lax.broadcasted_iota(jnp.int32, (HL, HL), 0)
    col_i = lax.broadcasted_iota(jnp.int32, (HL, HL), 1)
    causal = (col_i <= row_i) & (col_i >= (row_i & ~(L - 1)))
    row_t = lax.broadcasted_iota(jnp.int32, (L, 1), 0)
    gout = gout_ref[...]

    def stack_heads(a):
        return jnp.concatenate(
            [jnp.where(lane_head == hd, a, 0.0) for hd in range(H)], axis=0)

    def chunk(c, carry):
        r0 = pl.multiple_of(c * L, L)
        rows = pl.ds(r0, L)
        g = gate_ref[rows, :]
        g_hi = g.astype(BF16)
        g_lo = (g - g_hi.astype(F32)).astype(BF16)
        b = (jnp.dot(tri, g_hi, preferred_element_type=F32)
             + jnp.dot(tri, g_lo, preferred_element_type=F32))
        b_last = b[L - 1:L, :]
        q = q_ref[rows, :].astype(F32)
        k = k_ref[rows, :].astype(F32)
        v = v_ref[rows, :]
        qs = stack_heads(q * jnp.exp(b)).astype(BF16)
        kends = stack_heads(k * jnp.exp(b_last - b)).astype(BF16)
        vs = jnp.concatenate([v[:, hd * DV:(hd + 1) * DV] for hd in range(H)], axis=0)
        s_old = s_scr[...]
        o_inter = jnp.dot(qs, s_old.astype(BF16), preferred_element_type=F32)

        safe = jnp.min(b_last) >= GLA_SAFE_LOG_DECAY

        @pl.when(safe)
        def _():
            ks = stack_heads(k * jnp.exp(-b)).astype(BF16)
            a = lax.dot_general(qs, ks, (((1,), (1,)), ((), ())), preferred_element_type=F32)
            a = jnp.where(causal, a, 0.0).astype(BF16)
            oi_scr[...] = jnp.dot(a, vs, preferred_element_type=F32)

        @pl.when(jnp.logical_not(safe))
        def _():
            qf_scr[...] = q
            kf_scr[...] = k
            b_scr[...] = b

            def tok(t, carry_t):
                b_all = b_scr[...]
                w = (qf_scr[pl.ds(t, 1), :] * kf_scr[...]
                     * jnp.exp(jnp.minimum(b_scr[pl.ds(t, 1), :] - b_all, 0.0)))
                for hd in range(H):
                    a_col = jnp.sum(jnp.where(lane_head == hd, w, 0.0), axis=-1, keepdims=True)
                    a_col = jnp.where(row_t <= t, a_col, 0.0)
                    vh = v_ref[rows, hd * DV:(hd + 1) * DV].astype(F32)
                    oi_scr[pl.ds(hd * L + t, 1), :] = jnp.sum(a_col * vh, axis=0, keepdims=True)
                return carry_t

            lax.fori_loop(0, L, tok, 0)

        o = o_inter + oi_scr[...]
        o = _rms(o, gout)
        r = r_ref[rows, :].astype(F32)
        for hd in range(H):
            rh = r[:, hd * DV:(hd + 1) * DV]
            o_ref[rows, hd * DV:(hd + 1) * DV] = (
                o[hd * L:(hd + 1) * L, :] * (rh * jax.nn.sigmoid(rh))).astype(BF16)

        tdot = (((0,), (0,)), ((), ()))
        kv = lax.dot_general(kends, vs, tdot, preferred_element_type=F32)
        dec = (lax.dot_general(g_hi, ones_l, tdot, preferred_element_type=F32)
               + lax.dot_general(g_lo, ones_l, tdot, preferred_element_type=F32))
        s_scr[...] = jnp.exp(dec) * s_old + kv
        return carry

    lax.fori_loop(0, CB, chunk, 0)

    @pl.when(step == pl.num_programs(1) - 1)
    def _():
        sout_ref[0] = s_scr[...].reshape(H, DK, DV)


def _gla(qg, kg, vg, rg, gate, s0, g_out, *, nb, t, L, CB):
    n = qg.shape[0]
    tm = L * CB
    steps = t // tm
    row = lambda b, s: (b * steps + s, 0)
    st = lambda b, s: (b, 0, 0, 0)
    state_shape = (1, GLA_HEADS, GLA_DK, GLA_DV)
    return pl.pallas_call(
        functools.partial(_gla_kernel, L=L, CB=CB),
        grid=(nb, steps),
        in_specs=[pl.BlockSpec((tm, W_G), row), pl.BlockSpec((tm, W_G), row),
                  pl.BlockSpec((tm, W_V), row), pl.BlockSpec((tm, W_V), row),
                  pl.BlockSpec((tm, W_G), row), pl.BlockSpec(state_shape, st),
                  pl.BlockSpec((1, GLA_DV), lambda b, s: (0, 0))],
        out_specs=(pl.BlockSpec((tm, W_V), row), pl.BlockSpec(state_shape, st)),
        out_shape=(jax.ShapeDtypeStruct((n, W_V), BF16),
                   jax.ShapeDtypeStruct((nb,) + state_shape[1:], F32)),
        scratch_shapes=[pltpu.VMEM((GLA_HEADS * GLA_DK, GLA_DV), F32),
                        pltpu.VMEM((GLA_HEADS * L, GLA_DV), F32),
                        pltpu.VMEM((L, W_G), F32), pltpu.VMEM((L, W_G), F32),
                        pltpu.VMEM((L, W_G), F32)],
        compiler_params=pltpu.CompilerParams(
            dimension_semantics=("arbitrary", "arbitrary"), vmem_limit_bytes=VMEM_LIMIT_BYTES),
        name="gla",
    )(qg, kg, vg, rg, gate, s0, g_out)


def _lambda(lam_ref, lam_init):
    lp = lam_ref[...]
    s1 = jnp.sum(lp[0:1, :] * lp[1:2, :], axis=-1, keepdims=True)
    s2 = jnp.sum(lp[2:3, :] * lp[3:4, :], axis=-1, keepdims=True)
    return jnp.exp(s1) - jnp.exp(s2) + lam_init


def _split_halves(q):
    lane = lax.broadcasted_iota(jnp.int32, q.shape, 1)
    zero = jnp.zeros_like(q)
    return jnp.concatenate(
        [jnp.where(lane < DIFF_DH, q, zero), jnp.where(lane >= DIFF_DH, q, zero)], axis=0)


def _finish_attention(acc, l, lam, gsub, lam_init, tq):
    o = acc[:tq] / l[:tq] - lam * (acc[tq:] / l[tq:])
    return _rms(o, gsub) * (1.0 - lam_init)


def _split3_bf16(x):
    hi = x.astype(BF16).astype(F32)
    r = x - hi
    mid = r.astype(BF16).astype(F32)
    lo = (r - mid).astype(BF16).astype(F32)
    return hi, mid, lo


def _attn_prompt_kernel(slopes_ref, lam_ref, qt_ref, k_ref, vt_ref, gsub_ref, o_ref,
                        aug_scr, qt_scr, corr_scr, acc_scr, m_scr, *, tq, lam_init):
    nh = DIFF_HEADS
    tk = tq
    t = k_ref.shape[2]
    nq = t // tq
    slope2 = [slopes_ref[hd] * LOG2E for hd in range(nh)]
    j_lane = 3 * nh

    lane = lax.broadcasted_iota(jnp.int32, (tk, LANES), 1)
    k_loc = lax.broadcasted_iota(jnp.int32, (tk, LANES), 0).astype(F32)
    base = jnp.zeros((tk, LANES), F32)
    for hd in range(nh):
        for i, part in enumerate(_split3_bf16(slope2[hd] * k_loc)):
            base = jnp.where(lane == 3 * hd + i, part, base)
    blk_lanes = (lane >= j_lane) & (lane < j_lane + 3)
    for j in range(t // tk):
        aug_scr[j * tk:(j + 1) * tk, :] = jnp.where(blk_lanes, float(j), base).astype(BF16)

    row = lax.broadcasted_iota(jnp.int32, (LANES, 2 * tq), 0)
    kl = lax.broadcasted_iota(jnp.int32, (tk, tq), 0)
    ql = lax.broadcasted_iota(jnp.int32, (tk, tq), 1)
    same_chunk_or_earlier = _div_pow2(kl, CHUNK) <= _div_pow2(ql, CHUNK)
    ahead = jnp.maximum(kl - ql, 0).astype(F32)
    for hd in range(nh):
        rows = jnp.where((row >= 3 * hd) & (row < 3 * hd + 3), 1.0, 0.0)
        for i, part in enumerate(_split3_bf16(jnp.full((LANES, 2 * tq), slope2[hd] * tk, F32))):
            rows = jnp.where(row == j_lane + i, part, rows)
        qt_scr[hd, LANES:2 * LANES, :] = rows.astype(BF16)
        qt_scr[hd, DIFF_DH:2 * DIFF_DH, 0:tq] = jnp.zeros((DIFF_DH, tq), BF16)
        qt_scr[hd, 0:DIFF_DH, tq:2 * tq] = jnp.zeros((DIFF_DH, tq), BF16)
        corr_scr[hd] = jnp.where(same_chunk_or_earlier, (-2.0 * slope2[hd]) * ahead, -jnp.inf)

    lam = _lambda(lam_ref, lam_init)
    gsub = gsub_ref[...]

    def kblock(j, diag):
        ks = pl.ds(pl.multiple_of(j * tk, tk), tk)
        aug = aug_scr[ks, :]

        def scores(hd):
            k_aug = jnp.concatenate([k_ref[0, hd, ks, :], aug], axis=1)
            return jnp.dot(k_aug, qt_scr[hd], preferred_element_type=F32)

        def softmax(hd, s):
            if diag:
                corr = corr_scr[hd]
                s = s + jnp.concatenate([corr, corr], axis=1)
            m_old = m_scr[hd:hd + 1, :]
            m_new = jnp.maximum(m_old, jnp.max(s, axis=0, keepdims=True))
            m_scr[hd:hd + 1, :] = m_new
            return jnp.exp2(s - m_new).astype(BF16), jnp.exp2(m_old - m_new)

        def accumulate(hd, p, alpha):
            acc_scr[hd] = alpha * acc_scr[hd] + jnp.dot(
                vt_ref[0, hd, :, ks], p, preferred_element_type=F32)

        s_next = scores(0)
        pending = None
        for hd in range(nh):
            s_cur = s_next
            if hd + 1 < nh:
                s_next = scores(hd + 1)
            p_alpha = softmax(hd, s_cur)
            if pending is not None:
                accumulate(hd - 1, *pending)
            pending = p_alpha
        accumulate(nh - 1, *pending)

    def qblock(qi, carry):
        qs = pl.ds(pl.multiple_of(qi * tq, tq), tq)
        for hd in range(nh):
            qt = qt_ref[0, hd, :, qs]
            qt_scr[hd, 0:DIFF_DH, 0:tq] = qt[0:DIFF_DH, :]
            qt_scr[hd, DIFF_DH:2 * DIFF_DH, tq:2 * tq] = qt[DIFF_DH:2 * DIFF_DH, :]
        m_scr[...] = jnp.full(m_scr.shape, -jnp.inf, F32)
        acc_scr[...] = jnp.zeros(acc_scr.shape, F32)

        def off_diag(j, c):
            kblock(j, False)
            return c

        lax.fori_loop(0, qi, off_diag, 0)
        kblock(qi, True)

        for hd in range(nh):
            acc = acc_scr[hd]
            inv_l = 1.0 / acc[DIFF_DV:DIFF_DV + 1, :]
            o_t = (acc[0:DIFF_DV, 0:tq] * inv_l[:, 0:tq]
                   - lam * (acc[0:DIFF_DV, tq:2 * tq] * inv_l[:, tq:2 * tq]))
            ms = jnp.mean(o_t * o_t, axis=0, keepdims=True)
            o_t = o_t * lax.rsqrt(ms + EPS)
            o_ref[qs, hd * DIFF_DV:(hd + 1) * DIFF_DV] = (
                o_t.T * gsub * (1.0 - lam_init)).astype(BF16)
        return carry

    lax.fori_loop(0, nq, qblock, 0)


def _attn_prompt(slopes, lam_p, qdt, kdb, vdt, g_sub, *, tq, lam_init):
    nb, nh, t, _ = kdb.shape
    per_stream = lambda b, *_: (b, 0, 0, 0)
    grid_spec = pltpu.PrefetchScalarGridSpec(
        num_scalar_prefetch=1,
        grid=(nb,),
        in_specs=[pl.BlockSpec((4, DIFF_DH), lambda b, *_: (0, 0)),
                  pl.BlockSpec((1, nh, LANES, t), per_stream),
                  pl.BlockSpec((1, nh, t, LANES), per_stream),
                  pl.BlockSpec((1, nh, V_AUG_ROWS, t), per_stream),
                  pl.BlockSpec((1, DIFF_DV), lambda b, *_: (0, 0))],
        out_specs=pl.BlockSpec((t, nh * DIFF_DV), lambda b, *_: (b, 0)),
        scratch_shapes=[pltpu.VMEM((t, LANES), BF16),
                        pltpu.VMEM((nh, 2 * LANES, 2 * tq), BF16),
                        pltpu.VMEM((nh, tq, tq), F32),
                        pltpu.VMEM((nh, V_AUG_ROWS, 2 * tq), F32),
                        pltpu.VMEM((8, 2 * tq), F32)],
    )
    return pl.pallas_call(
        functools.partial(_attn_prompt_kernel, tq=tq, lam_init=lam_init),
        grid_spec=grid_spec,
        out_shape=jax.ShapeDtypeStruct((nb * t, nh * DIFF_DV), BF16),
        compiler_params=pltpu.CompilerParams(
            dimension_semantics=("arbitrary",), vmem_limit_bytes=VMEM_LIMIT_BYTES),
        name="attn_prompt",
    )(slopes, lam_p, qdt, kdb, vdt, g_sub)


def _attn_sample_kernel(slopes_ref, lam_ref, q_ref, ck_ref, cv_ref, kn_ref, vn_ref, gsub_ref, o_ref,
                        *, tq, past, lam_init):
    nh = DIFF_HEADS
    nt = (((1,), (1,)), ((), ()))
    k_pos = lax.broadcasted_iota(jnp.int32, (1, past), 1).astype(F32)
    qi = lax.broadcasted_iota(jnp.int32, (tq, tq), 0)
    kj = lax.broadcasted_iota(jnp.int32, (tq, tq), 1)
    rel_new = (past + qi - jnp.abs(qi - kj)).astype(F32)
    lam = _lambda(lam_ref, lam_init)
    gsub = gsub_ref[...]
    for hd in range(nh):
        slope = slopes_ref[hd]
        old = pl.ds(hd, past, stride=nh)
        new = pl.ds(hd, tq, stride=nh)
        qq = _split_halves(q_ref[0, hd])
        s_c = lax.dot_general(qq, ck_ref[0, old, :].astype(BF16), nt, preferred_element_type=F32)
        s_c = s_c + slope * k_pos
        s_n = lax.dot_general(qq, kn_ref[0, new, :].astype(BF16), nt, preferred_element_type=F32)
        bias_n = slope * rel_new
        s_n = s_n + jnp.concatenate([bias_n, bias_n], axis=0)
        m = jnp.maximum(jnp.max(s_c, axis=-1, keepdims=True), jnp.max(s_n, axis=-1, keepdims=True))
        p_c = jnp.exp(s_c - m)
        p_n = jnp.exp(s_n - m)
        l = jnp.sum(p_c, axis=-1, keepdims=True) + jnp.sum(p_n, axis=-1, keepdims=True)
        acc = (jnp.dot(p_c.astype(BF16), cv_ref[0, old, :].astype(BF16), preferred_element_type=F32)
               + jnp.dot(p_n.astype(BF16), vn_ref[0, new, :].astype(BF16),
                         preferred_element_type=F32))
        o_ref[:, hd * DIFF_DV:(hd + 1) * DIFF_DV] = _finish_attention(
            acc, l, lam, gsub, lam_init, tq).astype(BF16)


def _attn_sample(slopes, lam_p, qd, cache_k, cache_v, k_new, v_new, g_sub, *, lam_init):
    nb, rows, _ = cache_k.shape
    nh = DIFF_HEADS
    past = rows // nh
    tq = k_new.shape[1] // (nb * nh)
    cache_spec = pl.BlockSpec((1, past * nh, LANES), lambda b, *_: (b, 0, 0))
    new_spec = pl.BlockSpec((1, tq * nh, LANES), lambda b, *_: (0, b, 0))
    grid_spec = pltpu.PrefetchScalarGridSpec(
        num_scalar_prefetch=1,
        grid=(nb,),
        in_specs=[pl.BlockSpec((4, DIFF_DH), lambda b, *_: (0, 0)),
                  pl.BlockSpec((1, nh, tq, LANES), lambda b, *_: (0, 0, b, 0)),
                  cache_spec, cache_spec, new_spec, new_spec,
                  pl.BlockSpec((1, DIFF_DV), lambda b, *_: (0, 0))],
        out_specs=pl.BlockSpec((tq, nh * DIFF_DV), lambda b, *_: (b, 0)),
    )
    return pl.pallas_call(
        functools.partial(_attn_sample_kernel, tq=tq, past=past, lam_init=lam_init),
        grid_spec=grid_spec,
        out_shape=jax.ShapeDtypeStruct((nb * tq, nh * DIFF_DV), BF16),
        compiler_params=pltpu.CompilerParams(
            dimension_semantics=("arbitrary",), vmem_limit_bytes=VMEM_LIMIT_BYTES),
        name="attn_sample",
    )(slopes, lam_p, qd, cache_k, cache_v, k_new, v_new, g_sub)


def _out_ffn_kernel(og_ref, od_ref, x_ref, wo_ref, wu_ref, wd_ref,
                    gpm_ref, gpf_ref, gqf_ref, y_ref, *, ff_chunk):
    mix = (jnp.dot(og_ref[...], wo_ref[:W_V, :], preferred_element_type=F32)
           + jnp.dot(od_ref[...], wo_ref[W_V:, :], preferred_element_type=F32))
    x1 = x_ref[...] + _rms(mix, gpm_ref[...])
    f = _rms(x1, gpf_ref[...]).astype(BF16)
    d_ff = wu_ref.shape[1]
    acc = jnp.zeros(x1.shape, F32)
    for c in range(d_ff // ff_chunk):
        sl = slice(c * ff_chunk, (c + 1) * ff_chunk)
        hid = jnp.dot(f, wu_ref[:, sl], preferred_element_type=F32)
        hid = jnp.square(jnp.maximum(hid, 0.0)).astype(BF16)
        acc = acc + jnp.dot(hid, wd_ref[sl, :], preferred_element_type=F32)
    y_ref[...] = x1 + _rms(acc, gqf_ref[...])


def _out_ffn(og, od, x2d, w_out, w_up, w_down, g_post_mix, g_pre_ffn, g_post_ffn, *, tm):
    n, d = x2d.shape
    row = lambda i: (i, 0)
    return pl.pallas_call(
        functools.partial(_out_ffn_kernel, ff_chunk=1024),
        grid=(n // tm,),
        in_specs=[pl.BlockSpec((tm, W_V), row), pl.BlockSpec((tm, W_V), row),
                  pl.BlockSpec((tm, d), row),
                  _const_spec(w_out.shape), _const_spec(w_up.shape), _const_spec(w_down.shape),
                  _const_spec((1, d)), _const_spec((1, d)), _const_spec((1, d))],
        out_specs=pl.BlockSpec((tm, d), row),
        out_shape=jax.ShapeDtypeStruct((n, d), F32),
        compiler_params=pltpu.CompilerParams(
            dimension_semantics=("arbitrary",), vmem_limit_bytes=VMEM_LIMIT_BYTES),
        name="out_ffn",
    )(og, od, x2d, w_out, w_up, w_down, g_post_mix, g_pre_ffn, g_post_ffn)


def _reorder_w_in(w):
    d = w.shape[0]
    a0 = 2 * W_G + 2 * W_V
    a1 = a0 + GLA_RANK
    return jnp.concatenate(
        [w[:, :a0], w[:, a1:], w[:, a0:a1], jnp.zeros((d, LANES - GLA_RANK), w.dtype)],
        axis=1).astype(BF16)


def kernel(x_prompt, x_sample, cache_k, cache_v, state_gla, w_in, w_gate_up, b_gate, g_gla_out,
           lam_q1, lam_k1, lam_q2, lam_k2, g_subln, w_out, g_pre_mix, g_post_mix,
           g_pre_ffn, g_post_ffn, w_ff_up, w_ff_down):
    depth = w_in.shape[0]
    nb_p, t_p, d = x_prompt.shape
    nb_s, t_s, _ = x_sample.shape
    past = cache_k.shape[2]
    slopes = jnp.exp2(-8.0 / DIFF_HEADS * jnp.arange(1, DIFF_HEADS + 1, dtype=F32))
    yp = x_prompt.reshape(nb_p * t_p, d)
    ys = x_sample.reshape(nb_s * t_s, d)
    outs = [[] for _ in range(6)]
    for l in range(depth):
        lam_init = 0.8 - 0.6 * math.exp(-0.3 * l)
        w_in_r = _reorder_w_in(w_in[l])
        wg_pad = jnp.concatenate(
            [w_gate_up[l], jnp.zeros((LANES - GLA_RANK, W_G), F32)], axis=0).astype(BF16)
        bg = b_gate[l][None, :]
        lam_p = jnp.stack([lam_q1[l], lam_k1[l], lam_q2[l], lam_k2[l]], axis=0)
        g_out = g_gla_out[l][None, :]
        g_sub = g_subln[l][None, :]
        wo = w_out[l].astype(BF16)
        wu = w_ff_up[l].astype(BF16)
        wd = w_ff_down[l].astype(BF16)
        gains = (g_post_mix[l][None, :], g_pre_ffn[l][None, :], g_post_ffn[l][None, :])
        g_pre = g_pre_mix[l][None, :]

        qg, kg, vg, rg, gate, qd, kd, kdb, vd, vdb = _in_proj(
            yp, g_pre, w_in_r, wg_pad, bg, nb=nb_p, t=t_p, tm=512, transposed=True)
        s0 = jnp.zeros((nb_p, GLA_HEADS, GLA_DK, GLA_DV), F32)
        og, s_p = _gla(qg, kg, vg, rg, gate, s0, g_out, nb=nb_p, t=t_p, L=CHUNK, CB=8)
        od = _attn_prompt(slopes, lam_p, qd, kdb, vdb, g_sub, tq=256, lam_init=lam_init)
        yp = _out_ffn(og, od, yp, wo, wu, wd, *gains, tm=512)
        outs[0].append(kd.reshape(nb_p, t_p, DIFF_HEADS, 2 * DIFF_DH))
        outs[1].append(vd.reshape(nb_p, t_p, DIFF_HEADS, DIFF_DV))
        outs[2].append(s_p)

        n_s = nb_s * t_s
        qg, kg, vg, rg, gate, qd, kd, _, vd, _ = _in_proj(
            ys, g_pre, w_in_r, wg_pad, bg, nb=1, t=n_s, tm=n_s, transposed=False)
        og, s_s = _gla(qg, kg, vg, rg, gate, state_gla[l], g_out, nb=nb_s, t=t_s, L=t_s, CB=1)
        od = _attn_sample(slopes, lam_p, qd,
                          cache_k[l].reshape(nb_s, past * DIFF_HEADS, 2 * DIFF_DH),
                          cache_v[l].reshape(nb_s, past * DIFF_HEADS, DIFF_DV),
                          kd, vd, g_sub, lam_init=lam_init)
        ys = _out_ffn(og, od, ys, wo, wu, wd, *gains, tm=n_s)
        outs[3].append(kd.reshape(nb_s, t_s, DIFF_HEADS, 2 * DIFF_DH))
        outs[4].append(vd.reshape(nb_s, t_s, DIFF_HEADS, DIFF_DV))
        outs[5].append(s_s)

    stack = lambda xs: jnp.stack(xs, axis=0)
    return (yp.reshape(nb_p, t_p, d), ys.reshape(nb_s, t_s, d),
            stack(outs[0]), stack(outs[1]), stack(outs[2]),
            stack(outs[3]), stack(outs[4]), stack(outs[5]))
```

```python
import functools
import math

import jax
import jax.numpy as jnp
from jax import lax
from jax.experimental import pallas as pl
from jax.experimental.pallas import tpu as pltpu

F32 = jnp.float32
BF16 = jnp.bfloat16

EPS = 1e-6
CHUNK = 64
GLA_HEADS = 4
GLA_DK = 64
GLA_DV = 128
GLA_RANK = 16
GLA_TAU = 16.0
DIFF_HEADS = 4
DIFF_DH = 64
DIFF_DV = 128

W_G = GLA_HEADS * GLA_DK
W_V = GLA_HEADS * GLA_DV
W_D = DIFF_HEADS * 2 * DIFF_DH
LANES = 128
OFF_QG = 0
OFF_KG = OFF_QG + W_G
OFF_VG = OFF_KG + W_G
OFF_RG = OFF_VG + W_V
OFF_QD = OFF_RG + W_V
OFF_KD = OFF_QD + W_D
OFF_VD = OFF_KD + W_D
OFF_AG = OFF_VD + W_D
W_IN_PADDED = OFF_AG + LANES

BF16_SUBLANES = 16
V_AUG_ROWS = DIFF_DV + BF16_SUBLANES
LOG2E = math.log2(math.e)
K_UNROLL = 2
SCORE_AHEAD = 2

VMEM_LIMIT_BYTES = 56 * 1024 * 1024
GLA_SAFE_LOG_DECAY = -60.0


def _rms(x, g):
    ms = jnp.mean(x * x, axis=-1, keepdims=True)
    return x * lax.rsqrt(ms + EPS) * g


def _log_sigmoid(x):
    return jnp.minimum(x, 0.0) - jnp.log(1.0 + jnp.exp(-jnp.abs(x)))


def _div_pow2(x, d):
    assert d & (d - 1) == 0
    return lax.shift_right_arithmetic(x, d.bit_length() - 1)


def _const_spec(shape):
    zeros = (0,) * len(shape)
    return pl.BlockSpec(shape, lambda *_: zeros, pipeline_mode=pl.Buffered(1))


def _in_proj_kernel(x_ref, g_ref, w_ref, wg_ref, bg_ref,
                    qg_ref, kg_ref, vg_ref, rg_ref, gate_ref,
                    qd_ref, kd_ref, kdb_ref, vd_ref, vdb_ref, *, transposed):
    h = _rms(x_ref[...], g_ref[...]).astype(BF16)

    def proj(off, width):
        return jnp.dot(h, w_ref[:, off:off + width], preferred_element_type=F32)

    qg_ref[...] = (proj(OFF_QG, W_G) * (GLA_DK ** -0.5)).astype(BF16)
    kg_ref[...] = proj(OFF_KG, W_G).astype(BF16)
    vg_ref[...] = proj(OFF_VG, W_V).astype(BF16)
    rg_ref[...] = proj(OFF_RG, W_V).astype(BF16)
    qd = proj(OFF_QD, W_D) * (DIFF_DH ** -0.5)
    kd = proj(OFF_KD, W_D)
    vd = proj(OFF_VD, W_D)
    tm = qd.shape[0]
    for hd in range(DIFF_HEADS):
        head_rows = pl.ds(hd, tm, stride=DIFF_HEADS)
        kd_ref[0, head_rows, :] = kd[:, hd * LANES:(hd + 1) * LANES]
        vd_ref[0, head_rows, :] = vd[:, hd * LANES:(hd + 1) * LANES]
    ones_rows = (lax.broadcasted_iota(jnp.int32, (V_AUG_ROWS - DIFF_DV, tm), 0) == 0).astype(BF16)
    for hd in range(DIFF_HEADS):
        sl = slice(hd * LANES, (hd + 1) * LANES)
        kdb_ref[0, hd] = kd[:, sl].astype(BF16)
        if transposed:
            qd_ref[0, hd] = (qd[:, sl] * LOG2E).T.astype(BF16)
            vdb_ref[0, hd, 0:DIFF_DV, :] = vd[:, sl].T.astype(BF16)
            vdb_ref[0, hd, DIFF_DV:V_AUG_ROWS, :] = ones_rows
        else:
            qd_ref[0, hd] = qd[:, sl].astype(BF16)
            vdb_ref[0, hd] = vd[:, sl].astype(BF16)
    ag = proj(OFF_AG, LANES).astype(BF16)
    pre = jnp.dot(ag, wg_ref[...], preferred_element_type=F32) + bg_ref[...]
    gate_ref[...] = _log_sigmoid(pre) * (1.0 / GLA_TAU)


def _in_proj(x2d, g_pre, w_in_r, wg_pad, b_gate, *, nb, t, tm, transposed):
    n, d = x2d.shape
    steps_per_b = t // tm
    row = lambda i: (i, 0)
    hm = lambda i: (i // steps_per_b, 0, i % steps_per_b, 0)
    hm_t = lambda i: (i // steps_per_b, 0, 0, i % steps_per_b)
    hm_shape = jax.ShapeDtypeStruct((nb, DIFF_HEADS, t, LANES), BF16)
    hm_spec = pl.BlockSpec((1, DIFF_HEADS, tm, LANES), hm)
    kv_shape = jax.ShapeDtypeStruct((nb, t * DIFF_HEADS, LANES), F32)
    kv_spec = pl.BlockSpec((1, tm * DIFF_HEADS, LANES),
                           lambda i: (i // steps_per_b, i % steps_per_b, 0))
    if transposed:
        q_shape = jax.ShapeDtypeStruct((nb, DIFF_HEADS, LANES, t), BF16)
        q_spec = pl.BlockSpec((1, DIFF_HEADS, LANES, tm), hm_t)
        v_shape = jax.ShapeDtypeStruct((nb, DIFF_HEADS, V_AUG_ROWS, t), BF16)
        v_spec = pl.BlockSpec((1, DIFF_HEADS, V_AUG_ROWS, tm), hm_t)
    else:
        q_shape, q_spec, v_shape, v_spec = hm_shape, hm_spec, hm_shape, hm_spec
    out_shape = (
        jax.ShapeDtypeStruct((n, W_G), BF16), jax.ShapeDtypeStruct((n, W_G), BF16),
        jax.ShapeDtypeStruct((n, W_V), BF16), jax.ShapeDtypeStruct((n, W_V), BF16),
        jax.ShapeDtypeStruct((n, W_G), F32),
        q_shape,
        kv_shape, hm_shape,
        kv_shape, v_shape,
    )
    out_specs = (
        pl.BlockSpec((tm, W_G), row), pl.BlockSpec((tm, W_G), row),
        pl.BlockSpec((tm, W_V), row), pl.BlockSpec((tm, W_V), row),
        pl.BlockSpec((tm, W_G), row),
        q_spec,
        kv_spec, hm_spec,
        kv_spec, v_spec,
    )
    return pl.pallas_call(
        functools.partial(_in_proj_kernel, transposed=transposed),
        grid=(n // tm,),
        in_specs=[pl.BlockSpec((tm, d), row), _const_spec((1, d)),
                  _const_spec(w_in_r.shape), _const_spec(wg_pad.shape), _const_spec((1, W_G))],
        out_specs=out_specs,
        out_shape=out_shape,
        compiler_params=pltpu.CompilerParams(
            dimension_semantics=("arbitrary",), vmem_limit_bytes=VMEM_LIMIT_BYTES),
        name="in_proj",
    )(x2d, g_pre, w_in_r, wg_pad, b_gate)


def _gla_kernel(q_ref, k_ref, v_ref, r_ref, gate_ref, s0_ref, gout_ref,
                o_ref, sout_ref, s_scr, oi_scr, qf_scr, kf_scr, bf_scr, b_scr, *, L, CB):
    H, DK, DV = GLA_HEADS, GLA_DK, GLA_DV
    HL = H * L
    step = pl.program_id(1)

    @pl.when(step == 0)
    def _():
        s_scr[...] = s0_ref[0].reshape(H * DK, DV)

    tm = L * CB
    ones_l = jnp.ones((L, DV), BF16)
    lane_head = _div_pow2(lax.broadcasted_iota(jnp.int32, (L, W_G), 1), DK)
    row_i = lax.broadcasted_iota(jnp.int32, (HL, HL), 0)
    col_i = lax.broadcasted_iota(jnp.int32, (HL, HL), 1)
    causal = (col_i <= row_i) & (col_i >= (row_i & ~(L - 1)))
    row_t = lax.broadcasted_iota(jnp.int32, (L, 1), 0)
    gout = gout_ref[...]

    def split_hi_lo(x):
        hi = x.astype(BF16)
        return hi, (x - hi.astype(F32)).astype(BF16)

    tile_r = lax.broadcasted_iota(jnp.int32, (tm, tm), 0)
    tile_c = lax.broadcasted_iota(jnp.int32, (tm, tm), 1)
    tri = jnp.where((tile_c <= tile_r) & (tile_c >= (tile_r & ~(L - 1))), 1.0, 0.0).astype(BF16)
    g_hi, g_lo = split_hi_lo(gate_ref[...])
    b_scr[...] = (jnp.dot(tri, g_hi, preferred_element_type=F32)
                  + jnp.dot(tri, g_lo, preferred_element_type=F32))
    safe = jnp.min(b_scr[...]) >= GLA_SAFE_LOG_DECAY

    def stack_heads(a):
        return jnp.concatenate(
            [jnp.where(lane_head == hd, a, 0.0) for hd in range(H)], axis=0)

    def intra_factorised(rows, q, k, b, qs, vs):
        ks = stack_heads(k * jnp.exp(-b)).astype(BF16)
        a = lax.dot_general(qs, ks, (((1,), (1,)), ((), ())), preferred_element_type=F32)
        a = jnp.where(causal, a, 0.0).astype(BF16)
        return jnp.dot(a, vs, preferred_element_type=F32)

    def intra_per_token(rows, q, k, b, qs, vs):
        qf_scr[...] = q
        kf_scr[...] = k
        bf_scr[...] = b

        def tok(t, carry_t):
            w = (qf_scr[pl.ds(t, 1), :] * kf_scr[...]
                 * jnp.exp(jnp.minimum(bf_scr[pl.ds(t, 1), :] - bf_scr[...], 0.0)))
            for hd in range(H):
                a_col = jnp.sum(jnp.where(lane_head == hd, w, 0.0), axis=-1, keepdims=True)
                a_col = jnp.where(row_t <= t, a_col, 0.0)
                vh = v_ref[rows, hd * DV:(hd + 1) * DV].astype(F32)
                oi_scr[pl.ds(hd * L + t, 1), :] = jnp.sum(a_col * vh, axis=0, keepdims=True)
            return carry_t

        lax.fori_loop(0, L, tok, 0)
        return oi_scr[...]

    def chunk(rows, s_old, intra):
        b = b_scr[rows, :]
        b_last = b[L - 1:L, :]
        q = q_ref[rows, :].astype(F32)
        k = k_ref[rows, :].astype(F32)
        v = v_ref[rows, :]
        qs = stack_heads(q * jnp.exp(b)).astype(BF16)
        kends = stack_heads(k * jnp.exp(b_last - b)).astype(BF16)
        vs = jnp.concatenate([v[:, hd * DV:(hd + 1) * DV] for hd in range(H)], axis=0)
        o = (jnp.dot(qs, s_old.astype(BF16), preferred_element_type=F32)
             + intra(rows, q, k, b, qs, vs))
        o = _rms(o, gout)
        r = r_ref[rows, :].astype(F32)
        for hd in range(H):
            rh = r[:, hd * DV:(hd + 1) * DV]
            o_ref[rows, hd * DV:(hd + 1) * DV] = (
                o[hd * L:(hd + 1) * L, :] * (rh * jax.nn.sigmoid(rh))).astype(BF16)

        tdot = (((0,), (0,)), ((), ()))
        kv = lax.dot_general(kends, vs, tdot, preferred_element_type=F32)
        c_hi, c_lo = split_hi_lo(gate_ref[rows, :])
        dec = (lax.dot_general(c_hi, ones_l, tdot, preferred_element_type=F32)
               + lax.dot_general(c_lo, ones_l, tdot, preferred_element_type=F32))
        return jnp.exp(dec) * s_old + kv

    @pl.when(safe)
    def _():
        s = s_scr[...]
        for c in range(CB):
            s = chunk(slice(c * L, (c + 1) * L), s, intra_factorised)
        s_scr[...] = s

    @pl.when(jnp.logical_not(safe))
    def _():
        def body(c, carry):
            rows = pl.ds(pl.multiple_of(c * L, L), L)
            s_scr[...] = chunk(rows, s_scr[...], intra_per_token)
            return carry

        lax.fori_loop(0, CB, body, 0)

    @pl.when(step == pl.num_programs(1) - 1)
    def _():
        sout_ref[0] = s_scr[...].reshape(H, DK, DV)


def _gla(qg, kg, vg, rg, gate, s0, g_out, *, nb, t, L, CB):
    n = qg.shape[0]
    tm = L * CB
    steps = t // tm
    row = lambda b, s: (b * steps + s, 0)
    st = lambda b, s: (b, 0, 0, 0)
    state_shape = (1, GLA_HEADS, GLA_DK, GLA_DV)
    return pl.pallas_call(
        functools.partial(_gla_kernel, L=L, CB=CB),
        grid=(nb, steps),
        in_specs=[pl.BlockSpec((tm, W_G), row), pl.BlockSpec((tm, W_G), row),
                  pl.BlockSpec((tm, W_V), row), pl.BlockSpec((tm, W_V), row),
                  pl.BlockSpec((tm, W_G), row), pl.BlockSpec(state_shape, st),
                  pl.BlockSpec((1, GLA_DV), lambda b, s: (0, 0))],
        out_specs=(pl.BlockSpec((tm, W_V), row), pl.BlockSpec(state_shape, st)),
        out_shape=(jax.ShapeDtypeStruct((n, W_V), BF16),
                   jax.ShapeDtypeStruct((nb,) + state_shape[1:], F32)),
        scratch_shapes=[pltpu.VMEM((GLA_HEADS * GLA_DK, GLA_DV), F32),
                        pltpu.VMEM((GLA_HEADS * L, GLA_DV), F32),
                        pltpu.VMEM((L, W_G), F32), pltpu.VMEM((L, W_G), F32),
                        pltpu.VMEM((L, W_G), F32), pltpu.VMEM((tm, W_G), F32)],
        compiler_params=pltpu.CompilerParams(
            dimension_semantics=("arbitrary", "arbitrary"), vmem_limit_bytes=VMEM_LIMIT_BYTES),
        name="gla",
    )(qg, kg, vg, rg, gate, s0, g_out)


def _lambda(lam_ref, lam_init):
    lp = lam_ref[...]
    s1 = jnp.sum(lp[0:1, :] * lp[1:2, :], axis=-1, keepdims=True)
    s2 = jnp.sum(lp[2:3, :] * lp[3:4, :], axis=-1, keepdims=True)
    return jnp.exp(s1) - jnp.exp(s2) + lam_init


def _split_halves(q):
    lane = lax.broadcasted_iota(jnp.int32, q.shape, 1)
    zero = jnp.zeros_like(q)
    return jnp.concatenate(
        [jnp.where(lane < DIFF_DH, q, zero), jnp.where(lane >= DIFF_DH, q, zero)], axis=0)


def _finish_attention(acc, l, lam, gsub, lam_init, tq):
    o = acc[:tq] / l[:tq] - lam * (acc[tq:] / l[tq:])
    return _rms(o, gsub) * (1.0 - lam_init)


def _split3_bf16(x):
    hi = x.astype(BF16).astype(F32)
    r = x - hi
    mid = r.astype(BF16).astype(F32)
    lo = (r - mid).astype(BF16).astype(F32)
    return hi, mid, lo


def _attn_prompt_kernel(slopes_ref, lam_ref, qt_ref, k_ref, vt_ref, gsub_ref, o_ref,
                        aug_scr, qt_scr, corr_scr, acc_scr, m_scr, *, tq, lam_init):
    nh = DIFF_HEADS
    tk = tq
    t = k_ref.shape[2]
    nq = t // tq
    slope2 = [slopes_ref[hd] * LOG2E for hd in range(nh)]
    j_lane = 3 * nh

    lane = lax.broadcasted_iota(jnp.int32, (tk, LANES), 1)
    k_loc = lax.broadcasted_iota(jnp.int32, (tk, LANES), 0).astype(F32)
    base = jnp.zeros((tk, LANES), F32)
    for hd in range(nh):
        for i, part in enumerate(_split3_bf16(slope2[hd] * k_loc)):
            base = jnp.where(lane == 3 * hd + i, part, base)
    blk_lanes = (lane >= j_lane) & (lane < j_lane + 3)
    for j in range(t // tk):
        aug_scr[j * tk:(j + 1) * tk, :] = jnp.where(blk_lanes, float(j), base).astype(BF16)

    row = lax.broadcasted_iota(jnp.int32, (LANES, 2 * tq), 0)
    kl = lax.broadcasted_iota(jnp.int32, (tk, tq), 0)
    ql = lax.broadcasted_iota(jnp.int32, (tk, tq), 1)
    same_chunk_or_earlier = _div_pow2(kl, CHUNK) <= _div_pow2(ql, CHUNK)
    ahead = jnp.maximum(kl - ql, 0).astype(F32)
    for hd in range(nh):
        rows = jnp.where((row >= 3 * hd) & (row < 3 * hd + 3), 1.0, 0.0)
        for i, part in enumerate(_split3_bf16(jnp.full((LANES, 2 * tq), slope2[hd] * tk, F32))):
            rows = jnp.where(row == j_lane + i, part, rows)
        qt_scr[hd, LANES:2 * LANES, :] = rows.astype(BF16)
        qt_scr[hd, DIFF_DH:2 * DIFF_DH, 0:tq] = jnp.zeros((DIFF_DH, tq), BF16)
        qt_scr[hd, 0:DIFF_DH, tq:2 * tq] = jnp.zeros((DIFF_DH, tq), BF16)
        corr_scr[hd] = jnp.where(same_chunk_or_earlier, (-2.0 * slope2[hd]) * ahead, -jnp.inf)

    lam = _lambda(lam_ref, lam_init)
    gsub = gsub_ref[...]

    def kblocks(blocks):
        items = [(j, diag, hd) for j, diag in blocks for hd in range(nh)]

        def key_rows(j):
            return pl.ds(pl.multiple_of(j * tk, tk), tk)

        def scores(j, diag, hd):
            ks = key_rows(j)
            k_aug = jnp.concatenate([k_ref[0, hd, ks, :], aug_scr[ks, :]], axis=1)
            return jnp.dot(k_aug, qt_scr[hd], preferred_element_type=F32)

        def softmax(j, diag, hd, s):
            if diag:
                corr = corr_scr[hd]
                s = s + jnp.concatenate([corr, corr], axis=1)
            m_old = m_scr[hd:hd + 1, :]
            m_new = jnp.maximum(m_old, jnp.max(s, axis=0, keepdims=True))
            m_scr[hd:hd + 1, :] = m_new
            return jnp.exp2(s - m_new).astype(BF16), jnp.exp2(m_old - m_new)

        def accumulate(j, diag, hd, p, alpha):
            acc_scr[hd] = alpha * acc_scr[hd] + jnp.dot(
                vt_ref[0, hd, :, key_rows(j)], p, preferred_element_type=F32)

        n = len(items)
        s_tiles = {i: scores(*items[i]) for i in range(min(SCORE_AHEAD, n))}
        pending = None
        for i in range(n):
            if i + SCORE_AHEAD < n:
                s_tiles[i + SCORE_AHEAD] = scores(*items[i + SCORE_AHEAD])
            p_alpha = softmax(*items[i], s_tiles.pop(i))
            if pending is not None:
                accumulate(*items[i - 1], *pending)
            pending = p_alpha
        accumulate(*items[n - 1], *pending)

    def qblock(qi, carry):
        qs = pl.ds(pl.multiple_of(qi * tq, tq), tq)
        for hd in range(nh):
            qt = qt_ref[0, hd, :, qs]
            qt_scr[hd, 0:DIFF_DH, 0:tq] = qt[0:DIFF_DH, :]
            qt_scr[hd, DIFF_DH:2 * DIFF_DH, tq:2 * tq] = qt[DIFF_DH:2 * DIFF_DH, :]
        m_scr[...] = jnp.full(m_scr.shape, -jnp.inf, F32)
        acc_scr[...] = jnp.zeros(acc_scr.shape, F32)

        def off_diag(jj, c):
            kblocks([(jj * K_UNROLL + u, False) for u in range(K_UNROLL)])
            return c

        lax.fori_loop(0, qi // K_UNROLL, off_diag, 0)
        for r in range(1, K_UNROLL):
            @pl.when(qi % K_UNROLL >= r)
            def _(r=r):
                kblocks([((qi // K_UNROLL) * K_UNROLL + r - 1, False)])
        kblocks([(qi, True)])

        for hd in range(nh):
            acc = acc_scr[hd]
            inv_l = 1.0 / acc[DIFF_DV:DIFF_DV + 1, :]
            o_t = (acc[0:DIFF_DV, 0:tq] * inv_l[:, 0:tq]
                   - lam * (acc[0:DIFF_DV, tq:2 * tq] * inv_l[:, tq:2 * tq]))
            ms = jnp.mean(o_t * o_t, axis=0, keepdims=True)
            o_t = o_t * lax.rsqrt(ms + EPS)
            o_ref[qs, hd * DIFF_DV:(hd + 1) * DIFF_DV] = (
                o_t.T * gsub * (1.0 - lam_init)).astype(BF16)
        return carry

    lax.fori_loop(0, nq, qblock, 0)


def _attn_prompt(slopes, lam_p, qdt, kdb, vdt, g_sub, *, tq, lam_init):
    nb, nh, t, _ = kdb.shape
    per_stream = lambda b, *_: (b, 0, 0, 0)
    grid_spec = pltpu.PrefetchScalarGridSpec(
        num_scalar_prefetch=1,
        grid=(nb,),
        in_specs=[pl.BlockSpec((4, DIFF_DH), lambda b, *_: (0, 0)),
                  pl.BlockSpec((1, nh, LANES, t), per_stream),
                  pl.BlockSpec((1, nh, t, LANES), per_stream),
                  pl.BlockSpec((1, nh, V_AUG_ROWS, t), per_stream),
                  pl.BlockSpec((1, DIFF_DV), lambda b, *_: (0, 0))],
        out_specs=pl.BlockSpec((t, nh * DIFF_DV), lambda b, *_: (b, 0)),
        scratch_shapes=[pltpu.VMEM((t, LANES), BF16),
                        pltpu.VMEM((nh, 2 * LANES, 2 * tq), BF16),
                        pltpu.VMEM((nh, tq, tq), F32),
                        pltpu.VMEM((nh, V_AUG_ROWS, 2 * tq), F32),
                        pltpu.VMEM((8, 2 * tq), F32)],
    )
    return pl.pallas_call(
        functools.partial(_attn_prompt_kernel, tq=tq, lam_init=lam_init),
        grid_spec=grid_spec,
        out_shape=jax.ShapeDtypeStruct((nb * t, nh * DIFF_DV), BF16),
        compiler_params=pltpu.CompilerParams(
            dimension_semantics=("arbitrary",), vmem_limit_bytes=VMEM_LIMIT_BYTES),
        name="attn_prompt",
    )(slopes, lam_p, qdt, kdb, vdt, g_sub)


def _attn_sample_kernel(slopes_ref, lam_ref, q_ref, ck_ref, cv_ref, kn_ref, vn_ref, gsub_ref, o_ref,
                        *, tq, past, lam_init):
    nh = DIFF_HEADS
    nt = (((1,), (1,)), ((), ()))
    k_pos = lax.broadcasted_iota(jnp.int32, (1, past), 1).astype(F32)
    qi = lax.broadcasted_iota(jnp.int32, (tq, tq), 0)
    kj = lax.broadcasted_iota(jnp.int32, (tq, tq), 1)
    rel_new = (past + qi - jnp.abs(qi - kj)).astype(F32)
    lam = _lambda(lam_ref, lam_init)
    gsub = gsub_ref[...]
    for hd in range(nh):
        slope = slopes_ref[hd]
        old = pl.ds(hd, past, stride=nh)
        new = pl.ds(hd, tq, stride=nh)
        qq = _split_halves(q_ref[0, hd])
        s_c = lax.dot_general(qq, ck_ref[0, old, :].astype(BF16), nt, preferred_element_type=F32)
        s_c = s_c + slope * k_pos
        s_n = lax.dot_general(qq, kn_ref[0, new, :].astype(BF16), nt, preferred_element_type=F32)
        bias_n = slope * rel_new
        s_n = s_n + jnp.concatenate([bias_n, bias_n], axis=0)
        m = jnp.maximum(jnp.max(s_c, axis=-1, keepdims=True), jnp.max(s_n, axis=-1, keepdims=True))
        p_c = jnp.exp(s_c - m)
        p_n = jnp.exp(s_n - m)
        l = jnp.sum(p_c, axis=-1, keepdims=True) + jnp.sum(p_n, axis=-1, keepdims=True)
        acc = (jnp.dot(p_c.astype(BF16), cv_ref[0, old, :].astype(BF16), preferred_element_type=F32)
               + jnp.dot(p_n.astype(BF16), vn_ref[0, new, :].astype(BF16),
                         preferred_element_type=F32))
        o_ref[:, hd * DIFF_DV:(hd + 1) * DIFF_DV] = _finish_attention(
            acc, l, lam, gsub, lam_init, tq).astype(BF16)


def _attn_sample(slopes, lam_p, qd, cache_k, cache_v, k_new, v_new, g_sub, *, lam_init):
    nb, rows, _ = cache_k.shape
    nh = DIFF_HEADS
    past = rows // nh
    tq = k_new.shape[1] // (nb * nh)
    cache_spec = pl.BlockSpec((1, past * nh, LANES), lambda b, *_: (b, 0, 0))
    new_spec = pl.BlockSpec((1, tq * nh, LANES), lambda b, *_: (0, b, 0))
    grid_spec = pltpu.PrefetchScalarGridSpec(
        num_scalar_prefetch=1,
        grid=(nb,),
        in_specs=[pl.BlockSpec((4, DIFF_DH), lambda b, *_: (0, 0)),
                  pl.BlockSpec((1, nh, tq, LANES), lambda b, *_: (0, 0, b, 0)),
                  cache_spec, cache_spec, new_spec, new_spec,
                  pl.BlockSpec((1, DIFF_DV), lambda b, *_: (0, 0))],
        out_specs=pl.BlockSpec((tq, nh * DIFF_DV), lambda b, *_: (b, 0)),
    )
    return pl.pallas_call(
        functools.partial(_attn_sample_kernel, tq=tq, past=past, lam_init=lam_init),
        grid_spec=grid_spec,
        out_shape=jax.ShapeDtypeStruct((nb * tq, nh * DIFF_DV), BF16),
        compiler_params=pltpu.CompilerParams(
            dimension_semantics=("arbitrary",), vmem_limit_bytes=VMEM_LIMIT_BYTES),
        name="attn_sample",
    )(slopes, lam_p, qd, cache_k, cache_v, k_new, v_new, g_sub)


def _out_ffn_kernel(og_ref, od_ref, x_ref, wo_ref, wu_ref, wd_ref,
                    gpm_ref, gpf_ref, gqf_ref, y_ref, *, ff_chunk):
    mix = (jnp.dot(og_ref[...], wo_ref[:W_V, :], preferred_element_type=F32)
           + jnp.dot(od_ref[...], wo_ref[W_V:, :], preferred_element_type=F32))
    x1 = x_ref[...] + _rms(mix, gpm_ref[...])
    f = _rms(x1, gpf_ref[...]).astype(BF16)
    d_ff = wu_ref.shape[1]
    acc = jnp.zeros(x1.shape, F32)
    for c in range(d_ff // ff_chunk):
        sl = slice(c * ff_chunk, (c + 1) * ff_chunk)
        hid = jnp.dot(f, wu_ref[:, sl], preferred_element_type=F32)
        hid = jnp.square(jnp.maximum(hid, 0.0)).astype(BF16)
        acc = acc + jnp.dot(hid, wd_ref[sl, :], preferred_element_type=F32)
    y_ref[...] = x1 + _rms(acc, gqf_ref[...])


def _out_ffn(og, od, x2d, w_out, w_up, w_down, g_post_mix, g_pre_ffn, g_post_ffn, *, tm):
    n, d = x2d.shape
    row = lambda i: (i, 0)
    return pl.pallas_call(
        functools.partial(_out_ffn_kernel, ff_chunk=1024),
        grid=(n // tm,),
        in_specs=[pl.BlockSpec((tm, W_V), row), pl.BlockSpec((tm, W_V), row),
                  pl.BlockSpec((tm, d), row),
                  _const_spec(w_out.shape), _const_spec(w_up.shape), _const_spec(w_down.shape),
                  _const_spec((1, d)), _const_spec((1, d)), _const_spec((1, d))],
        out_specs=pl.BlockSpec((tm, d), row),
        out_shape=jax.ShapeDtypeStruct((n, d), F32),
        compiler_params=pltpu.CompilerParams(
            dimension_semantics=("arbitrary",), vmem_limit_bytes=VMEM_LIMIT_BYTES),
        name="out_ffn",
    )(og, od, x2d, w_out, w_up, w_down, g_post_mix, g_pre_ffn, g_post_ffn)


def _reorder_w_in(w):
    d = w.shape[0]
    a0 = 2 * W_G + 2 * W_V
    a1 = a0 + GLA_RANK
    return jnp.concatenate(
        [w[:, :a0], w[:, a1:], w[:, a0:a1], jnp.zeros((d, LANES - GLA_RANK), w.dtype)],
        axis=1).astype(BF16)


def kernel(x_prompt, x_sample, cache_k, cache_v, state_gla, w_in, w_gate_up, b_gate, g_gla_out,
           lam_q1, lam_k1, lam_q2, lam_k2, g_subln, w_out, g_pre_mix, g_post_mix,
           g_pre_ffn, g_post_ffn, w_ff_up, w_ff_down):
    depth = w_in.shape[0]
    nb_p, t_p, d = x_prompt.shape
    nb_s, t_s, _ = x_sample.shape
    past = cache_k.shape[2]
    slopes = jnp.exp2(-8.0 / DIFF_HEADS * jnp.arange(1, DIFF_HEADS + 1, dtype=F32))
    yp = x_prompt.reshape(nb_p * t_p, d)
    ys = x_sample.reshape(nb_s * t_s, d)
    outs = [[] for _ in range(6)]
    for l in range(depth):
        lam_init = 0.8 - 0.6 * math.exp(-0.3 * l)
        w_in_r = _reorder_w_in(w_in[l])
        wg_pad = jnp.concatenate(
            [w_gate_up[l], jnp.zeros((LANES - GLA_RANK, W_G), F32)], axis=0).astype(BF16)
        bg = b_gate[l][None, :]
        lam_p = jnp.stack([lam_q1[l], lam_k1[l], lam_q2[l], lam_k2[l]], axis=0)
        g_out = g_gla_out[l][None, :]
        g_sub = g_subln[l][None, :]
        wo = w_out[l].astype(BF16)
        wu = w_ff_up[l].astype(BF16)
        wd = w_ff_down[l].astype(BF16)
        gains = (g_post_mix[l][None, :], g_pre_ffn[l][None, :], g_post_ffn[l][None, :])
        g_pre = g_pre_mix[l][None, :]

        qg, kg, vg, rg, gate, qd, kd, kdb, vd, vdb = _in_proj(
            yp, g_pre, w_in_r, wg_pad, bg, nb=nb_p, t=t_p, tm=512, transposed=True)
        s0 = jnp.zeros((nb_p, GLA_HEADS, GLA_DK, GLA_DV), F32)
        og, s_p = _gla(qg, kg, vg, rg, gate, s0, g_out, nb=nb_p, t=t_p, L=CHUNK, CB=8)
        od = _attn_prompt(slopes, lam_p, qd, kdb, vdb, g_sub, tq=256, lam_init=lam_init)
        yp = _out_ffn(og, od, yp, wo, wu, wd, *gains, tm=512)
        outs[0].append(kd.reshape(nb_p, t_p, DIFF_HEADS, 2 * DIFF_DH))
        outs[1].append(vd.reshape(nb_p, t_p, DIFF_HEADS, DIFF_DV))
        outs[2].append(s_p)

        n_s = nb_s * t_s
        qg, kg, vg, rg, gate, qd, kd, _, vd, _ = _in_proj(
            ys, g_pre, w_in_r, wg_pad, bg, nb=1, t=n_s, tm=n_s, transposed=False)
        og, s_s = _gla(qg, kg, vg, rg, gate, state_gla[l], g_out, nb=nb_s, t=t_s, L=t_s, CB=1)
        od = _attn_sample(slopes, lam_p, qd,
                          cache_k[l].reshape(nb_s, past * DIFF_HEADS, 2 * DIFF_DH),
                          cache_v[l].reshape(nb_s, past * DIFF_HEADS, DIFF_DV),
                          kd, vd, g_sub, lam_init=lam_init)
        ys = _out_ffn(og, od, ys, wo, wu, wd, *gains, tm=n_s)
        outs[3].append(kd.reshape(nb_s, t_s, DIFF_HEADS, 2 * DIFF_DH))
        outs[4].append(vd.reshape(nb_s, t_s, DIFF_HEADS, DIFF_DV))
        outs[5].append(s_s)

    stack = lambda xs: jnp.stack(xs, axis=0)
    return (yp.reshape(nb_p, t_p, d), ys.reshape(nb_s, t_s, d),
            stack(outs[0]), stack(outs[1]), stack(outs[2]),
            stack(outs[3]), stack(outs[4]), stack(outs[5]))
```

```python
import functools
import math

import jax
import jax.numpy as jnp
from jax import lax
from jax.experimental import pallas as pl
from jax.experimental.pallas import tpu as pltpu

F32 = jnp.float32
BF16 = jnp.bfloat16

EPS = 1e-6
CHUNK = 64
GLA_HEADS = 4
GLA_DK = 64
GLA_DV = 128
GLA_RANK = 16
GLA_TAU = 16.0
DIFF_HEADS = 4
DIFF_DH = 64
DIFF_DV = 128

W_G = GLA_HEADS * GLA_DK
W_V = GLA_HEADS * GLA_DV
W_D = DIFF_HEADS * 2 * DIFF_DH
LANES = 128
OFF_QG = 0
OFF_KG = OFF_QG + W_G
OFF_VG = OFF_KG + W_G
OFF_RG = OFF_VG + W_V
OFF_QD = OFF_RG + W_V
OFF_KD = OFF_QD + W_D
OFF_VD = OFF_KD + W_D
OFF_AG = OFF_VD + W_D
W_IN_PADDED = OFF_AG + LANES

BF16_SUBLANES = 16
V_AUG_ROWS = DIFF_DV + BF16_SUBLANES
LOG2E = math.log2(math.e)
K_WIDE = 2
K_UNROLL = 2
SCORE_AHEAD = 2
S_SLOTS = SCORE_AHEAD + 1

VMEM_LIMIT_BYTES = 56 * 1024 * 1024
GLA_SAFE_LOG_DECAY = -60.0


def _rms(x, g):
    ms = jnp.mean(x * x, axis=-1, keepdims=True)
    return x * lax.rsqrt(ms + EPS) * g


def _log_sigmoid(x):
    return jnp.minimum(x, 0.0) - jnp.log(1.0 + jnp.exp(-jnp.abs(x)))


def _div_pow2(x, d):
    assert d & (d - 1) == 0
    return lax.shift_right_arithmetic(x, d.bit_length() - 1)


def _const_spec(shape):
    zeros = (0,) * len(shape)
    return pl.BlockSpec(shape, lambda *_: zeros, pipeline_mode=pl.Buffered(1))


def _in_proj_kernel(x_ref, g_ref, w_ref, wg_ref, bg_ref,
                    qg_ref, kg_ref, vg_ref, rg_ref, gate_ref,
                    qd_ref, kd_ref, kdb_ref, vd_ref, vdb_ref, *, transposed):
    h = _rms(x_ref[...], g_ref[...]).astype(BF16)

    def proj(off, width):
        return jnp.dot(h, w_ref[:, off:off + width], preferred_element_type=F32)

    qg_ref[...] = (proj(OFF_QG, W_G) * (GLA_DK ** -0.5)).astype(BF16)
    kg_ref[...] = proj(OFF_KG, W_G).astype(BF16)
    vg_ref[...] = proj(OFF_VG, W_V).astype(BF16)
    rg_ref[...] = proj(OFF_RG, W_V).astype(BF16)
    qd = proj(OFF_QD, W_D) * (DIFF_DH ** -0.5)
    kd = proj(OFF_KD, W_D)
    vd = proj(OFF_VD, W_D)
    tm = qd.shape[0]
    for hd in range(DIFF_HEADS):
        head_rows = pl.ds(hd, tm, stride=DIFF_HEADS)
        kd_ref[0, head_rows, :] = kd[:, hd * LANES:(hd + 1) * LANES]
        vd_ref[0, head_rows, :] = vd[:, hd * LANES:(hd + 1) * LANES]
    ones_rows = (lax.broadcasted_iota(jnp.int32, (V_AUG_ROWS - DIFF_DV, tm), 0) == 0).astype(BF16)
    for hd in range(DIFF_HEADS):
        sl = slice(hd * LANES, (hd + 1) * LANES)
        kdb_ref[0, hd] = kd[:, sl].astype(BF16)
        if transposed:
            qd_ref[0, hd] = (qd[:, sl] * LOG2E).T.astype(BF16)
            vdb_ref[0, hd, 0:DIFF_DV, :] = vd[:, sl].T.astype(BF16)
            vdb_ref[0, hd, DIFF_DV:V_AUG_ROWS, :] = ones_rows
        else:
            qd_ref[0, hd] = qd[:, sl].astype(BF16)
            vdb_ref[0, hd] = vd[:, sl].astype(BF16)
    ag = proj(OFF_AG, LANES).astype(BF16)
    pre = jnp.dot(ag, wg_ref[...], preferred_element_type=F32) + bg_ref[...]
    gate_ref[...] = _log_sigmoid(pre) * (1.0 / GLA_TAU)


def _in_proj(x2d, g_pre, w_in_r, wg_pad, b_gate, *, nb, t, tm, transposed):
    n, d = x2d.shape
    steps_per_b = t // tm
    row = lambda i: (i, 0)
    hm = lambda i: (i // steps_per_b, 0, i % steps_per_b, 0)
    hm_t = lambda i: (i // steps_per_b, 0, 0, i % steps_per_b)
    hm_shape = jax.ShapeDtypeStruct((nb, DIFF_HEADS, t, LANES), BF16)
    hm_spec = pl.BlockSpec((1, DIFF_HEADS, tm, LANES), hm)
    kv_shape = jax.ShapeDtypeStruct((nb, t * DIFF_HEADS, LANES), F32)
    kv_spec = pl.BlockSpec((1, tm * DIFF_HEADS, LANES),
                           lambda i: (i // steps_per_b, i % steps_per_b, 0))
    if transposed:
        q_shape = jax.ShapeDtypeStruct((nb, DIFF_HEADS, LANES, t), BF16)
        q_spec = pl.BlockSpec((1, DIFF_HEADS, LANES, tm), hm_t)
        v_shape = jax.ShapeDtypeStruct((nb, DIFF_HEADS, V_AUG_ROWS, t), BF16)
        v_spec = pl.BlockSpec((1, DIFF_HEADS, V_AUG_ROWS, tm), hm_t)
    else:
        q_shape, q_spec, v_shape, v_spec = hm_shape, hm_spec, hm_shape, hm_spec
    out_shape = (
        jax.ShapeDtypeStruct((n, W_G), BF16), jax.ShapeDtypeStruct((n, W_G), BF16),
        jax.ShapeDtypeStruct((n, W_V), BF16), jax.ShapeDtypeStruct((n, W_V), BF16),
        jax.ShapeDtypeStruct((n, W_G), F32),
        q_shape,
        kv_shape, hm_shape,
        kv_shape, v_shape,
    )
    out_specs = (
        pl.BlockSpec((tm, W_G), row), pl.BlockSpec((tm, W_G), row),
        pl.BlockSpec((tm, W_V), row), pl.BlockSpec((tm, W_V), row),
        pl.BlockSpec((tm, W_G), row),
        q_spec,
        kv_spec, hm_spec,
        kv_spec, v_spec,
    )
    return pl.pallas_call(
        functools.partial(_in_proj_kernel, transposed=transposed),
        grid=(n // tm,),
        in_specs=[pl.BlockSpec((tm, d), row), _const_spec((1, d)),
                  _const_spec(w_in_r.shape), _const_spec(wg_pad.shape), _const_spec((1, W_G))],
        out_specs=out_specs,
        out_shape=out_shape,
        compiler_params=pltpu.CompilerParams(
            dimension_semantics=("arbitrary",), vmem_limit_bytes=VMEM_LIMIT_BYTES),
        name="in_proj",
    )(x2d, g_pre, w_in_r, wg_pad, b_gate)


def _gla_kernel(q_ref, k_ref, v_ref, r_ref, gate_ref, s0_ref, gout_ref,
                o_ref, sout_ref, s_scr, oi_scr, qf_scr, kf_scr, bf_scr, b_scr, *, L, CB):
    H, DK, DV = GLA_HEADS, GLA_DK, GLA_DV
    HL = H * L
    step = pl.program_id(1)

    @pl.when(step == 0)
    def _():
        s_scr[...] = s0_ref[0].reshape(H * DK, DV)

    tm = L * CB
    ones_l = jnp.ones((L, DV), BF16)
    lane_head = _div_pow2(lax.broadcasted_iota(jnp.int32, (L, W_G), 1), DK)
    row_i = lax.broadcasted_iota(jnp.int32, (HL, HL), 0)
    col_i = lax.broadcasted_iota(jnp.int32, (HL, HL), 1)
    causal = (col_i <= row_i) & (col_i >= (row_i & ~(L - 1)))
    row_t = lax.broadcasted_iota(jnp.int32, (L, 1), 0)
    gout = gout_ref[...]

    def split_hi_lo(x):
        hi = x.astype(BF16)
        return hi, (x - hi.astype(F32)).astype(BF16)

    tile_r = lax.broadcasted_iota(jnp.int32, (tm, tm), 0)
    tile_c = lax.broadcasted_iota(jnp.int32, (tm, tm), 1)
    tri = jnp.where((tile_c <= tile_r) & (tile_c >= (tile_r & ~(L - 1))), 1.0, 0.0).astype(BF16)
    g_hi, g_lo = split_hi_lo(gate_ref[...])
    b_scr[...] = (jnp.dot(tri, g_hi, preferred_element_type=F32)
                  + jnp.dot(tri, g_lo, preferred_element_type=F32))
    safe = jnp.min(b_scr[...]) >= GLA_SAFE_LOG_DECAY

    def stack_heads(a):
        return jnp.concatenate(
            [jnp.where(lane_head == hd, a, 0.0) for hd in range(H)], axis=0)

    def intra_factorised(rows, q, k, b, qs, vs):
        ks = stack_heads(k * jnp.exp(-b)).astype(BF16)
        a = lax.dot_general(qs, ks, (((1,), (1,)), ((), ())), preferred_element_type=F32)
        a = jnp.where(causal, a, 0.0).astype(BF16)
        return jnp.dot(a, vs, preferred_element_type=F32)

    def intra_per_token(rows, q, k, b, qs, vs):
        qf_scr[...] = q
        kf_scr[...] = k
        bf_scr[...] = b

        def tok(t, carry_t):
            w = (qf_scr[pl.ds(t, 1), :] * kf_scr[...]
                 * jnp.exp(jnp.minimum(bf_scr[pl.ds(t, 1), :] - bf_scr[...], 0.0)))
            for hd in range(H):
                a_col = jnp.sum(jnp.where(lane_head == hd, w, 0.0), axis=-1, keepdims=True)
                a_col = jnp.where(row_t <= t, a_col, 0.0)
                vh = v_ref[rows, hd * DV:(hd + 1) * DV].astype(F32)
                oi_scr[pl.ds(hd * L + t, 1), :] = jnp.sum(a_col * vh, axis=0, keepdims=True)
            return carry_t

        lax.fori_loop(0, L, tok, 0)
        return oi_scr[...]

    def chunk(rows, s_old, intra):
        b = b_scr[rows, :]
        b_last = b[L - 1:L, :]
        q = q_ref[rows, :].astype(F32)
        k = k_ref[rows, :].astype(F32)
        v = v_ref[rows, :]
        qs = stack_heads(q * jnp.exp(b)).astype(BF16)
        kends = stack_heads(k * jnp.exp(b_last - b)).astype(BF16)
        vs = jnp.concatenate([v[:, hd * DV:(hd + 1) * DV] for hd in range(H)], axis=0)
        o = (jnp.dot(qs, s_old.astype(BF16), preferred_element_type=F32)
             + intra(rows, q, k, b, qs, vs))
        o = _rms(o, gout)
        r = r_ref[rows, :].astype(F32)
        for hd in range(H):
            rh = r[:, hd * DV:(hd + 1) * DV]
            o_ref[rows, hd * DV:(hd + 1) * DV] = (
                o[hd * L:(hd + 1) * L, :] * (rh * jax.nn.sigmoid(rh))).astype(BF16)

        tdot = (((0,), (0,)), ((), ()))
        kv = lax.dot_general(kends, vs, tdot, preferred_element_type=F32)
        c_hi, c_lo = split_hi_lo(gate_ref[rows, :])
        dec = (lax.dot_general(c_hi, ones_l, tdot, preferred_element_type=F32)
               + lax.dot_general(c_lo, ones_l, tdot, preferred_element_type=F32))
        return jnp.exp(dec) * s_old + kv

    @pl.when(safe)
    def _():
        s = s_scr[...]
        for c in range(CB):
            s = chunk(slice(c * L, (c + 1) * L), s, intra_factorised)
        s_scr[...] = s

    @pl.when(jnp.logical_not(safe))
    def _():
        def body(c, carry):
            rows = pl.ds(pl.multiple_of(c * L, L), L)
            s_scr[...] = chunk(rows, s_scr[...], intra_per_token)
            return carry

        lax.fori_loop(0, CB, body, 0)

    @pl.when(step == pl.num_programs(1) - 1)
    def _():
        sout_ref[0] = s_scr[...].reshape(H, DK, DV)


def _gla(qg, kg, vg, rg, gate, s0, g_out, *, nb, t, L, CB):
    n = qg.shape[0]
    tm = L * CB
    steps = t // tm
    row = lambda b, s: (b * steps + s, 0)
    st = lambda b, s: (b, 0, 0, 0)
    state_shape = (1, GLA_HEADS, GLA_DK, GLA_DV)
    return pl.pallas_call(
        functools.partial(_gla_kernel, L=L, CB=CB),
        grid=(nb, steps),
        in_specs=[pl.BlockSpec((tm, W_G), row), pl.BlockSpec((tm, W_G), row),
                  pl.BlockSpec((tm, W_V), row), pl.BlockSpec((tm, W_V), row),
                  pl.BlockSpec((tm, W_G), row), pl.BlockSpec(state_shape, st),
                  pl.BlockSpec((1, GLA_DV), lambda b, s: (0, 0))],
        out_specs=(pl.BlockSpec((tm, W_V), row), pl.BlockSpec(state_shape, st)),
        out_shape=(jax.ShapeDtypeStruct((n, W_V), BF16),
                   jax.ShapeDtypeStruct((nb,) + state_shape[1:], F32)),
        scratch_shapes=[pltpu.VMEM((GLA_HEADS * GLA_DK, GLA_DV), F32),
                        pltpu.VMEM((GLA_HEADS * L, GLA_DV), F32),
                        pltpu.VMEM((L, W_G), F32), pltpu.VMEM((L, W_G), F32),
                        pltpu.VMEM((L, W_G), F32), pltpu.VMEM((tm, W_G), F32)],
        compiler_params=pltpu.CompilerParams(
            dimension_semantics=("arbitrary", "arbitrary"), vmem_limit_bytes=VMEM_LIMIT_BYTES),
        name="gla",
    )(qg, kg, vg, rg, gate, s0, g_out)


def _lambda(lam_ref, lam_init):
    lp = lam_ref[...]
    s1 = jnp.sum(lp[0:1, :] * lp[1:2, :], axis=-1, keepdims=True)
    s2 = jnp.sum(lp[2:3, :] * lp[3:4, :], axis=-1, keepdims=True)
    return jnp.exp(s1) - jnp.exp(s2) + lam_init


def _split_halves(q):
    lane = lax.broadcasted_iota(jnp.int32, q.shape, 1)
    zero = jnp.zeros_like(q)
    return jnp.concatenate(
        [jnp.where(lane < DIFF_DH, q, zero), jnp.where(lane >= DIFF_DH, q, zero)], axis=0)


def _finish_attention(acc, l, lam, gsub, lam_init, tq):
    o = acc[:tq] / l[:tq] - lam * (acc[tq:] / l[tq:])
    return _rms(o, gsub) * (1.0 - lam_init)


def _split3_bf16(x):
    hi = x.astype(BF16).astype(F32)
    r = x - hi
    mid = r.astype(BF16).astype(F32)
    lo = (r - mid).astype(BF16).astype(F32)
    return hi, mid, lo


def _attn_prompt_kernel(slopes_ref, lam_ref, qt_ref, k_ref, vt_ref, gsub_ref, o_ref,
                        aug_scr, qt_scr, corr_scr, acc_scr, m_scr, s_scr, *, tq, lam_init):
    nh = DIFF_HEADS
    tk = tq
    t = k_ref.shape[2]
    nq = t // tq
    slope2 = [slopes_ref[hd] * LOG2E for hd in range(nh)]
    j_lane = 3 * nh

    lane = lax.broadcasted_iota(jnp.int32, (tk, LANES), 1)
    k_loc = lax.broadcasted_iota(jnp.int32, (tk, LANES), 0).astype(F32)
    base = jnp.zeros((tk, LANES), F32)
    for hd in range(nh):
        for i, part in enumerate(_split3_bf16(slope2[hd] * k_loc)):
            base = jnp.where(lane == 3 * hd + i, part, base)
    blk_lanes = (lane >= j_lane) & (lane < j_lane + 3)
    for j in range(t // tk):
        aug_scr[j * tk:(j + 1) * tk, :] = jnp.where(blk_lanes, float(j), base).astype(BF16)

    row = lax.broadcasted_iota(jnp.int32, (LANES, 2 * tq), 0)
    kl = lax.broadcasted_iota(jnp.int32, (tk, tq), 0)
    ql = lax.broadcasted_iota(jnp.int32, (tk, tq), 1)
    same_chunk_or_earlier = _div_pow2(kl, CHUNK) <= _div_pow2(ql, CHUNK)
    ahead = jnp.maximum(kl - ql, 0).astype(F32)
    for hd in range(nh):
        rows = jnp.where((row >= 3 * hd) & (row < 3 * hd + 3), 1.0, 0.0)
        for i, part in enumerate(_split3_bf16(jnp.full((LANES, 2 * tq), slope2[hd] * tk, F32))):
            rows = jnp.where(row == j_lane + i, part, rows)
        qt_scr[hd, LANES:2 * LANES, :] = rows.astype(BF16)
        qt_scr[hd, DIFF_DH:2 * DIFF_DH, 0:tq] = jnp.zeros((DIFF_DH, tq), BF16)
        qt_scr[hd, 0:DIFF_DH, tq:2 * tq] = jnp.zeros((DIFF_DH, tq), BF16)
        corr_scr[hd] = jnp.where(same_chunk_or_earlier, (-2.0 * slope2[hd]) * ahead, -jnp.inf)

    lam = _lambda(lam_ref, lam_init)
    gsub = gsub_ref[...]

    def ksteps(steps):
        items = [(j0, nblk, diag, hd) for j0, nblk, diag in steps for hd in range(nh)]

        def key_rows(j0, nblk):
            return pl.ds(pl.multiple_of(j0 * tk, tk), nblk * tk)

        def scores(i):
            j0, nblk, diag, hd = items[i]
            ks = key_rows(j0, nblk)
            k_aug = jnp.concatenate([k_ref[0, hd, ks, :], aug_scr[ks, :]], axis=1)
            s_scr[i % S_SLOTS, 0:nblk * tk, :] = jnp.dot(
                k_aug, qt_scr[hd], preferred_element_type=F32)
            return None

        def softmax(i, s):
            j0, nblk, diag, hd = items[i]
            p_parts, alpha_parts = [], []
            for g in range(2 * tq // LANES):
                lanes = slice(g * LANES, (g + 1) * LANES)
                sg = s_scr[i % S_SLOTS, 0:nblk * tk, lanes]
                if diag:
                    c0 = (g * LANES) % tq
                    sg = sg + corr_scr[hd, :, c0:c0 + LANES]
                m_old = m_scr[hd:hd + 1, lanes]
                m_new = jnp.maximum(m_old, jnp.max(sg, axis=0, keepdims=True))
                m_scr[hd:hd + 1, lanes] = m_new
                p_parts.append(jnp.exp2(sg - m_new).astype(BF16))
                alpha_parts.append(jnp.exp2(m_old - m_new))
            return jnp.concatenate(p_parts, axis=1), jnp.concatenate(alpha_parts, axis=1)

        def accumulate(i, p, alpha):
            j0, nblk, diag, hd = items[i]
            acc_scr[hd] = alpha * acc_scr[hd] + jnp.dot(
                vt_ref[0, hd, :, key_rows(j0, nblk)], p, preferred_element_type=F32)

        n = len(items)
        s_tiles = {i: scores(i) for i in range(min(SCORE_AHEAD, n))}
        pending = None
        for i in range(n):
            if i + SCORE_AHEAD < n:
                s_tiles[i + SCORE_AHEAD] = scores(i + SCORE_AHEAD)
            p_alpha = softmax(i, s_tiles.pop(i))
            if pending is not None:
                accumulate(i - 1, *pending)
            pending = p_alpha
        accumulate(n - 1, *pending)

    def qblock(qi, carry):
        qs = pl.ds(pl.multiple_of(qi * tq, tq), tq)
        for hd in range(nh):
            qt = qt_ref[0, hd, :, qs]
            qt_scr[hd, 0:DIFF_DH, 0:tq] = qt[0:DIFF_DH, :]
            qt_scr[hd, DIFF_DH:2 * DIFF_DH, tq:2 * tq] = qt[DIFF_DH:2 * DIFF_DH, :]
        m_scr[...] = jnp.full(m_scr.shape, -jnp.inf, F32)
        acc_scr[...] = jnp.zeros(acc_scr.shape, F32)

        per_trip = K_UNROLL * K_WIDE

        def off_diag(jj, c):
            ksteps([(jj * per_trip + u * K_WIDE, K_WIDE, False) for u in range(K_UNROLL)])
            return c

        lax.fori_loop(0, qi // per_trip, off_diag, 0)
        done = (qi // per_trip) * per_trip
        size = per_trip // 2
        while size >= 1:
            nblk = min(size, K_WIDE)
            @pl.when((qi & size) != 0)
            def _(done=done, size=size, nblk=nblk):
                ksteps([(done + u * nblk, nblk, False) for u in range(size // nblk)])
            done = done + (qi & size)
            size //= 2
        ksteps([(qi, 1, True)])

        for hd in range(nh):
            acc = acc_scr[hd]
            inv_l = 1.0 / acc[DIFF_DV:DIFF_DV + 1, :]
            o_t = (acc[0:DIFF_DV, 0:tq] * inv_l[:, 0:tq]
                   - lam * (acc[0:DIFF_DV, tq:2 * tq] * inv_l[:, tq:2 * tq]))
            ms = jnp.mean(o_t * o_t, axis=0, keepdims=True)
            o_t = o_t * lax.rsqrt(ms + EPS)
            o_ref[qs, hd * DIFF_DV:(hd + 1) * DIFF_DV] = (
                o_t.T * gsub * (1.0 - lam_init)).astype(BF16)
        return carry

    lax.fori_loop(0, nq, qblock, 0)


def _attn_prompt(slopes, lam_p, qdt, kdb, vdt, g_sub, *, tq, lam_init):
    nb, nh, t, _ = kdb.shape
    per_stream = lambda b, *_: (b, 0, 0, 0)
    grid_spec = pltpu.PrefetchScalarGridSpec(
        num_scalar_prefetch=1,
        grid=(nb,),
        in_specs=[pl.BlockSpec((4, DIFF_DH), lambda b, *_: (0, 0)),
                  pl.BlockSpec((1, nh, LANES, t), per_stream),
                  pl.BlockSpec((1, nh, t, LANES), per_stream),
                  pl.BlockSpec((1, nh, V_AUG_ROWS, t), per_stream),
                  pl.BlockSpec((1, DIFF_DV), lambda b, *_: (0, 0))],
        out_specs=pl.BlockSpec((t, nh * DIFF_DV), lambda b, *_: (b, 0)),
        scratch_shapes=[pltpu.VMEM((t, LANES), BF16),
                        pltpu.VMEM((nh, 2 * LANES, 2 * tq), BF16),
                        pltpu.VMEM((nh, tq, tq), F32),
                        pltpu.VMEM((nh, V_AUG_ROWS, 2 * tq), F32),
                        pltpu.VMEM((8, 2 * tq), F32),
                        pltpu.VMEM((S_SLOTS, K_WIDE * tq, 2 * tq), F32)],
    )
    return pl.pallas_call(
        functools.partial(_attn_prompt_kernel, tq=tq, lam_init=lam_init),
        grid_spec=grid_spec,
        out_shape=jax.ShapeDtypeStruct((nb * t, nh * DIFF_DV), BF16),
        compiler_params=pltpu.CompilerParams(
            dimension_semantics=("arbitrary",), vmem_limit_bytes=VMEM_LIMIT_BYTES),
        name="attn_prompt",
    )(slopes, lam_p, qdt, kdb, vdt, g_sub)


def _attn_sample_kernel(slopes_ref, lam_ref, q_ref, ck_ref, cv_ref, kn_ref, vn_ref, gsub_ref, o_ref,
                        *, tq, past, lam_init):
    nh = DIFF_HEADS
    nt = (((1,), (1,)), ((), ()))
    k_pos = lax.broadcasted_iota(jnp.int32, (1, past), 1).astype(F32)
    qi = lax.broadcasted_iota(jnp.int32, (tq, tq), 0)
    kj = lax.broadcasted_iota(jnp.int32, (tq, tq), 1)
    rel_new = (past + qi - jnp.abs(qi - kj)).astype(F32)
    lam = _lambda(lam_ref, lam_init)
    gsub = gsub_ref[...]
    for hd in range(nh):
        slope = slopes_ref[hd]
        old = pl.ds(hd, past, stride=nh)
        new = pl.ds(hd, tq, stride=nh)
        qq = _split_halves(q_ref[0, hd])
        s_c = lax.dot_general(qq, ck_ref[0, old, :].astype(BF16), nt, preferred_element_type=F32)
        s_c = s_c + slope * k_pos
        s_n = lax.dot_general(qq, kn_ref[0, new, :].astype(BF16), nt, preferred_element_type=F32)
        bias_n = slope * rel_new
        s_n = s_n + jnp.concatenate([bias_n, bias_n], axis=0)
        m = jnp.maximum(jnp.max(s_c, axis=-1, keepdims=True), jnp.max(s_n, axis=-1, keepdims=True))
        p_c = jnp.exp(s_c - m)
        p_n = jnp.exp(s_n - m)
        l = jnp.sum(p_c, axis=-1, keepdims=True) + jnp.sum(p_n, axis=-1, keepdims=True)
        acc = (jnp.dot(p_c.astype(BF16), cv_ref[0, old, :].astype(BF16), preferred_element_type=F32)
               + jnp.dot(p_n.astype(BF16), vn_ref[0, new, :].astype(BF16),
                         preferred_element_type=F32))
        o_ref[:, hd * DIFF_DV:(hd + 1) * DIFF_DV] = _finish_attention(
            acc, l, lam, gsub, lam_init, tq).astype(BF16)


def _attn_sample(slopes, lam_p, qd, cache_k, cache_v, k_new, v_new, g_sub, *, lam_init):
    nb, rows, _ = cache_k.shape
    nh = DIFF_HEADS
    past = rows // nh
    tq = k_new.shape[1] // (nb * nh)
    cache_spec = pl.BlockSpec((1, past * nh, LANES), lambda b, *_: (b, 0, 0))
    new_spec = pl.BlockSpec((1, tq * nh, LANES), lambda b, *_: (0, b, 0))
    grid_spec = pltpu.PrefetchScalarGridSpec(
        num_scalar_prefetch=1,
        grid=(nb,),
        in_specs=[pl.BlockSpec((4, DIFF_DH), lambda b, *_: (0, 0)),
                  pl.BlockSpec((1, nh, tq, LANES), lambda b, *_: (0, 0, b, 0)),
                  cache_spec, cache_spec, new_spec, new_spec,
                  pl.BlockSpec((1, DIFF_DV), lambda b, *_: (0, 0))],
        out_specs=pl.BlockSpec((tq, nh * DIFF_DV), lambda b, *_: (b, 0)),
    )
    return pl.pallas_call(
        functools.partial(_attn_sample_kernel, tq=tq, past=past, lam_init=lam_init),
        grid_spec=grid_spec,
        out_shape=jax.ShapeDtypeStruct((nb * tq, nh * DIFF_DV), BF16),
        compiler_params=pltpu.CompilerParams(
            dimension_semantics=("arbitrary",), vmem_limit_bytes=VMEM_LIMIT_BYTES),
        name="attn_sample",
    )(slopes, lam_p, qd, cache_k, cache_v, k_new, v_new, g_sub)


def _out_ffn_kernel(og_ref, od_ref, x_ref, wo_ref, wu_ref, wd_ref,
                    gpm_ref, gpf_ref, gqf_ref, y_ref, *, ff_chunk):
    mix = (jnp.dot(og_ref[...], wo_ref[:W_V, :], preferred_element_type=F32)
           + jnp.dot(od_ref[...], wo_ref[W_V:, :], preferred_element_type=F32))
    x1 = x_ref[...] + _rms(mix, gpm_ref[...])
    f = _rms(x1, gpf_ref[...]).astype(BF16)
    d_ff = wu_ref.shape[1]
    acc = jnp.zeros(x1.shape, F32)
    for c in range(d_ff // ff_chunk):
        sl = slice(c * ff_chunk, (c + 1) * ff_chunk)
        hid = jnp.dot(f, wu_ref[:, sl], preferred_element_type=F32)
        hid = jnp.square(jnp.maximum(hid, 0.0)).astype(BF16)
        acc = acc + jnp.dot(hid, wd_ref[sl, :], preferred_element_type=F32)
    y_ref[...] = x1 + _rms(acc, gqf_ref[...])


def _out_ffn(og, od, x2d, w_out, w_up, w_down, g_post_mix, g_pre_ffn, g_post_ffn, *, tm):
    n, d = x2d.shape
    row = lambda i: (i, 0)
    return pl.pallas_call(
        functools.partial(_out_ffn_kernel, ff_chunk=1024),
        grid=(n // tm,),
        in_specs=[pl.BlockSpec((tm, W_V), row), pl.BlockSpec((tm, W_V), row),
                  pl.BlockSpec((tm, d), row),
                  _const_spec(w_out.shape), _const_spec(w_up.shape), _const_spec(w_down.shape),
                  _const_spec((1, d)), _const_spec((1, d)), _const_spec((1, d))],
        out_specs=pl.BlockSpec((tm, d), row),
        out_shape=jax.ShapeDtypeStruct((n, d), F32),
        compiler_params=pltpu.CompilerParams(
            dimension_semantics=("arbitrary",), vmem_limit_bytes=VMEM_LIMIT_BYTES),
        name="out_ffn",
    )(og, od, x2d, w_out, w_up, w_down, g_post_mix, g_pre_ffn, g_post_ffn)


def _reorder_w_in(w):
    d = w.shape[0]
    a0 = 2 * W_G + 2 * W_V
    a1 = a0 + GLA_RANK
    return jnp.concatenate(
        [w[:, :a0], w[:, a1:], w[:, a0:a1], jnp.zeros((d, LANES - GLA_RANK), w.dtype)],
        axis=1).astype(BF16)


def kernel(x_prompt, x_sample, cache_k, cache_v, state_gla, w_in, w_gate_up, b_gate, g_gla_out,
           lam_q1, lam_k1, lam_q2, lam_k2, g_subln, w_out, g_pre_mix, g_post_mix,
           g_pre_ffn, g_post_ffn, w_ff_up, w_ff_down):
    depth = w_in.shape[0]
    nb_p, t_p, d = x_prompt.shape
    nb_s, t_s, _ = x_sample.shape
    past = cache_k.shape[2]
    slopes = jnp.exp2(-8.0 / DIFF_HEADS * jnp.arange(1, DIFF_HEADS + 1, dtype=F32))
    yp = x_prompt.reshape(nb_p * t_p, d)
    ys = x_sample.reshape(nb_s * t_s, d)
    outs = [[] for _ in range(6)]
    for l in range(depth):
        lam_init = 0.8 - 0.6 * math.exp(-0.3 * l)
        w_in_r = _reorder_w_in(w_in[l])
        wg_pad = jnp.concatenate(
            [w_gate_up[l], jnp.zeros((LANES - GLA_RANK, W_G), F32)], axis=0).astype(BF16)
        bg = b_gate[l][None, :]
        lam_p = jnp.stack([lam_q1[l], lam_k1[l], lam_q2[l], lam_k2[l]], axis=0)
        g_out = g_gla_out[l][None, :]
        g_sub = g_subln[l][None, :]
        wo = w_out[l].astype(BF16)
        wu = w_ff_up[l].astype(BF16)
        wd = w_ff_down[l].astype(BF16)
        gains = (g_post_mix[l][None, :], g_pre_ffn[l][None, :], g_post_ffn[l][None, :])
        g_pre = g_pre_mix[l][None, :]

        qg, kg, vg, rg, gate, qd, kd, kdb, vd, vdb = _in_proj(
            yp, g_pre, w_in_r, wg_pad, bg, nb=nb_p, t=t_p, tm=512, transposed=True)
        s0 = jnp.zeros((nb_p, GLA_HEADS, GLA_DK, GLA_DV), F32)
        og, s_p = _gla(qg, kg, vg, rg, gate, s0, g_out, nb=nb_p, t=t_p, L=CHUNK, CB=8)
        od = _attn_prompt(slopes, lam_p, qd, kdb, vdb, g_sub, tq=256, lam_init=lam_init)
        yp = _out_ffn(og, od, yp, wo, wu, wd, *gains, tm=512)
        outs[0].append(kd.reshape(nb_p, t_p, DIFF_HEADS, 2 * DIFF_DH))
        outs[1].append(vd.reshape(nb_p, t_p, DIFF_HEADS, DIFF_DV))
        outs[2].append(s_p)

        n_s = nb_s * t_s
        qg, kg, vg, rg, gate, qd, kd, _, vd, _ = _in_proj(
            ys, g_pre, w_in_r, wg_pad, bg, nb=1, t=n_s, tm=n_s, transposed=False)
        og, s_s = _gla(qg, kg, vg, rg, gate, state_gla[l], g_out, nb=nb_s, t=t_s, L=t_s, CB=1)
        od = _attn_sample(slopes, lam_p, qd,
                          cache_k[l].reshape(nb_s, past * DIFF_HEADS, 2 * DIFF_DH),
                          cache_v[l].reshape(nb_s, past * DIFF_HEADS, DIFF_DV),
                          kd, vd, g_sub, lam_init=lam_init)
        ys = _out_ffn(og, od, ys, wo, wu, wd, *gains, tm=n_s)
        outs[3].append(kd.reshape(nb_s, t_s, DIFF_HEADS, 2 * DIFF_DH))
        outs[4].append(vd.reshape(nb_s, t_s, DIFF_HEADS, DIFF_DV))
        outs[5].append(s_s)

    stack = lambda xs: jnp.stack(xs, axis=0)
    return (yp.reshape(nb_p, t_p, d), ys.reshape(nb_s, t_s, d),
            stack(outs[0]), stack(outs[1]), stack(outs[2]),
            stack(outs[3]), stack(outs[4]), stack(outs[5]))
```

```python
import functools
import math

import jax
import jax.numpy as jnp
from jax import lax
from jax.experimental import pallas as pl
from jax.experimental.pallas import tpu as pltpu

F32 = jnp.float32
BF16 = jnp.bfloat16

EPS = 1e-6
CHUNK = 64
GLA_HEADS = 4
GLA_DK = 64
GLA_DV = 128
GLA_RANK = 16
GLA_TAU = 16.0
DIFF_HEADS = 4
DIFF_DH = 64
DIFF_DV = 128

W_G = GLA_HEADS * GLA_DK
W_V = GLA_HEADS * GLA_DV
W_D = DIFF_HEADS * 2 * DIFF_DH
LANES = 128
OFF_QG = 0
OFF_KG = OFF_QG + W_G
OFF_VG = OFF_KG + W_G
OFF_RG = OFF_VG + W_V
OFF_QD = OFF_RG + W_V
OFF_KD = OFF_QD + W_D
OFF_VD = OFF_KD + W_D
OFF_AG = OFF_VD + W_D
W_IN_PADDED = OFF_AG + LANES

BF16_SUBLANES = 16
V_AUG_ROWS = DIFF_DV + BF16_SUBLANES
LOG2E = math.log2(math.e)
K_WIDE = 2
K_UNROLL = 2
SCORE_AHEAD = 2
S_SLOTS = SCORE_AHEAD + 1

VMEM_LIMIT_BYTES = 56 * 1024 * 1024
GLA_SAFE_LOG_DECAY = -60.0


def _rms(x, g):
    ms = jnp.mean(x * x, axis=-1, keepdims=True)
    return x * lax.rsqrt(ms + EPS) * g


def _log_sigmoid(x):
    return jnp.minimum(x, 0.0) - jnp.log(1.0 + jnp.exp(-jnp.abs(x)))


def _div_pow2(x, d):
    assert d & (d - 1) == 0
    return lax.shift_right_arithmetic(x, d.bit_length() - 1)


def _const_spec(shape):
    zeros = (0,) * len(shape)
    return pl.BlockSpec(shape, lambda *_: zeros, pipeline_mode=pl.Buffered(1))


def _in_proj_kernel(x_ref, g_ref, w_ref, wg_ref, bg_ref,
                    qg_ref, kg_ref, vg_ref, rg_ref, gate_ref,
                    qd_ref, kd_ref, kdb_ref, vd_ref, vdb_ref, *, transposed):
    h = _rms(x_ref[...], g_ref[...]).astype(BF16)

    def proj(off, width):
        return jnp.dot(h, w_ref[:, off:off + width], preferred_element_type=F32)

    qg_ref[...] = (proj(OFF_QG, W_G) * (GLA_DK ** -0.5)).astype(BF16)
    kg_ref[...] = proj(OFF_KG, W_G).astype(BF16)
    vg_ref[...] = proj(OFF_VG, W_V).astype(BF16)
    rg_ref[...] = proj(OFF_RG, W_V).astype(BF16)
    qd = proj(OFF_QD, W_D) * (DIFF_DH ** -0.5)
    kd = proj(OFF_KD, W_D)
    vd = proj(OFF_VD, W_D)
    tm = qd.shape[0]
    for hd in range(DIFF_HEADS):
        head_rows = pl.ds(hd, tm, stride=DIFF_HEADS)
        kd_ref[0, head_rows, :] = kd[:, hd * LANES:(hd + 1) * LANES]
        vd_ref[0, head_rows, :] = vd[:, hd * LANES:(hd + 1) * LANES]
    ones_rows = (lax.broadcasted_iota(jnp.int32, (V_AUG_ROWS - DIFF_DV, tm), 0) == 0).astype(BF16)
    for hd in range(DIFF_HEADS):
        sl = slice(hd * LANES, (hd + 1) * LANES)
        kdb_ref[0, hd] = kd[:, sl].astype(BF16)
        if transposed:
            qd_ref[0, hd] = (qd[:, sl] * LOG2E).T.astype(BF16)
            vdb_ref[0, hd, 0:DIFF_DV, :] = vd[:, sl].T.astype(BF16)
            vdb_ref[0, hd, DIFF_DV:V_AUG_ROWS, :] = ones_rows
        else:
            qd_ref[0, hd] = qd[:, sl].astype(BF16)
            vdb_ref[0, hd] = vd[:, sl].astype(BF16)
    ag = proj(OFF_AG, LANES).astype(BF16)
    pre = jnp.dot(ag, wg_ref[...], preferred_element_type=F32) + bg_ref[...]
    gate_ref[...] = _log_sigmoid(pre) * (1.0 / GLA_TAU)


def _in_proj(x2d, g_pre, w_in_r, wg_pad, b_gate, *, nb, t, tm, transposed):
    n, d = x2d.shape
    assert t % tm == 0 and n == nb * t
    steps_per_b = t // tm
    row = lambda i: (i, 0)
    hm = lambda i: (i // steps_per_b, 0, i % steps_per_b, 0)
    hm_t = lambda i: (i // steps_per_b, 0, 0, i % steps_per_b)
    hm_shape = jax.ShapeDtypeStruct((nb, DIFF_HEADS, t, LANES), BF16)
    hm_spec = pl.BlockSpec((1, DIFF_HEADS, tm, LANES), hm)
    kv_shape = jax.ShapeDtypeStruct((nb, t * DIFF_HEADS, LANES), F32)
    kv_spec = pl.BlockSpec((1, tm * DIFF_HEADS, LANES),
                           lambda i: (i // steps_per_b, i % steps_per_b, 0))
    if transposed:
        q_shape = jax.ShapeDtypeStruct((nb, DIFF_HEADS, LANES, t), BF16)
        q_spec = pl.BlockSpec((1, DIFF_HEADS, LANES, tm), hm_t)
        v_shape = jax.ShapeDtypeStruct((nb, DIFF_HEADS, V_AUG_ROWS, t), BF16)
        v_spec = pl.BlockSpec((1, DIFF_HEADS, V_AUG_ROWS, tm), hm_t)
    else:
        q_shape, q_spec, v_shape, v_spec = hm_shape, hm_spec, hm_shape, hm_spec
    out_shape = (
        jax.ShapeDtypeStruct((n, W_G), BF16), jax.ShapeDtypeStruct((n, W_G), BF16),
        jax.ShapeDtypeStruct((n, W_V), BF16), jax.ShapeDtypeStruct((n, W_V), BF16),
        jax.ShapeDtypeStruct((n, W_G), F32),
        q_shape,
        kv_shape, hm_shape,
        kv_shape, v_shape,
    )
    out_specs = (
        pl.BlockSpec((tm, W_G), row), pl.BlockSpec((tm, W_G), row),
        pl.BlockSpec((tm, W_V), row), pl.BlockSpec((tm, W_V), row),
        pl.BlockSpec((tm, W_G), row),
        q_spec,
        kv_spec, hm_spec,
        kv_spec, v_spec,
    )
    return pl.pallas_call(
        functools.partial(_in_proj_kernel, transposed=transposed),
        grid=(n // tm,),
        in_specs=[pl.BlockSpec((tm, d), row), _const_spec((1, d)),
                  _const_spec(w_in_r.shape), _const_spec(wg_pad.shape), _const_spec((1, W_G))],
        out_specs=out_specs,
        out_shape=out_shape,
        compiler_params=pltpu.CompilerParams(
            dimension_semantics=("arbitrary",), vmem_limit_bytes=VMEM_LIMIT_BYTES),
        name="in_proj",
    )(x2d, g_pre, w_in_r, wg_pad, b_gate)


def _gla_kernel(q_ref, k_ref, v_ref, r_ref, gate_ref, s0_ref, gout_ref,
                o_ref, sout_ref, s_scr, oi_scr, qf_scr, kf_scr, bf_scr, b_scr, tri_scr, *, L, CB):
    H, DK, DV = GLA_HEADS, GLA_DK, GLA_DV
    HL = H * L
    step = pl.program_id(1)

    @pl.when(step == 0)
    def _():
        s_scr[...] = s0_ref[0].reshape(H * DK, DV)

    tm = L * CB
    ones_l = jnp.ones((L, DV), BF16)
    lane_head = _div_pow2(lax.broadcasted_iota(jnp.int32, (L, W_G), 1), DK)
    row_i = lax.broadcasted_iota(jnp.int32, (HL, HL), 0)
    col_i = lax.broadcasted_iota(jnp.int32, (HL, HL), 1)
    causal = (col_i <= row_i) & (col_i >= (row_i & ~(L - 1)))
    row_t = lax.broadcasted_iota(jnp.int32, (L, 1), 0)
    gout = gout_ref[...]

    def split_hi_lo(x):
        hi = x.astype(BF16)
        return hi, (x - hi.astype(F32)).astype(BF16)

    @pl.when((pl.program_id(0) == 0) & (step == 0))
    def _():
        tile_r = lax.broadcasted_iota(jnp.int32, (tm, tm), 0)
        tile_c = lax.broadcasted_iota(jnp.int32, (tm, tm), 1)
        tri_scr[...] = jnp.where(
            (tile_c <= tile_r) & (tile_c >= (tile_r & ~(L - 1))), 1.0, 0.0).astype(BF16)

    tri = tri_scr[...]
    g_hi, g_lo = split_hi_lo(gate_ref[...])
    b_scr[...] = (jnp.dot(tri, g_hi, preferred_element_type=F32)
                  + jnp.dot(tri, g_lo, preferred_element_type=F32))
    safe = jnp.min(b_scr[...]) >= GLA_SAFE_LOG_DECAY

    def stack_heads(a):
        return jnp.concatenate(
            [jnp.where(lane_head == hd, a, 0.0) for hd in range(H)], axis=0)

    def intra_factorised(rows, q, k, b, qs, vs):
        ks = stack_heads(k * jnp.exp(-b)).astype(BF16)
        a = lax.dot_general(qs, ks, (((1,), (1,)), ((), ())), preferred_element_type=F32)
        a = jnp.where(causal, a, 0.0).astype(BF16)
        return jnp.dot(a, vs, preferred_element_type=F32)

    def intra_per_token(rows, q, k, b, qs, vs):
        qf_scr[...] = q
        kf_scr[...] = k
        bf_scr[...] = b

        def tok(t, carry_t):
            w = (qf_scr[pl.ds(t, 1), :] * kf_scr[...]
                 * jnp.exp(jnp.minimum(bf_scr[pl.ds(t, 1), :] - bf_scr[...], 0.0)))
            for hd in range(H):
                a_col = jnp.sum(jnp.where(lane_head == hd, w, 0.0), axis=-1, keepdims=True)
                a_col = jnp.where(row_t <= t, a_col, 0.0)
                vh = v_ref[rows, hd * DV:(hd + 1) * DV].astype(F32)
                oi_scr[pl.ds(hd * L + t, 1), :] = jnp.sum(a_col * vh, axis=0, keepdims=True)
            return carry_t

        lax.fori_loop(0, L, tok, 0)
        return oi_scr[...]

    def chunk(rows, s_old, intra):
        b = b_scr[rows, :]
        b_last = b[L - 1:L, :]
        q = q_ref[rows, :].astype(F32)
        k = k_ref[rows, :].astype(F32)
        v = v_ref[rows, :]
        qs = stack_heads(q * jnp.exp(b)).astype(BF16)
        kends = stack_heads(k * jnp.exp(b_last - b)).astype(BF16)
        vs = jnp.concatenate([v[:, hd * DV:(hd + 1) * DV] for hd in range(H)], axis=0)
        o = (jnp.dot(qs, s_old.astype(BF16), preferred_element_type=F32)
             + intra(rows, q, k, b, qs, vs))
        o = _rms(o, gout)
        r = r_ref[rows, :].astype(F32)
        for hd in range(H):
            rh = r[:, hd * DV:(hd + 1) * DV]
            o_ref[rows, hd * DV:(hd + 1) * DV] = (
                o[hd * L:(hd + 1) * L, :] * (rh * jax.nn.sigmoid(rh))).astype(BF16)

        tdot = (((0,), (0,)), ((), ()))
        kv = lax.dot_general(kends, vs, tdot, preferred_element_type=F32)
        c_hi, c_lo = split_hi_lo(gate_ref[rows, :])
        dec = (lax.dot_general(c_hi, ones_l, tdot, preferred_element_type=F32)
               + lax.dot_general(c_lo, ones_l, tdot, preferred_element_type=F32))
        return jnp.exp(dec) * s_old + kv

    @pl.when(safe)
    def _():
        s = s_scr[...]
        for c in range(CB):
            s = chunk(slice(c * L, (c + 1) * L), s, intra_factorised)
        s_scr[...] = s

    @pl.when(jnp.logical_not(safe))
    def _():
        def body(c, carry):
            rows = pl.ds(pl.multiple_of(c * L, L), L)
            s_scr[...] = chunk(rows, s_scr[...], intra_per_token)
            return carry

        lax.fori_loop(0, CB, body, 0)

    @pl.when(step == pl.num_programs(1) - 1)
    def _():
        sout_ref[0] = s_scr[...].reshape(H, DK, DV)


def _gla(qg, kg, vg, rg, gate, s0, g_out, *, nb, t, L, CB):
    n = qg.shape[0]
    tm = L * CB
    steps = t // tm
    row = lambda b, s: (b * steps + s, 0)
    st = lambda b, s: (b, 0, 0, 0)
    state_shape = (1, GLA_HEADS, GLA_DK, GLA_DV)
    return pl.pallas_call(
        functools.partial(_gla_kernel, L=L, CB=CB),
        grid=(nb, steps),
        in_specs=[pl.BlockSpec((tm, W_G), row), pl.BlockSpec((tm, W_G), row),
                  pl.BlockSpec((tm, W_V), row), pl.BlockSpec((tm, W_V), row),
                  pl.BlockSpec((tm, W_G), row), pl.BlockSpec(state_shape, st),
                  pl.BlockSpec((1, GLA_DV), lambda b, s: (0, 0))],
        out_specs=(pl.BlockSpec((tm, W_V), row), pl.BlockSpec(state_shape, st)),
        out_shape=(jax.ShapeDtypeStruct((n, W_V), BF16),
                   jax.ShapeDtypeStruct((nb,) + state_shape[1:], F32)),
        scratch_shapes=[pltpu.VMEM((GLA_HEADS * GLA_DK, GLA_DV), F32),
                        pltpu.VMEM((GLA_HEADS * L, GLA_DV), F32),
                        pltpu.VMEM((L, W_G), F32), pltpu.VMEM((L, W_G), F32),
                        pltpu.VMEM((L, W_G), F32), pltpu.VMEM((tm, W_G), F32),
                        pltpu.VMEM((tm, tm), BF16)],
        compiler_params=pltpu.CompilerParams(
            dimension_semantics=("arbitrary", "arbitrary"), vmem_limit_bytes=VMEM_LIMIT_BYTES),
        name="gla",
    )(qg, kg, vg, rg, gate, s0, g_out)


def _lambda(lam_ref, lam_init):
    lp = lam_ref[...]
    s1 = jnp.sum(lp[0:1, :] * lp[1:2, :], axis=-1, keepdims=True)
    s2 = jnp.sum(lp[2:3, :] * lp[3:4, :], axis=-1, keepdims=True)
    return jnp.exp(s1) - jnp.exp(s2) + lam_init


def _split_halves(q):
    lane = lax.broadcasted_iota(jnp.int32, q.shape, 1)
    zero = jnp.zeros_like(q)
    return jnp.concatenate(
        [jnp.where(lane < DIFF_DH, q, zero), jnp.where(lane >= DIFF_DH, q, zero)], axis=0)


def _finish_attention(acc, l, lam, gsub, lam_init, tq):
    o = acc[:tq] / l[:tq] - lam * (acc[tq:] / l[tq:])
    return _rms(o, gsub) * (1.0 - lam_init)


def _split3_bf16(x):
    hi = x.astype(BF16).astype(F32)
    r = x - hi
    mid = r.astype(BF16).astype(F32)
    lo = (r - mid).astype(BF16).astype(F32)
    return hi, mid, lo


def _attn_prompt_kernel(slopes_ref, lam_ref, qt_ref, k_ref, vt_ref, gsub_ref, o_ref,
                        aug_scr, qt_scr, corr_scr, acc_scr, m_scr, s_scr, *, tq, lam_init):
    nh = DIFF_HEADS
    tk = tq
    t = k_ref.shape[2]
    nq = t // tq
    slope2 = [slopes_ref[hd] * LOG2E for hd in range(nh)]
    j_lane = 3 * nh

    lane = lax.broadcasted_iota(jnp.int32, (tk, LANES), 1)
    k_loc = lax.broadcasted_iota(jnp.int32, (tk, LANES), 0).astype(F32)
    base = jnp.zeros((tk, LANES), F32)
    for hd in range(nh):
        for i, part in enumerate(_split3_bf16(slope2[hd] * k_loc)):
            base = jnp.where(lane == 3 * hd + i, part, base)
    blk_lanes = (lane >= j_lane) & (lane < j_lane + 3)
    for j in range(t // tk):
        aug_scr[j * tk:(j + 1) * tk, :] = jnp.where(blk_lanes, float(j), base).astype(BF16)

    row = lax.broadcasted_iota(jnp.int32, (LANES, 2 * tq), 0)
    kl = lax.broadcasted_iota(jnp.int32, (tk, tq), 0)
    ql = lax.broadcasted_iota(jnp.int32, (tk, tq), 1)
    same_chunk_or_earlier = _div_pow2(kl, CHUNK) <= _div_pow2(ql, CHUNK)
    ahead = jnp.maximum(kl - ql, 0).astype(F32)
    for hd in range(nh):
        rows = jnp.where((row >= 3 * hd) & (row < 3 * hd + 3), 1.0, 0.0)
        for i, part in enumerate(_split3_bf16(jnp.full((LANES, 2 * tq), slope2[hd] * tk, F32))):
            rows = jnp.where(row == j_lane + i, part, rows)
        qt_scr[hd, LANES:2 * LANES, :] = rows.astype(BF16)
        qt_scr[hd, DIFF_DH:2 * DIFF_DH, 0:tq] = jnp.zeros((DIFF_DH, tq), BF16)
        qt_scr[hd, 0:DIFF_DH, tq:2 * tq] = jnp.zeros((DIFF_DH, tq), BF16)
        corr_scr[hd] = jnp.where(same_chunk_or_earlier, (-2.0 * slope2[hd]) * ahead, -jnp.inf)

    lam = _lambda(lam_ref, lam_init)
    gsub = gsub_ref[...]

    def ksteps(steps):
        items = [(j0, nblk, diag, hd) for j0, nblk, diag in steps for hd in range(nh)]

        def key_rows(j0, nblk):
            return pl.ds(pl.multiple_of(j0 * tk, tk), nblk * tk)

        def scores(i):
            j0, nblk, diag, hd = items[i]
            ks = key_rows(j0, nblk)
            k_aug = jnp.concatenate([k_ref[0, hd, ks, :], aug_scr[ks, :]], axis=1)
            s_scr[i % S_SLOTS, 0:nblk * tk, :] = jnp.dot(
                k_aug, qt_scr[hd], preferred_element_type=F32)
            return None

        def softmax(i, s):
            j0, nblk, diag, hd = items[i]
            p_parts, alpha_parts = [], []
            for g in range(2 * tq // LANES):
                lanes = slice(g * LANES, (g + 1) * LANES)
                sg = s_scr[i % S_SLOTS, 0:nblk * tk, lanes]
                if diag:
                    c0 = (g * LANES) % tq
                    sg = sg + corr_scr[hd, :, c0:c0 + LANES]
                m_old = m_scr[hd:hd + 1, lanes]
                m_new = jnp.maximum(m_old, jnp.max(sg, axis=0, keepdims=True))
                m_scr[hd:hd + 1, lanes] = m_new
                p_parts.append(jnp.exp2(sg - m_new).astype(BF16))
                alpha_parts.append(jnp.exp2(m_old - m_new))
            return jnp.concatenate(p_parts, axis=1), jnp.concatenate(alpha_parts, axis=1)

        def accumulate(i, p, alpha):
            j0, nblk, diag, hd = items[i]
            acc_scr[hd] = alpha * acc_scr[hd] + jnp.dot(
                vt_ref[0, hd, :, key_rows(j0, nblk)], p, preferred_element_type=F32)

        n = len(items)
        s_tiles = {i: scores(i) for i in range(min(SCORE_AHEAD, n))}
        pending = None
        for i in range(n):
            if i + SCORE_AHEAD < n:
                s_tiles[i + SCORE_AHEAD] = scores(i + SCORE_AHEAD)
            p_alpha = softmax(i, s_tiles.pop(i))
            if pending is not None:
                accumulate(i - 1, *pending)
            pending = p_alpha
        accumulate(n - 1, *pending)

    def qblock(qi, carry):
        qs = pl.ds(pl.multiple_of(qi * tq, tq), tq)
        for hd in range(nh):
            qt = qt_ref[0, hd, :, qs]
            qt_scr[hd, 0:DIFF_DH, 0:tq] = qt[0:DIFF_DH, :]
            qt_scr[hd, DIFF_DH:2 * DIFF_DH, tq:2 * tq] = qt[DIFF_DH:2 * DIFF_DH, :]
        m_scr[...] = jnp.full(m_scr.shape, -jnp.inf, F32)
        acc_scr[...] = jnp.zeros(acc_scr.shape, F32)

        per_trip = K_UNROLL * K_WIDE

        def off_diag(jj, c):
            ksteps([(jj * per_trip + u * K_WIDE, K_WIDE, False) for u in range(K_UNROLL)])
            return c

        lax.fori_loop(0, qi // per_trip, off_diag, 0)
        done = (qi // per_trip) * per_trip
        for left in range(per_trip):
            @pl.when(qi - done == left)
            def _(left=left):
                steps, off, size = [], 0, per_trip // 2
                while size >= 1:
                    if left & size:
                        nblk = min(size, K_WIDE)
                        steps += [(done + off + u * nblk, nblk, False) for u in range(size // nblk)]
                        off += size
                    size //= 2
                ksteps(steps + [(qi, 1, True)])

        for hd in range(nh):
            acc = acc_scr[hd]
            inv_l = 1.0 / acc[DIFF_DV:DIFF_DV + 1, :]
            o_t = (acc[0:DIFF_DV, 0:tq] * inv_l[:, 0:tq]
                   - lam * (acc[0:DIFF_DV, tq:2 * tq] * inv_l[:, tq:2 * tq]))
            ms = jnp.mean(o_t * o_t, axis=0, keepdims=True)
            o_t = o_t * lax.rsqrt(ms + EPS)
            o_ref[qs, hd * DIFF_DV:(hd + 1) * DIFF_DV] = (
                o_t.T * gsub * (1.0 - lam_init)).astype(BF16)
        return carry

    lax.fori_loop(0, nq, qblock, 0)


def _attn_prompt(slopes, lam_p, qdt, kdb, vdt, g_sub, *, tq, lam_init):
    nb, nh, t, _ = kdb.shape
    per_stream = lambda b, *_: (b, 0, 0, 0)
    grid_spec = pltpu.PrefetchScalarGridSpec(
        num_scalar_prefetch=1,
        grid=(nb,),
        in_specs=[pl.BlockSpec((4, DIFF_DH), lambda b, *_: (0, 0)),
                  pl.BlockSpec((1, nh, LANES, t), per_stream),
                  pl.BlockSpec((1, nh, t, LANES), per_stream),
                  pl.BlockSpec((1, nh, V_AUG_ROWS, t), per_stream),
                  pl.BlockSpec((1, DIFF_DV), lambda b, *_: (0, 0))],
        out_specs=pl.BlockSpec((t, nh * DIFF_DV), lambda b, *_: (b, 0)),
        scratch_shapes=[pltpu.VMEM((t, LANES), BF16),
                        pltpu.VMEM((nh, 2 * LANES, 2 * tq), BF16),
                        pltpu.VMEM((nh, tq, tq), F32),
                        pltpu.VMEM((nh, V_AUG_ROWS, 2 * tq), F32),
                        pltpu.VMEM((8, 2 * tq), F32),
                        pltpu.VMEM((S_SLOTS, K_WIDE * tq, 2 * tq), F32)],
    )
    return pl.pallas_call(
        functools.partial(_attn_prompt_kernel, tq=tq, lam_init=lam_init),
        grid_spec=grid_spec,
        out_shape=jax.ShapeDtypeStruct((nb * t, nh * DIFF_DV), BF16),
        compiler_params=pltpu.CompilerParams(
            dimension_semantics=("arbitrary",), vmem_limit_bytes=VMEM_LIMIT_BYTES),
        name="attn_prompt",
    )(slopes, lam_p, qdt, kdb, vdt, g_sub)


def _attn_sample_kernel(slopes_ref, lam_ref, q_ref, ck_ref, cv_ref, kn_ref, vn_ref, gsub_ref, o_ref,
                        *, tq, past, lam_init):
    nh = DIFF_HEADS
    nt = (((1,), (1,)), ((), ()))
    k_pos = lax.broadcasted_iota(jnp.int32, (1, past), 1).astype(F32)
    qi = lax.broadcasted_iota(jnp.int32, (tq, tq), 0)
    kj = lax.broadcasted_iota(jnp.int32, (tq, tq), 1)
    rel_new = (past + qi - jnp.abs(qi - kj)).astype(F32)
    lam = _lambda(lam_ref, lam_init)
    gsub = gsub_ref[...]
    for hd in range(nh):
        slope = slopes_ref[hd]
        old = pl.ds(hd, past, stride=nh)
        new = pl.ds(hd, tq, stride=nh)
        qq = _split_halves(q_ref[0, hd])
        s_c = lax.dot_general(qq, ck_ref[0, old, :].astype(BF16), nt, preferred_element_type=F32)
        s_c = s_c + slope * k_pos
        s_n = lax.dot_general(qq, kn_ref[0, new, :].astype(BF16), nt, preferred_element_type=F32)
        bias_n = slope * rel_new
        s_n = s_n + jnp.concatenate([bias_n, bias_n], axis=0)
        m = jnp.maximum(jnp.max(s_c, axis=-1, keepdims=True), jnp.max(s_n, axis=-1, keepdims=True))
        p_c = jnp.exp(s_c - m)
        p_n = jnp.exp(s_n - m)
        l = jnp.sum(p_c, axis=-1, keepdims=True) + jnp.sum(p_n, axis=-1, keepdims=True)
        acc = (jnp.dot(p_c.astype(BF16), cv_ref[0, old, :].astype(BF16), preferred_element_type=F32)
               + jnp.dot(p_n.astype(BF16), vn_ref[0, new, :].astype(BF16),
                         preferred_element_type=F32))
        o_ref[:, hd * DIFF_DV:(hd + 1) * DIFF_DV] = _finish_attention(
            acc, l, lam, gsub, lam_init, tq).astype(BF16)


def _attn_sample(slopes, lam_p, qd, cache_k, cache_v, k_new, v_new, g_sub, *, lam_init):
    nb, rows, _ = cache_k.shape
    nh = DIFF_HEADS
    past = rows // nh
    tq = k_new.shape[1] // (nb * nh)
    cache_spec = pl.BlockSpec((1, past * nh, LANES), lambda b, *_: (b, 0, 0))
    new_spec = pl.BlockSpec((1, tq * nh, LANES), lambda b, *_: (0, b, 0))
    grid_spec = pltpu.PrefetchScalarGridSpec(
        num_scalar_prefetch=1,
        grid=(nb,),
        in_specs=[pl.BlockSpec((4, DIFF_DH), lambda b, *_: (0, 0)),
                  pl.BlockSpec((1, nh, tq, LANES), lambda b, *_: (0, 0, b, 0)),
                  cache_spec, cache_spec, new_spec, new_spec,
                  pl.BlockSpec((1, DIFF_DV), lambda b, *_: (0, 0))],
        out_specs=pl.BlockSpec((tq, nh * DIFF_DV), lambda b, *_: (b, 0)),
    )
    return pl.pallas_call(
        functools.partial(_attn_sample_kernel, tq=tq, past=past, lam_init=lam_init),
        grid_spec=grid_spec,
        out_shape=jax.ShapeDtypeStruct((nb * tq, nh * DIFF_DV), BF16),
        compiler_params=pltpu.CompilerParams(
            dimension_semantics=("arbitrary",), vmem_limit_bytes=VMEM_LIMIT_BYTES),
        name="attn_sample",
    )(slopes, lam_p, qd, cache_k, cache_v, k_new, v_new, g_sub)


def _out_ffn_kernel(og_ref, od_ref, x_ref, wo_ref, wu_ref, wd_ref,
                    gpm_ref, gpf_ref, gqf_ref, y_ref, *, ff_chunk):
    mix = (jnp.dot(og_ref[...], wo_ref[:W_V, :], preferred_element_type=F32)
           + jnp.dot(od_ref[...], wo_ref[W_V:, :], preferred_element_type=F32))
    x1 = x_ref[...] + _rms(mix, gpm_ref[...])
    f = _rms(x1, gpf_ref[...]).astype(BF16)
    d_ff = wu_ref.shape[1]
    acc = jnp.zeros(x1.shape, F32)
    for c in range(d_ff // ff_chunk):
        sl = slice(c * ff_chunk, (c + 1) * ff_chunk)
        hid = jnp.dot(f, wu_ref[:, sl], preferred_element_type=F32)
        hid = jnp.square(jnp.maximum(hid, 0.0)).astype(BF16)
        acc = acc + jnp.dot(hid, wd_ref[sl, :], preferred_element_type=F32)
    y_ref[...] = x1 + _rms(acc, gqf_ref[...])


def _out_ffn(og, od, x2d, w_out, w_up, w_down, g_post_mix, g_pre_ffn, g_post_ffn, *, tm):
    n, d = x2d.shape
    row = lambda i: (i, 0)
    return pl.pallas_call(
        functools.partial(_out_ffn_kernel, ff_chunk=1024),
        grid=(n // tm,),
        in_specs=[pl.BlockSpec((tm, W_V), row), pl.BlockSpec((tm, W_V), row),
                  pl.BlockSpec((tm, d), row),
                  _const_spec(w_out.shape), _const_spec(w_up.shape), _const_spec(w_down.shape),
                  _const_spec((1, d)), _const_spec((1, d)), _const_spec((1, d))],
        out_specs=pl.BlockSpec((tm, d), row),
        out_shape=jax.ShapeDtypeStruct((n, d), F32),
        compiler_params=pltpu.CompilerParams(
            dimension_semantics=("arbitrary",), vmem_limit_bytes=VMEM_LIMIT_BYTES),
        name="out_ffn",
    )(og, od, x2d, w_out, w_up, w_down, g_post_mix, g_pre_ffn, g_post_ffn)


def _reorder_w_in(w):
    d = w.shape[0]
    a0 = 2 * W_G + 2 * W_V
    a1 = a0 + GLA_RANK
    return jnp.concatenate(
        [w[:, :a0], w[:, a1:], w[:, a0:a1], jnp.zeros((d, LANES - GLA_RANK), w.dtype)],
        axis=1).astype(BF16)


def kernel(x_prompt, x_sample, cache_k, cache_v, state_gla, w_in, w_gate_up, b_gate, g_gla_out,
           lam_q1, lam_k1, lam_q2, lam_k2, g_subln, w_out, g_pre_mix, g_post_mix,
           g_pre_ffn, g_post_ffn, w_ff_up, w_ff_down):
    depth = w_in.shape[0]
    nb_p, t_p, d = x_prompt.shape
    nb_s, t_s, _ = x_sample.shape
    past = cache_k.shape[2]
    slopes = jnp.exp2(-8.0 / DIFF_HEADS * jnp.arange(1, DIFF_HEADS + 1, dtype=F32))
    yp = x_prompt.reshape(nb_p * t_p, d)
    ys = x_sample.reshape(nb_s * t_s, d)
    outs = [[] for _ in range(6)]
    for l in range(depth):
        lam_init = 0.8 - 0.6 * math.exp(-0.3 * l)
        w_in_r = _reorder_w_in(w_in[l])
        wg_pad = jnp.concatenate(
            [w_gate_up[l], jnp.zeros((LANES - GLA_RANK, W_G), F32)], axis=0).astype(BF16)
        bg = b_gate[l][None, :]
        lam_p = jnp.stack([lam_q1[l], lam_k1[l], lam_q2[l], lam_k2[l]], axis=0)
        g_out = g_gla_out[l][None, :]
        g_sub = g_subln[l][None, :]
        wo = w_out[l].astype(BF16)
        wu = w_ff_up[l].astype(BF16)
        wd = w_ff_down[l].astype(BF16)
        gains = (g_post_mix[l][None, :], g_pre_ffn[l][None, :], g_post_ffn[l][None, :])
        g_pre = g_pre_mix[l][None, :]

        qg, kg, vg, rg, gate, qd, kd, kdb, vd, vdb = _in_proj(
            yp, g_pre, w_in_r, wg_pad, bg, nb=nb_p, t=t_p, tm=1024, transposed=True)
        s0 = jnp.zeros((nb_p, GLA_HEADS, GLA_DK, GLA_DV), F32)
        og, s_p = _gla(qg, kg, vg, rg, gate, s0, g_out, nb=nb_p, t=t_p, L=CHUNK, CB=8)
        od = _attn_prompt(slopes, lam_p, qd, kdb, vdb, g_sub, tq=256, lam_init=lam_init)
        yp = _out_ffn(og, od, yp, wo, wu, wd, *gains, tm=512)
        outs[0].append(kd.reshape(nb_p, t_p, DIFF_HEADS, 2 * DIFF_DH))
        outs[1].append(vd.reshape(nb_p, t_p, DIFF_HEADS, DIFF_DV))
        outs[2].append(s_p)

        n_s = nb_s * t_s
        qg, kg, vg, rg, gate, qd, kd, _, vd, _ = _in_proj(
            ys, g_pre, w_in_r, wg_pad, bg, nb=1, t=n_s, tm=n_s, transposed=False)
        og, s_s = _gla(qg, kg, vg, rg, gate, state_gla[l], g_out, nb=nb_s, t=t_s, L=t_s, CB=1)
        od = _attn_sample(slopes, lam_p, qd,
                          cache_k[l].reshape(nb_s, past * DIFF_HEADS, 2 * DIFF_DH),
                          cache_v[l].reshape(nb_s, past * DIFF_HEADS, DIFF_DV),
                          kd, vd, g_sub, lam_init=lam_init)
        ys = _out_ffn(og, od, ys, wo, wu, wd, *gains, tm=n_s)
        outs[3].append(kd.reshape(nb_s, t_s, DIFF_HEADS, 2 * DIFF_DH))
        outs[4].append(vd.reshape(nb_s, t_s, DIFF_HEADS, DIFF_DV))
        outs[5].append(s_s)

    stack = lambda xs: jnp.stack(xs, axis=0)
    return (yp.reshape(nb_p, t_p, d), ys.reshape(nb_s, t_s, d),
            stack(outs[0]), stack(outs[1]), stack(outs[2]),
            stack(outs[3]), stack(outs[4]), stack(outs[5]))
```

```python
import functools
import math

import jax
import jax.numpy as jnp
from jax import lax
from jax.experimental import pallas as pl
from jax.experimental.pallas import tpu as pltpu

F32 = jnp.float32
BF16 = jnp.bfloat16

EPS = 1e-6
CHUNK = 64
GLA_HEADS = 4
GLA_DK = 64
GLA_DV = 128
GLA_RANK = 16
GLA_TAU = 16.0
DIFF_HEADS = 4
DIFF_DH = 64
DIFF_DV = 128

W_G = GLA_HEADS * GLA_DK
W_V = GLA_HEADS * GLA_DV
W_D = DIFF_HEADS * 2 * DIFF_DH
LANES = 128
OFF_QG = 0
OFF_KG = OFF_QG + W_G
OFF_VG = OFF_KG + W_G
OFF_RG = OFF_VG + W_V
OFF_QD = OFF_RG + W_V
OFF_KD = OFF_QD + W_D
OFF_VD = OFF_KD + W_D
OFF_AG = OFF_VD + W_D
W_IN_PADDED = OFF_AG + LANES

BF16_SUBLANES = 16
V_AUG_ROWS = DIFF_DV + BF16_SUBLANES
LOG2E = math.log2(math.e)
K_WIDE = 2
K_UNROLL = 2
SCORE_AHEAD = 2
S_SLOTS = SCORE_AHEAD + 1

VMEM_LIMIT_BYTES = 56 * 1024 * 1024
GLA_SAFE_LOG_DECAY = -60.0


def _rms(x, g):
    ms = jnp.mean(x * x, axis=-1, keepdims=True)
    return x * lax.rsqrt(ms + EPS) * g


def _log_sigmoid(x):
    return jnp.minimum(x, 0.0) - jnp.log(1.0 + jnp.exp(-jnp.abs(x)))


def _div_pow2(x, d):
    assert d & (d - 1) == 0
    return lax.shift_right_arithmetic(x, d.bit_length() - 1)


def _const_spec(shape):
    zeros = (0,) * len(shape)
    return pl.BlockSpec(shape, lambda *_: zeros, pipeline_mode=pl.Buffered(1))


def _in_proj_kernel(x_ref, g_ref, w_ref, wg_ref, bg_ref,
                    qg_ref, kg_ref, vg_ref, rg_ref, gate_ref,
                    qd_ref, kd_ref, kdb_ref, vd_ref, vdb_ref, *, transposed):
    h = _rms(x_ref[...], g_ref[...]).astype(BF16)

    def proj(off, width):
        return jnp.dot(h, w_ref[:, off:off + width], preferred_element_type=F32)

    qg_ref[...] = (proj(OFF_QG, W_G) * (GLA_DK ** -0.5)).astype(BF16)
    kg_ref[...] = proj(OFF_KG, W_G).astype(BF16)
    vg_ref[...] = proj(OFF_VG, W_V).astype(BF16)
    rg_ref[...] = proj(OFF_RG, W_V).astype(BF16)
    qd = proj(OFF_QD, W_D) * (DIFF_DH ** -0.5)
    kd = proj(OFF_KD, W_D)
    vd = proj(OFF_VD, W_D)
    tm = qd.shape[0]
    for hd in range(DIFF_HEADS):
        head_rows = pl.ds(hd, tm, stride=DIFF_HEADS)
        kd_ref[0, head_rows, :] = kd[:, hd * LANES:(hd + 1) * LANES]
        vd_ref[0, head_rows, :] = vd[:, hd * LANES:(hd + 1) * LANES]
    ones_rows = (lax.broadcasted_iota(jnp.int32, (V_AUG_ROWS - DIFF_DV, tm), 0) == 0).astype(BF16)
    for hd in range(DIFF_HEADS):
        sl = slice(hd * LANES, (hd + 1) * LANES)
        kdb_ref[0, hd] = kd[:, sl].astype(BF16)
        if transposed:
            qd_ref[0, hd] = (qd[:, sl] * LOG2E).T.astype(BF16)
            vdb_ref[0, hd, 0:DIFF_DV, :] = vd[:, sl].T.astype(BF16)
            vdb_ref[0, hd, DIFF_DV:V_AUG_ROWS, :] = ones_rows
        else:
            qd_ref[0, hd] = qd[:, sl].astype(BF16)
            vdb_ref[0, hd] = vd[:, sl].astype(BF16)
    ag = proj(OFF_AG, LANES).astype(BF16)
    pre = jnp.dot(ag, wg_ref[...], preferred_element_type=F32) + bg_ref[...]
    gate_ref[...] = _log_sigmoid(pre) * (1.0 / GLA_TAU)


def _in_proj(x2d, g_pre, w_in_r, wg_pad, b_gate, *, nb, t, tm, transposed):
    n, d = x2d.shape
    assert t % tm == 0 and n == nb * t
    steps_per_b = t // tm
    row = lambda i: (i, 0)
    hm = lambda i: (i // steps_per_b, 0, i % steps_per_b, 0)
    hm_t = lambda i: (i // steps_per_b, 0, 0, i % steps_per_b)
    hm_shape = jax.ShapeDtypeStruct((nb, DIFF_HEADS, t, LANES), BF16)
    hm_spec = pl.BlockSpec((1, DIFF_HEADS, tm, LANES), hm)
    kv_shape = jax.ShapeDtypeStruct((nb, t * DIFF_HEADS, LANES), F32)
    kv_spec = pl.BlockSpec((1, tm * DIFF_HEADS, LANES),
                           lambda i: (i // steps_per_b, i % steps_per_b, 0))
    if transposed:
        q_shape = jax.ShapeDtypeStruct((nb, DIFF_HEADS, LANES, t), BF16)
        q_spec = pl.BlockSpec((1, DIFF_HEADS, LANES, tm), hm_t)
        v_shape = jax.ShapeDtypeStruct((nb, DIFF_HEADS, V_AUG_ROWS, t), BF16)
        v_spec = pl.BlockSpec((1, DIFF_HEADS, V_AUG_ROWS, tm), hm_t)
    else:
        q_shape, q_spec, v_shape, v_spec = hm_shape, hm_spec, hm_shape, hm_spec
    out_shape = (
        jax.ShapeDtypeStruct((n, W_G), BF16), jax.ShapeDtypeStruct((n, W_G), BF16),
        jax.ShapeDtypeStruct((n, W_V), BF16), jax.ShapeDtypeStruct((n, W_V), BF16),
        jax.ShapeDtypeStruct((n, W_G), F32),
        q_shape,
        kv_shape, hm_shape,
        kv_shape, v_shape,
    )
    out_specs = (
        pl.BlockSpec((tm, W_G), row), pl.BlockSpec((tm, W_G), row),
        pl.BlockSpec((tm, W_V), row), pl.BlockSpec((tm, W_V), row),
        pl.BlockSpec((tm, W_G), row),
        q_spec,
        kv_spec, hm_spec,
        kv_spec, v_spec,
    )
    return pl.pallas_call(
        functools.partial(_in_proj_kernel, transposed=transposed),
        grid=(n // tm,),
        in_specs=[pl.BlockSpec((tm, d), row), _const_spec((1, d)),
                  _const_spec(w_in_r.shape), _const_spec(wg_pad.shape), _const_spec((1, W_G))],
        out_specs=out_specs,
        out_shape=out_shape,
        compiler_params=pltpu.CompilerParams(
            dimension_semantics=("arbitrary",), vmem_limit_bytes=VMEM_LIMIT_BYTES),
        name="in_proj",
    )(x2d, g_pre, w_in_r, wg_pad, b_gate)


def _gla_kernel(q_ref, k_ref, v_ref, r_ref, gate_ref, s0_ref, gout_ref,
                o_ref, sout_ref, s_scr, oi_scr, qf_scr, kf_scr, bf_scr, b_scr,
                *, L, CB, carry_state):
    H, DK, DV = GLA_HEADS, GLA_DK, GLA_DV
    HL = H * L
    step = pl.program_id(1)

    if carry_state:
        @pl.when(step == 0)
        def _():
            s_scr[...] = s0_ref[0].reshape(H * DK, DV)

    ones_l = jnp.ones((L, DV), BF16)
    lane_head = _div_pow2(lax.broadcasted_iota(jnp.int32, (L, W_G), 1), DK)
    row_i = lax.broadcasted_iota(jnp.int32, (HL, HL), 0)
    col_i = lax.broadcasted_iota(jnp.int32, (HL, HL), 1)
    causal = (col_i <= row_i) & (col_i >= (row_i & ~(L - 1)))
    row_t = lax.broadcasted_iota(jnp.int32, (L, 1), 0)
    gout = gout_ref[...]

    def split_hi_lo(x):
        hi = x.astype(BF16)
        return hi, (x - hi.astype(F32)).astype(BF16)

    tri = jnp.where(lax.broadcasted_iota(jnp.int32, (L, L), 1)
                    <= lax.broadcasted_iota(jnp.int32, (L, L), 0), 1.0, 0.0).astype(BF16)
    totals = []
    for c in range(CB):
        c_hi, c_lo = split_hi_lo(gate_ref[c * L:(c + 1) * L, :])
        b_c = (jnp.dot(tri, c_hi, preferred_element_type=F32)
               + jnp.dot(tri, c_lo, preferred_element_type=F32))
        b_scr[c * L:(c + 1) * L, :] = b_c
        totals.append(b_c[L - 1:L, :])
    safe = jnp.min(jnp.concatenate(totals, axis=0)) >= GLA_SAFE_LOG_DECAY

    def stack_heads(a):
        return jnp.concatenate(
            [jnp.where(lane_head == hd, a, 0.0) for hd in range(H)], axis=0)

    def intra_factorised(rows, q, k, b, qs, vs):
        ks = stack_heads(k * jnp.exp(-b)).astype(BF16)
        a = lax.dot_general(qs, ks, (((1,), (1,)), ((), ())), preferred_element_type=F32)
        a = jnp.where(causal, a, 0.0).astype(BF16)
        return jnp.dot(a, vs, preferred_element_type=F32)

    def intra_per_token(rows, q, k, b, qs, vs):
        qf_scr[...] = q
        kf_scr[...] = k
        bf_scr[...] = b

        def tok(t, carry_t):
            w = (qf_scr[pl.ds(t, 1), :] * kf_scr[...]
                 * jnp.exp(jnp.minimum(bf_scr[pl.ds(t, 1), :] - bf_scr[...], 0.0)))
            for hd in range(H):
                a_col = jnp.sum(jnp.where(lane_head == hd, w, 0.0), axis=-1, keepdims=True)
                a_col = jnp.where(row_t <= t, a_col, 0.0)
                vh = v_ref[rows, hd * DV:(hd + 1) * DV].astype(F32)
                oi_scr[pl.ds(hd * L + t, 1), :] = jnp.sum(a_col * vh, axis=0, keepdims=True)
            return carry_t

        lax.fori_loop(0, L, tok, 0)
        return oi_scr[...]

    def chunk(rows, s_old, intra):
        b = b_scr[rows, :]
        b_last = b[L - 1:L, :]
        q = q_ref[rows, :].astype(F32)
        k = k_ref[rows, :].astype(F32)
        v = v_ref[rows, :]
        qs = stack_heads(q * jnp.exp(b)).astype(BF16)
        kends = stack_heads(k * jnp.exp(b_last - b)).astype(BF16)
        vs = jnp.concatenate([v[:, hd * DV:(hd + 1) * DV] for hd in range(H)], axis=0)
        o = (jnp.dot(qs, s_old.astype(BF16), preferred_element_type=F32)
             + intra(rows, q, k, b, qs, vs))
        o = _rms(o, gout)
        r = r_ref[rows, :].astype(F32)
        for hd in range(H):
            rh = r[:, hd * DV:(hd + 1) * DV]
            o_ref[rows, hd * DV:(hd + 1) * DV] = (
                o[hd * L:(hd + 1) * L, :] * (rh * jax.nn.sigmoid(rh))).astype(BF16)

        tdot = (((0,), (0,)), ((), ()))
        kv = lax.dot_general(kends, vs, tdot, preferred_element_type=F32)
        c_hi, c_lo = split_hi_lo(gate_ref[rows, :])
        dec = (lax.dot_general(c_hi, ones_l, tdot, preferred_element_type=F32)
               + lax.dot_general(c_lo, ones_l, tdot, preferred_element_type=F32))
        return jnp.exp(dec) * s_old + kv

    def state_in(c):
        return s_scr[...] if carry_state else s0_ref[c].reshape(H * DK, DV)

    def state_out(c, s):
        if carry_state:
            s_scr[...] = s
        else:
            sout_ref[c] = s.reshape(H, DK, DV)

    @pl.when(safe)
    def _():
        s = state_in(0)
        for c in range(CB):
            if not carry_state and c > 0:
                s = state_in(c)
            s = chunk(slice(c * L, (c + 1) * L), s, intra_factorised)
            if not carry_state or c == CB - 1:
                state_out(c, s)

    @pl.when(jnp.logical_not(safe))
    def _():
        def body(c, carry):
            rows = pl.ds(pl.multiple_of(c * L, L), L)
            state_out(c, chunk(rows, state_in(c), intra_per_token))
            return carry

        lax.fori_loop(0, CB, body, 0)

    if carry_state:
        @pl.when(step == pl.num_programs(1) - 1)
        def _():
            sout_ref[0] = s_scr[...].reshape(H, DK, DV)


def _gla(qg, kg, vg, rg, gate, s0, g_out, *, nb, t, L, CB, carry_state):
    n = qg.shape[0]
    tm = L * CB
    if carry_state:
        assert t % tm == 0
        grid = (nb, t // tm)
        steps = t // tm
        row = lambda b, s: (b * steps + s, 0)
        st = lambda b, s: (b, 0, 0, 0)
        state_block = (1, GLA_HEADS, GLA_DK, GLA_DV)
    else:
        assert t == L and nb % CB == 0
        grid = (nb // CB, 1)
        row = lambda b, s: (b, 0)
        st = lambda b, s: (b, 0, 0, 0)
        state_block = (CB, GLA_HEADS, GLA_DK, GLA_DV)
    return pl.pallas_call(
        functools.partial(_gla_kernel, L=L, CB=CB, carry_state=carry_state),
        grid=grid,
        in_specs=[pl.BlockSpec((tm, W_G), row), pl.BlockSpec((tm, W_G), row),
                  pl.BlockSpec((tm, W_V), row), pl.BlockSpec((tm, W_V), row),
                  pl.BlockSpec((tm, W_G), row), pl.BlockSpec(state_block, st),
                  pl.BlockSpec((1, GLA_DV), lambda b, s: (0, 0))],
        out_specs=(pl.BlockSpec((tm, W_V), row), pl.BlockSpec(state_block, st)),
        out_shape=(jax.ShapeDtypeStruct((n, W_V), BF16),
                   jax.ShapeDtypeStruct((nb, GLA_HEADS, GLA_DK, GLA_DV), F32)),
        scratch_shapes=[pltpu.VMEM((GLA_HEADS * GLA_DK, GLA_DV), F32),
                        pltpu.VMEM((GLA_HEADS * L, GLA_DV), F32),
                        pltpu.VMEM((L, W_G), F32), pltpu.VMEM((L, W_G), F32),
                        pltpu.VMEM((L, W_G), F32), pltpu.VMEM((tm, W_G), F32)],
        compiler_params=pltpu.CompilerParams(
            dimension_semantics=("arbitrary", "arbitrary"), vmem_limit_bytes=VMEM_LIMIT_BYTES),
        name="gla",
    )(qg, kg, vg, rg, gate, s0, g_out)


def _lambda(lam_ref, lam_init):
    lp = lam_ref[...]
    s1 = jnp.sum(lp[0:1, :] * lp[1:2, :], axis=-1, keepdims=True)
    s2 = jnp.sum(lp[2:3, :] * lp[3:4, :], axis=-1, keepdims=True)
    return jnp.exp(s1) - jnp.exp(s2) + lam_init


def _split_halves(q):
    lane = lax.broadcasted_iota(jnp.int32, q.shape, 1)
    zero = jnp.zeros_like(q)
    return jnp.concatenate(
        [jnp.where(lane < DIFF_DH, q, zero), jnp.where(lane >= DIFF_DH, q, zero)], axis=0)


def _finish_attention(acc, l, lam, gsub, lam_init, tq):
    o = acc[:tq] / l[:tq] - lam * (acc[tq:] / l[tq:])
    return _rms(o, gsub) * (1.0 - lam_init)


def _split3_bf16(x):
    hi = x.astype(BF16).astype(F32)
    r = x - hi
    mid = r.astype(BF16).astype(F32)
    lo = (r - mid).astype(BF16).astype(F32)
    return hi, mid, lo


def _attn_prompt_kernel(slopes_ref, lam_ref, qt_ref, k_ref, vt_ref, gsub_ref, o_ref,
                        aug_scr, qt_scr, corr_scr, acc_scr, m_scr, s_scr, *, tq, lam_init):
    nh = DIFF_HEADS
    tk = tq
    t = k_ref.shape[2]
    nq = t // tq
    slope2 = [slopes_ref[hd] * LOG2E for hd in range(nh)]
    j_lane = 3 * nh

    lane = lax.broadcasted_iota(jnp.int32, (tk, LANES), 1)
    k_loc = lax.broadcasted_iota(jnp.int32, (tk, LANES), 0).astype(F32)
    base = jnp.zeros((tk, LANES), F32)
    for hd in range(nh):
        for i, part in enumerate(_split3_bf16(slope2[hd] * k_loc)):
            base = jnp.where(lane == 3 * hd + i, part, base)
    blk_lanes = (lane >= j_lane) & (lane < j_lane + 3)
    for j in range(t // tk):
        aug_scr[j * tk:(j + 1) * tk, :] = jnp.where(blk_lanes, float(j), base).astype(BF16)

    row = lax.broadcasted_iota(jnp.int32, (LANES, 2 * tq), 0)
    kl = lax.broadcasted_iota(jnp.int32, (tk, tq), 0)
    ql = lax.broadcasted_iota(jnp.int32, (tk, tq), 1)
    same_chunk_or_earlier = _div_pow2(kl, CHUNK) <= _div_pow2(ql, CHUNK)
    ahead = jnp.maximum(kl - ql, 0).astype(F32)
    for hd in range(nh):
        rows = jnp.where((row >= 3 * hd) & (row < 3 * hd + 3), 1.0, 0.0)
        for i, part in enumerate(_split3_bf16(jnp.full((LANES, 2 * tq), slope2[hd] * tk, F32))):
            rows = jnp.where(row == j_lane + i, part, rows)
        qt_scr[hd, LANES:2 * LANES, :] = rows.astype(BF16)
        qt_scr[hd, DIFF_DH:2 * DIFF_DH, 0:tq] = jnp.zeros((DIFF_DH, tq), BF16)
        qt_scr[hd, 0:DIFF_DH, tq:2 * tq] = jnp.zeros((DIFF_DH, tq), BF16)
        corr_scr[hd] = jnp.where(same_chunk_or_earlier, (-2.0 * slope2[hd]) * ahead, -jnp.inf)

    lam = _lambda(lam_ref, lam_init)
    gsub = gsub_ref[...]

    def ksteps(steps):
        items = [(j0, nblk, diag, hd) for j0, nblk, diag in steps for hd in range(nh)]

        def key_rows(j0, nblk):
            return pl.ds(pl.multiple_of(j0 * tk, tk), nblk * tk)

        def scores(i):
            j0, nblk, diag, hd = items[i]
            ks = key_rows(j0, nblk)
            k_aug = jnp.concatenate([k_ref[0, hd, ks, :], aug_scr[ks, :]], axis=1)
            s_scr[i % S_SLOTS, 0:nblk * tk, :] = jnp.dot(
                k_aug, qt_scr[hd], preferred_element_type=F32)
            return None

        def softmax(i, s):
            j0, nblk, diag, hd = items[i]
            p_parts, alpha_parts = [], []
            for g in range(2 * tq // LANES):
                lanes = slice(g * LANES, (g + 1) * LANES)
                sg = s_scr[i % S_SLOTS, 0:nblk * tk, lanes]
                if diag:
                    c0 = (g * LANES) % tq
                    sg = sg + corr_scr[hd, :, c0:c0 + LANES]
                m_old = m_scr[hd:hd + 1, lanes]
                m_new = jnp.maximum(m_old, jnp.max(sg, axis=0, keepdims=True))
                m_scr[hd:hd + 1, lanes] = m_new
                p_parts.append(jnp.exp2(sg - m_new).astype(BF16))
                alpha_parts.append(jnp.exp2(m_old - m_new))
            return jnp.concatenate(p_parts, axis=1), jnp.concatenate(alpha_parts, axis=1)

        def accumulate(i, p, alpha):
            j0, nblk, diag, hd = items[i]
            acc_scr[hd] = alpha * acc_scr[hd] + jnp.dot(
                vt_ref[0, hd, :, key_rows(j0, nblk)], p, preferred_element_type=F32)

        n = len(items)
        s_tiles = {i: scores(i) for i in range(min(SCORE_AHEAD, n))}
        pending = None
        for i in range(n):
            if i + SCORE_AHEAD < n:
                s_tiles[i + SCORE_AHEAD] = scores(i + SCORE_AHEAD)
            p_alpha = softmax(i, s_tiles.pop(i))
            if pending is not None:
                accumulate(i - 1, *pending)
            pending = p_alpha
        accumulate(n - 1, *pending)

    def qblock(qi, carry):
        qs = pl.ds(pl.multiple_of(qi * tq, tq), tq)
        for hd in range(nh):
            qt = qt_ref[0, hd, :, qs]
            qt_scr[hd, 0:DIFF_DH, 0:tq] = qt[0:DIFF_DH, :]
            qt_scr[hd, DIFF_DH:2 * DIFF_DH, tq:2 * tq] = qt[DIFF_DH:2 * DIFF_DH, :]
        m_scr[...] = jnp.full(m_scr.shape, -jnp.inf, F32)
        acc_scr[...] = jnp.zeros(acc_scr.shape, F32)

        per_trip = K_UNROLL * K_WIDE

        def off_diag(jj, c):
            ksteps([(jj * per_trip + u * K_WIDE, K_WIDE, False) for u in range(K_UNROLL)])
            return c

        lax.fori_loop(0, qi // per_trip, off_diag, 0)
        done = (qi // per_trip) * per_trip
        for left in range(per_trip):
            @pl.when(qi - done == left)
            def _(left=left):
                steps, off, size = [], 0, per_trip // 2
                while size >= 1:
                    if left & size:
                        nblk = min(size, K_WIDE)
                        steps += [(done + off + u * nblk, nblk, False) for u in range(size // nblk)]
                        off += size
                    size //= 2
                ksteps(steps + [(qi, 1, True)])

        for hd in range(nh):
            acc = acc_scr[hd]
            inv_l = 1.0 / acc[DIFF_DV:DIFF_DV + 1, :]
            o_t = (acc[0:DIFF_DV, 0:tq] * inv_l[:, 0:tq]
                   - lam * (acc[0:DIFF_DV, tq:2 * tq] * inv_l[:, tq:2 * tq]))
            ms = jnp.mean(o_t * o_t, axis=0, keepdims=True)
            o_t = o_t * lax.rsqrt(ms + EPS)
            o_ref[qs, hd * DIFF_DV:(hd + 1) * DIFF_DV] = (
                o_t.T * gsub * (1.0 - lam_init)).astype(BF16)
        return carry

    lax.fori_loop(0, nq, qblock, 0)


def _attn_prompt(slopes, lam_p, qdt, kdb, vdt, g_sub, *, tq, lam_init):
    nb, nh, t, _ = kdb.shape
    per_stream = lambda b, *_: (b, 0, 0, 0)
    grid_spec = pltpu.PrefetchScalarGridSpec(
        num_scalar_prefetch=1,
        grid=(nb,),
        in_specs=[pl.BlockSpec((4, DIFF_DH), lambda b, *_: (0, 0)),
                  pl.BlockSpec((1, nh, LANES, t), per_stream),
                  pl.BlockSpec((1, nh, t, LANES), per_stream),
                  pl.BlockSpec((1, nh, V_AUG_ROWS, t), per_stream),
                  pl.BlockSpec((1, DIFF_DV), lambda b, *_: (0, 0))],
        out_specs=pl.BlockSpec((t, nh * DIFF_DV), lambda b, *_: (b, 0)),
        scratch_shapes=[pltpu.VMEM((t, LANES), BF16),
                        pltpu.VMEM((nh, 2 * LANES, 2 * tq), BF16),
                        pltpu.VMEM((nh, tq, tq), F32),
                        pltpu.VMEM((nh, V_AUG_ROWS, 2 * tq), F32),
                        pltpu.VMEM((8, 2 * tq), F32),
                        pltpu.VMEM((S_SLOTS, K_WIDE * tq, 2 * tq), F32)],
    )
    return pl.pallas_call(
        functools.partial(_attn_prompt_kernel, tq=tq, lam_init=lam_init),
        grid_spec=grid_spec,
        out_shape=jax.ShapeDtypeStruct((nb * t, nh * DIFF_DV), BF16),
        compiler_params=pltpu.CompilerParams(
            dimension_semantics=("arbitrary",), vmem_limit_bytes=VMEM_LIMIT_BYTES),
        name="attn_prompt",
    )(slopes, lam_p, qdt, kdb, vdt, g_sub)


def _attn_sample_kernel(slopes_ref, lam_ref, q_ref, ck_ref, cv_ref, kn_ref, vn_ref, gsub_ref, o_ref,
                        *, tq, past, lam_init):
    nh = DIFF_HEADS
    nt = (((1,), (1,)), ((), ()))
    k_pos = lax.broadcasted_iota(jnp.int32, (1, past), 1).astype(F32)
    qi = lax.broadcasted_iota(jnp.int32, (tq, tq), 0)
    kj = lax.broadcasted_iota(jnp.int32, (tq, tq), 1)
    rel_new = (past + qi - jnp.abs(qi - kj)).astype(F32)
    lam = _lambda(lam_ref, lam_init)
    gsub = gsub_ref[...]
    for hd in range(nh):
        slope = slopes_ref[hd]
        old = pl.ds(hd, past, stride=nh)
        new = pl.ds(hd, tq, stride=nh)
        qq = _split_halves(q_ref[0, hd])
        s_c = lax.dot_general(qq, ck_ref[0, old, :].astype(BF16), nt, preferred_element_type=F32)
        s_c = s_c + slope * k_pos
        s_n = lax.dot_general(qq, kn_ref[0, new, :].astype(BF16), nt, preferred_element_type=F32)
        bias_n = slope * rel_new
        s_n = s_n + jnp.concatenate([bias_n, bias_n], axis=0)
        m = jnp.maximum(jnp.max(s_c, axis=-1, keepdims=True), jnp.max(s_n, axis=-1, keepdims=True))
        p_c = jnp.exp(s_c - m)
        p_n = jnp.exp(s_n - m)
        l = jnp.sum(p_c, axis=-1, keepdims=True) + jnp.sum(p_n, axis=-1, keepdims=True)
        acc = (jnp.dot(p_c.astype(BF16), cv_ref[0, old, :].astype(BF16), preferred_element_type=F32)
               + jnp.dot(p_n.astype(BF16), vn_ref[0, new, :].astype(BF16),
                         preferred_element_type=F32))
        o_ref[:, hd * DIFF_DV:(hd + 1) * DIFF_DV] = _finish_attention(
            acc, l, lam, gsub, lam_init, tq).astype(BF16)


def _attn_sample(slopes, lam_p, qd, cache_k, cache_v, k_new, v_new, g_sub, *, lam_init):
    nb, rows, _ = cache_k.shape
    nh = DIFF_HEADS
    past = rows // nh
    tq = k_new.shape[1] // (nb * nh)
    cache_spec = pl.BlockSpec((1, past * nh, LANES), lambda b, *_: (b, 0, 0))
    new_spec = pl.BlockSpec((1, tq * nh, LANES), lambda b, *_: (0, b, 0))
    grid_spec = pltpu.PrefetchScalarGridSpec(
        num_scalar_prefetch=1,
        grid=(nb,),
        in_specs=[pl.BlockSpec((4, DIFF_DH), lambda b, *_: (0, 0)),
                  pl.BlockSpec((1, nh, tq, LANES), lambda b, *_: (0, 0, b, 0)),
                  cache_spec, cache_spec, new_spec, new_spec,
                  pl.BlockSpec((1, DIFF_DV), lambda b, *_: (0, 0))],
        out_specs=pl.BlockSpec((tq, nh * DIFF_DV), lambda b, *_: (b, 0)),
    )
    return pl.pallas_call(
        functools.partial(_attn_sample_kernel, tq=tq, past=past, lam_init=lam_init),
        grid_spec=grid_spec,
        out_shape=jax.ShapeDtypeStruct((nb * tq, nh * DIFF_DV), BF16),
        compiler_params=pltpu.CompilerParams(
            dimension_semantics=("arbitrary",), vmem_limit_bytes=VMEM_LIMIT_BYTES),
        name="attn_sample",
    )(slopes, lam_p, qd, cache_k, cache_v, k_new, v_new, g_sub)


def _out_ffn_kernel(og_ref, od_ref, x_ref, wo_ref, wu_ref, wd_ref,
                    gpm_ref, gpf_ref, gqf_ref, y_ref, *, ff_chunk):
    mix = (jnp.dot(og_ref[...], wo_ref[:W_V, :], preferred_element_type=F32)
           + jnp.dot(od_ref[...], wo_ref[W_V:, :], preferred_element_type=F32))
    x1 = x_ref[...] + _rms(mix, gpm_ref[...])
    f = _rms(x1, gpf_ref[...]).astype(BF16)
    d_ff = wu_ref.shape[1]
    acc = jnp.zeros(x1.shape, F32)
    for c in range(d_ff // ff_chunk):
        sl = slice(c * ff_chunk, (c + 1) * ff_chunk)
        hid = jnp.dot(f, wu_ref[:, sl], preferred_element_type=F32)
        hid = jnp.square(jnp.maximum(hid, 0.0)).astype(BF16)
        acc = acc + jnp.dot(hid, wd_ref[sl, :], preferred_element_type=F32)
    y_ref[...] = x1 + _rms(acc, gqf_ref[...])


def _out_ffn(og, od, x2d, w_out, w_up, w_down, g_post_mix, g_pre_ffn, g_post_ffn, *, tm):
    n, d = x2d.shape
    row = lambda i: (i, 0)
    return pl.pallas_call(
        functools.partial(_out_ffn_kernel, ff_chunk=1024),
        grid=(n // tm,),
        in_specs=[pl.BlockSpec((tm, W_V), row), pl.BlockSpec((tm, W_V), row),
                  pl.BlockSpec((tm, d), row),
                  _const_spec(w_out.shape), _const_spec(w_up.shape), _const_spec(w_down.shape),
                  _const_spec((1, d)), _const_spec((1, d)), _const_spec((1, d))],
        out_specs=pl.BlockSpec((tm, d), row),
        out_shape=jax.ShapeDtypeStruct((n, d), F32),
        compiler_params=pltpu.CompilerParams(
            dimension_semantics=("arbitrary",), vmem_limit_bytes=VMEM_LIMIT_BYTES),
        name="out_ffn",
    )(og, od, x2d, w_out, w_up, w_down, g_post_mix, g_pre_ffn, g_post_ffn)


def _reorder_w_in(w):
    d = w.shape[0]
    a0 = 2 * W_G + 2 * W_V
    a1 = a0 + GLA_RANK
    return jnp.concatenate(
        [w[:, :a0], w[:, a1:], w[:, a0:a1], jnp.zeros((d, LANES - GLA_RANK), w.dtype)],
        axis=1).astype(BF16)


def kernel(x_prompt, x_sample, cache_k, cache_v, state_gla, w_in, w_gate_up, b_gate, g_gla_out,
           lam_q1, lam_k1, lam_q2, lam_k2, g_subln, w_out, g_pre_mix, g_post_mix,
           g_pre_ffn, g_post_ffn, w_ff_up, w_ff_down):
    depth = w_in.shape[0]
    nb_p, t_p, d = x_prompt.shape
    nb_s, t_s, _ = x_sample.shape
    past = cache_k.shape[2]
    slopes = jnp.exp2(-8.0 / DIFF_HEADS * jnp.arange(1, DIFF_HEADS + 1, dtype=F32))
    yp = x_prompt.reshape(nb_p * t_p, d)
    ys = x_sample.reshape(nb_s * t_s, d)
    outs = [[] for _ in range(6)]
    for l in range(depth):
        lam_init = 0.8 - 0.6 * math.exp(-0.3 * l)
        w_in_r = _reorder_w_in(w_in[l])
        wg_pad = jnp.concatenate(
            [w_gate_up[l], jnp.zeros((LANES - GLA_RANK, W_G), F32)], axis=0).astype(BF16)
        bg = b_gate[l][None, :]
        lam_p = jnp.stack([lam_q1[l], lam_k1[l], lam_q2[l], lam_k2[l]], axis=0)
        g_out = g_gla_out[l][None, :]
        g_sub = g_subln[l][None, :]
        wo = w_out[l].astype(BF16)
        wu = w_ff_up[l].astype(BF16)
        wd = w_ff_down[l].astype(BF16)
        gains = (g_post_mix[l][None, :], g_pre_ffn[l][None, :], g_post_ffn[l][None, :])
        g_pre = g_pre_mix[l][None, :]

        qg, kg, vg, rg, gate, qd, kd, kdb, vd, vdb = _in_proj(
            yp, g_pre, w_in_r, wg_pad, bg, nb=nb_p, t=t_p, tm=1024, transposed=True)
        s0 = jnp.zeros((nb_p, GLA_HEADS, GLA_DK, GLA_DV), F32)
        og, s_p = _gla(qg, kg, vg, rg, gate, s0, g_out, nb=nb_p, t=t_p, L=CHUNK, CB=8,
                       carry_state=True)
        od = _attn_prompt(slopes, lam_p, qd, kdb, vdb, g_sub, tq=256, lam_init=lam_init)
        yp = _out_ffn(og, od, yp, wo, wu, wd, *gains, tm=1024)
        outs[0].append(kd.reshape(nb_p, t_p, DIFF_HEADS, 2 * DIFF_DH))
        outs[1].append(vd.reshape(nb_p, t_p, DIFF_HEADS, DIFF_DV))
        outs[2].append(s_p)

        n_s = nb_s * t_s
        qg, kg, vg, rg, gate, qd, kd, _, vd, _ = _in_proj(
            ys, g_pre, w_in_r, wg_pad, bg, nb=1, t=n_s, tm=n_s, transposed=False)
        og, s_s = _gla(qg, kg, vg, rg, gate, state_gla[l], g_out, nb=nb_s, t=t_s, L=t_s,
                       CB=math.gcd(nb_s, 8), carry_state=False)
        od = _attn_sample(slopes, lam_p, qd,
                          cache_k[l].reshape(nb_s, past * DIFF_HEADS, 2 * DIFF_DH),
                          cache_v[l].reshape(nb_s, past * DIFF_HEADS, DIFF_DV),
                          kd, vd, g_sub, lam_init=lam_init)
        ys = _out_ffn(og, od, ys, wo, wu, wd, *gains, tm=n_s)
        outs[3].append(kd.reshape(nb_s, t_s, DIFF_HEADS, 2 * DIFF_DH))
        outs[4].append(vd.reshape(nb_s, t_s, DIFF_HEADS, DIFF_DV))
        outs[5].append(s_s)

    stack = lambda xs: jnp.stack(xs, axis=0)
    return (yp.reshape(nb_p, t_p, d), ys.reshape(nb_s, t_s, d),
            stack(outs[0]), stack(outs[1]), stack(outs[2]),
            stack(outs[3]), stack(outs[4]), stack(outs[5]))
```

```python
import functools
import math

import jax
import jax.numpy as jnp
from jax import lax
from jax.experimental import pallas as pl
from jax.experimental.pallas import tpu as pltpu

F32 = jnp.float32
BF16 = jnp.bfloat16

EPS = 1e-6
CHUNK = 64
GLA_HEADS = 4
GLA_DK = 64
GLA_DV = 128
GLA_RANK = 16
GLA_TAU = 16.0
DIFF_HEADS = 4
DIFF_DH = 64
DIFF_DV = 128

W_G = GLA_HEADS * GLA_DK
W_V = GLA_HEADS * GLA_DV
W_D = DIFF_HEADS * 2 * DIFF_DH
LANES = 128
OFF_QG = 0
OFF_KG = OFF_QG + W_G
OFF_VG = OFF_KG + W_G
OFF_RG = OFF_VG + W_V
OFF_QD = 0
OFF_KD = OFF_QD + W_D
OFF_VD = OFF_KD + W_D

BF16_SUBLANES = 16
V_AUG_ROWS = DIFF_DV + BF16_SUBLANES
LOG2E = math.log2(math.e)
K_WIDE = 2
K_UNROLL = 2
SCORE_AHEAD = 2
S_SLOTS = SCORE_AHEAD + 1

VMEM_LIMIT_BYTES = 56 * 1024 * 1024
GLA_SAFE_LOG_DECAY = -60.0


def _rms(x, g):
    ms = jnp.mean(x * x, axis=-1, keepdims=True)
    return x * lax.rsqrt(ms + EPS) * g


def _log_sigmoid(x):
    return jnp.minimum(x, 0.0) - jnp.log(1.0 + jnp.exp(-jnp.abs(x)))


def _div_pow2(x, d):
    assert d & (d - 1) == 0
    return lax.shift_right_arithmetic(x, d.bit_length() - 1)


def _const_spec(shape):
    zeros = (0,) * len(shape)
    return pl.BlockSpec(shape, lambda *_: zeros, pipeline_mode=pl.Buffered(1))


def _in_proj_kernel(x_ref, g_ref, w_gla_ref, w_diff_ref, w_a_ref, wg_ref, bg_ref,
                    qg_ref, kg_ref, vg_ref, rg_ref, gate_ref,
                    qd_ref, kd_ref, kdb_ref, vd_ref, vdb_ref, *, transposed):
    h = _rms(x_ref[...], g_ref[...]).astype(BF16)

    def proj(w_ref, off, width):
        return jnp.dot(h, w_ref[:, off:off + width], preferred_element_type=F32)

    qg_ref[...] = (proj(w_gla_ref, OFF_QG, W_G) * (GLA_DK ** -0.5)).astype(BF16)
    kg_ref[...] = proj(w_gla_ref, OFF_KG, W_G).astype(BF16)
    vg_ref[...] = proj(w_gla_ref, OFF_VG, W_V).astype(BF16)
    rg_ref[...] = proj(w_gla_ref, OFF_RG, W_V).astype(BF16)
    qd = proj(w_diff_ref, OFF_QD, W_D) * (DIFF_DH ** -0.5)
    kd = proj(w_diff_ref, OFF_KD, W_D)
    vd = proj(w_diff_ref, OFF_VD, W_D)
    tm = qd.shape[0]
    for hd in range(DIFF_HEADS):
        head_rows = pl.ds(hd, tm, stride=DIFF_HEADS)
        kd_ref[0, head_rows, :] = kd[:, hd * LANES:(hd + 1) * LANES]
        vd_ref[0, head_rows, :] = vd[:, hd * LANES:(hd + 1) * LANES]
    ones_rows = (lax.broadcasted_iota(jnp.int32, (V_AUG_ROWS - DIFF_DV, tm), 0) == 0).astype(BF16)
    for hd in range(DIFF_HEADS):
        sl = slice(hd * LANES, (hd + 1) * LANES)
        kdb_ref[0, hd] = kd[:, sl].astype(BF16)
        if transposed:
            qd_ref[0, hd] = (qd[:, sl] * LOG2E).T.astype(BF16)
            vdb_ref[0, hd, 0:DIFF_DV, :] = vd[:, sl].T.astype(BF16)
            vdb_ref[0, hd, DIFF_DV:V_AUG_ROWS, :] = ones_rows
        else:
            qd_ref[0, hd] = qd[:, sl].astype(BF16)
            vdb_ref[0, hd] = vd[:, sl].astype(BF16)
    ag = proj(w_a_ref, 0, LANES).astype(BF16)
    pre = jnp.dot(ag, wg_ref[...], preferred_element_type=F32) + bg_ref[...]
    gate_ref[...] = _log_sigmoid(pre) * (1.0 / GLA_TAU)


def _in_proj(x2d, g_pre, w_parts, wg_pad, b_gate, *, nb, t, tm, transposed):
    n, d = x2d.shape
    assert t % tm == 0 and n == nb * t
    steps_per_b = t // tm
    row = lambda i: (i, 0)
    hm = lambda i: (i // steps_per_b, 0, i % steps_per_b, 0)
    hm_t = lambda i: (i // steps_per_b, 0, 0, i % steps_per_b)
    hm_shape = jax.ShapeDtypeStruct((nb, DIFF_HEADS, t, LANES), BF16)
    hm_spec = pl.BlockSpec((1, DIFF_HEADS, tm, LANES), hm)
    kv_shape = jax.ShapeDtypeStruct((nb, t * DIFF_HEADS, LANES), F32)
    kv_spec = pl.BlockSpec((1, tm * DIFF_HEADS, LANES),
                           lambda i: (i // steps_per_b, i % steps_per_b, 0))
    if transposed:
        q_shape = jax.ShapeDtypeStruct((nb, DIFF_HEADS, LANES, t), BF16)
        q_spec = pl.BlockSpec((1, DIFF_HEADS, LANES, tm), hm_t)
        v_shape = jax.ShapeDtypeStruct((nb, DIFF_HEADS, V_AUG_ROWS, t), BF16)
        v_spec = pl.BlockSpec((1, DIFF_HEADS, V_AUG_ROWS, tm), hm_t)
    else:
        q_shape, q_spec, v_shape, v_spec = hm_shape, hm_spec, hm_shape, hm_spec
    out_shape = (
        jax.ShapeDtypeStruct((n, W_G), BF16), jax.ShapeDtypeStruct((n, W_G), BF16),
        jax.ShapeDtypeStruct((n, W_V), BF16), jax.ShapeDtypeStruct((n, W_V), BF16),
        jax.ShapeDtypeStruct((n, W_G), F32),
        q_shape,
        kv_shape, hm_shape,
        kv_shape, v_shape,
    )
    out_specs = (
        pl.BlockSpec((tm, W_G), row), pl.BlockSpec((tm, W_G), row),
        pl.BlockSpec((tm, W_V), row), pl.BlockSpec((tm, W_V), row),
        pl.BlockSpec((tm, W_G), row),
        q_spec,
        kv_spec, hm_spec,
        kv_spec, v_spec,
    )
    return pl.pallas_call(
        functools.partial(_in_proj_kernel, transposed=transposed),
        grid=(n // tm,),
        in_specs=[pl.BlockSpec((tm, d), row), _const_spec((1, d)),
                  *[_const_spec(w.shape) for w in w_parts],
                  _const_spec(wg_pad.shape), _const_spec((1, W_G))],
        out_specs=out_specs,
        out_shape=out_shape,
        compiler_params=pltpu.CompilerParams(
            dimension_semantics=("arbitrary",), vmem_limit_bytes=VMEM_LIMIT_BYTES),
        name="in_proj",
    )(x2d, g_pre, *w_parts, wg_pad, b_gate)


def _gla_kernel(q_ref, k_ref, v_ref, r_ref, gate_ref, s0_ref, gout_ref,
                o_ref, sout_ref, s_scr, oi_scr, qf_scr, kf_scr, bf_scr, b_scr,
                *, L, CB, carry_state):
    H, DK, DV = GLA_HEADS, GLA_DK, GLA_DV
    HL = H * L
    step = pl.program_id(1)

    if carry_state:
        @pl.when(step == 0)
        def _():
            s_scr[...] = s0_ref[0].reshape(H * DK, DV).T

    lane_head = _div_pow2(lax.broadcasted_iota(jnp.int32, (L, W_G), 1), DK)
    row_i = lax.broadcasted_iota(jnp.int32, (HL, HL), 0)
    col_i = lax.broadcasted_iota(jnp.int32, (HL, HL), 1)
    causal = (col_i <= row_i) & (col_i >= (row_i & ~(L - 1)))
    row_t = lax.broadcasted_iota(jnp.int32, (L, 1), 0)
    nt = (((1,), (1,)), ((), ()))
    tn = (((0,), (0,)), ((), ()))
    gout = gout_ref[...]

    def split_hi_lo(x):
        hi = x.astype(BF16)
        return hi, (x - hi.astype(F32)).astype(BF16)

    tri = jnp.where(lax.broadcasted_iota(jnp.int32, (L, L), 1)
                    <= lax.broadcasted_iota(jnp.int32, (L, L), 0), 1.0, 0.0).astype(BF16)
    totals = []
    for c in range(CB):
        c_hi, c_lo = split_hi_lo(gate_ref[c * L:(c + 1) * L, :])
        b_c = (jnp.dot(tri, c_hi, preferred_element_type=F32)
               + jnp.dot(tri, c_lo, preferred_element_type=F32))
        b_scr[c * L:(c + 1) * L, :] = b_c
        totals.append(b_c[L - 1:L, :])
    safe = jnp.min(jnp.concatenate(totals, axis=0)) >= GLA_SAFE_LOG_DECAY

    def stack_heads(a):
        return jnp.concatenate(
            [jnp.where(lane_head == hd, a, 0.0) for hd in range(H)], axis=0)

    def intra_factorised(rows, q, k, b, qs, vs):
        ks = stack_heads(k * jnp.exp(-b)).astype(BF16)
        a = lax.dot_general(qs, ks, nt, preferred_element_type=F32)
        a = jnp.where(causal, a, 0.0).astype(BF16)
        return jnp.dot(a, vs, preferred_element_type=F32)

    def intra_per_token(rows, q, k, b, qs, vs):
        qf_scr[...] = q
        kf_scr[...] = k
        bf_scr[...] = b

        def tok(t, carry_t):
            w = (qf_scr[pl.ds(t, 1), :] * kf_scr[...]
                 * jnp.exp(jnp.minimum(bf_scr[pl.ds(t, 1), :] - bf_scr[...], 0.0)))
            for hd in range(H):
                a_col = jnp.sum(jnp.where(lane_head == hd, w, 0.0), axis=-1, keepdims=True)
                a_col = jnp.where(row_t <= t, a_col, 0.0)
                vh = v_ref[rows, hd * DV:(hd + 1) * DV].astype(F32)
                oi_scr[pl.ds(hd * L + t, 1), :] = jnp.sum(a_col * vh, axis=0, keepdims=True)
            return carry_t

        lax.fori_loop(0, L, tok, 0)
        return oi_scr[...]

    def chunk(rows, s_old, intra):
        b = b_scr[rows, :]
        b_last = b[L - 1:L, :]
        q = q_ref[rows, :].astype(F32)
        k = k_ref[rows, :].astype(F32)
        v = v_ref[rows, :]
        qs = stack_heads(q * jnp.exp(b)).astype(BF16)
        kends = stack_heads(k * jnp.exp(b_last - b)).astype(BF16)
        vs = jnp.concatenate([v[:, hd * DV:(hd + 1) * DV] for hd in range(H)], axis=0)
        o = (lax.dot_general(qs, s_old.astype(BF16), nt, preferred_element_type=F32)
             + intra(rows, q, k, b, qs, vs))
        o = _rms(o, gout)
        r = r_ref[rows, :].astype(F32)
        for hd in range(H):
            rh = r[:, hd * DV:(hd + 1) * DV]
            o_ref[rows, hd * DV:(hd + 1) * DV] = (
                o[hd * L:(hd + 1) * L, :] * (rh * jax.nn.sigmoid(rh))).astype(BF16)

        kv_t = lax.dot_general(vs, kends, tn, preferred_element_type=F32)
        return jnp.exp(b_last) * s_old + kv_t

    def state_in(c):
        return s_scr[...] if carry_state else s0_ref[c].reshape(H * DK, DV).T

    def state_out(c, s):
        if carry_state:
            s_scr[...] = s
        else:
            sout_ref[c] = s.T.reshape(H, DK, DV)

    @pl.when(safe)
    def _():
        s = state_in(0)
        for c in range(CB):
            if not carry_state and c > 0:
                s = state_in(c)
            s = chunk(slice(c * L, (c + 1) * L), s, intra_factorised)
            if not carry_state or c == CB - 1:
                state_out(c, s)

    @pl.when(jnp.logical_not(safe))
    def _():
        def body(c, carry):
            rows = pl.ds(pl.multiple_of(c * L, L), L)
            state_out(c, chunk(rows, state_in(c), intra_per_token))
            return carry

        lax.fori_loop(0, CB, body, 0)

    if carry_state:
        @pl.when(step == pl.num_programs(1) - 1)
        def _():
            sout_ref[0] = s_scr[...].T.reshape(H, DK, DV)


def _gla(qg, kg, vg, rg, gate, s0, g_out, *, nb, t, L, CB, carry_state):
    n = qg.shape[0]
    tm = L * CB
    if carry_state:
        assert t % tm == 0
        grid = (nb, t // tm)
        steps = t // tm
        row = lambda b, s: (b * steps + s, 0)
        st = lambda b, s: (b, 0, 0, 0)
        state_block = (1, GLA_HEADS, GLA_DK, GLA_DV)
    else:
        assert t == L and nb % CB == 0
        grid = (nb // CB, 1)
        row = lambda b, s: (b, 0)
        st = lambda b, s: (b, 0, 0, 0)
        state_block = (CB, GLA_HEADS, GLA_DK, GLA_DV)
    return pl.pallas_call(
        functools.partial(_gla_kernel, L=L, CB=CB, carry_state=carry_state),
        grid=grid,
        in_specs=[pl.BlockSpec((tm, W_G), row), pl.BlockSpec((tm, W_G), row),
                  pl.BlockSpec((tm, W_V), row), pl.BlockSpec((tm, W_V), row),
                  pl.BlockSpec((tm, W_G), row), pl.BlockSpec(state_block, st),
                  pl.BlockSpec((1, GLA_DV), lambda b, s: (0, 0))],
        out_specs=(pl.BlockSpec((tm, W_V), row), pl.BlockSpec(state_block, st)),
        out_shape=(jax.ShapeDtypeStruct((n, W_V), BF16),
                   jax.ShapeDtypeStruct((nb, GLA_HEADS, GLA_DK, GLA_DV), F32)),
        scratch_shapes=[pltpu.VMEM((GLA_DV, GLA_HEADS * GLA_DK), F32),
                        pltpu.VMEM((GLA_HEADS * L, GLA_DV), F32),
                        pltpu.VMEM((L, W_G), F32), pltpu.VMEM((L, W_G), F32),
                        pltpu.VMEM((L, W_G), F32), pltpu.VMEM((tm, W_G), F32)],
        compiler_params=pltpu.CompilerParams(
            dimension_semantics=("arbitrary", "arbitrary"), vmem_limit_bytes=VMEM_LIMIT_BYTES),
        name="gla",
    )(qg, kg, vg, rg, gate, s0, g_out)


def _lambda(lam_ref, lam_init):
    lp = lam_ref[...]
    s1 = jnp.sum(lp[0:1, :] * lp[1:2, :], axis=-1, keepdims=True)
    s2 = jnp.sum(lp[2:3, :] * lp[3:4, :], axis=-1, keepdims=True)
    return jnp.exp(s1) - jnp.exp(s2) + lam_init


def _split_halves(q):
    lane = lax.broadcasted_iota(jnp.int32, q.shape, 1)
    zero = jnp.zeros_like(q)
    return jnp.concatenate(
        [jnp.where(lane < DIFF_DH, q, zero), jnp.where(lane >= DIFF_DH, q, zero)], axis=0)


def _finish_attention(acc, l, lam, gsub, lam_init, tq):
    o = acc[:tq] / l[:tq] - lam * (acc[tq:] / l[tq:])
    return _rms(o, gsub) * (1.0 - lam_init)


def _split3_bf16(x):
    hi = x.astype(BF16).astype(F32)
    r = x - hi
    mid = r.astype(BF16).astype(F32)
    lo = (r - mid).astype(BF16).astype(F32)
    return hi, mid, lo


def _attn_prompt_kernel(slopes_ref, lam_ref, qt_ref, k_ref, vt_ref, gsub_ref, o_ref,
                        aug_scr, qt_scr, corr_scr, acc_scr, m_scr, s_scr, *, tq, lam_init):
    nh = DIFF_HEADS
    tk = tq
    t = k_ref.shape[2]
    nq = t // tq
    slope2 = [slopes_ref[hd] * LOG2E for hd in range(nh)]
    j_lane = 3 * nh

    lane = lax.broadcasted_iota(jnp.int32, (tk, LANES), 1)
    k_loc = lax.broadcasted_iota(jnp.int32, (tk, LANES), 0).astype(F32)
    base = jnp.zeros((tk, LANES), F32)
    for hd in range(nh):
        for i, part in enumerate(_split3_bf16(slope2[hd] * k_loc)):
            base = jnp.where(lane == 3 * hd + i, part, base)
    blk_lanes = (lane >= j_lane) & (lane < j_lane + 3)
    for j in range(t // tk):
        aug_scr[j * tk:(j + 1) * tk, :] = jnp.where(blk_lanes, float(j), base).astype(BF16)

    row = lax.broadcasted_iota(jnp.int32, (LANES, 2 * tq), 0)
    kl = lax.broadcasted_iota(jnp.int32, (tk, tq), 0)
    ql = lax.broadcasted_iota(jnp.int32, (tk, tq), 1)
    same_chunk_or_earlier = _div_pow2(kl, CHUNK) <= _div_pow2(ql, CHUNK)
    ahead = jnp.maximum(kl - ql, 0).astype(F32)
    for hd in range(nh):
        rows = jnp.where((row >= 3 * hd) & (row < 3 * hd + 3), 1.0, 0.0)
        for i, part in enumerate(_split3_bf16(jnp.full((LANES, 2 * tq), slope2[hd] * tk, F32))):
            rows = jnp.where(row == j_lane + i, part, rows)
        qt_scr[hd, LANES:2 * LANES, :] = rows.astype(BF16)
        qt_scr[hd, DIFF_DH:2 * DIFF_DH, 0:tq] = jnp.zeros((DIFF_DH, tq), BF16)
        qt_scr[hd, 0:DIFF_DH, tq:2 * tq] = jnp.zeros((DIFF_DH, tq), BF16)
        corr_scr[hd] = jnp.where(same_chunk_or_earlier, (-2.0 * slope2[hd]) * ahead, -jnp.inf)

    lam = _lambda(lam_ref, lam_init)
    gsub = gsub_ref[...]

    def ksteps(steps):
        items = [(j0, nblk, diag, hd) for j0, nblk, diag in steps for hd in range(nh)]

        def key_rows(j0, nblk):
            return pl.ds(pl.multiple_of(j0 * tk, tk), nblk * tk)

        def scores(i):
            j0, nblk, diag, hd = items[i]
            ks = key_rows(j0, nblk)
            k_aug = jnp.concatenate([k_ref[0, hd, ks, :], aug_scr[ks, :]], axis=1)
            s_scr[i % S_SLOTS, 0:nblk * tk, :] = jnp.dot(
                k_aug, qt_scr[hd], preferred_element_type=F32)
            return None

        def softmax(i, s):
            j0, nblk, diag, hd = items[i]
            p_parts, alpha_parts = [], []
            for g in range(2 * tq // LANES):
                lanes = slice(g * LANES, (g + 1) * LANES)
                sg = s_scr[i % S_SLOTS, 0:nblk * tk, lanes]
                if diag:
                    c0 = (g * LANES) % tq
                    sg = sg + corr_scr[hd, :, c0:c0 + LANES]
                m_old = m_scr[hd:hd + 1, lanes]
                m_new = jnp.maximum(m_old, jnp.max(sg, axis=0, keepdims=True))
                m_scr[hd:hd + 1, lanes] = m_new
                p_parts.append(jnp.exp2(sg - m_new).astype(BF16))
                alpha_parts.append(jnp.exp2(m_old - m_new))
            return jnp.concatenate(p_parts, axis=1), jnp.concatenate(alpha_parts, axis=1)

        def accumulate(i, p, alpha):
            j0, nblk, diag, hd = items[i]
            acc_scr[hd] = alpha * acc_scr[hd] + jnp.dot(
                vt_ref[0, hd, :, key_rows(j0, nblk)], p, preferred_element_type=F32)

        n = len(items)
        s_tiles = {i: scores(i) for i in range(min(SCORE_AHEAD, n))}
        pending = None
        for i in range(n):
            if i + SCORE_AHEAD < n:
                s_tiles[i + SCORE_AHEAD] = scores(i + SCORE_AHEAD)
            p_alpha = softmax(i, s_tiles.pop(i))
            if pending is not None:
                accumulate(i - 1, *pending)
            pending = p_alpha
        accumulate(n - 1, *pending)

    def qblock(qi, carry):
        qs = pl.ds(pl.multiple_of(qi * tq, tq), tq)
        for hd in range(nh):
            qt = qt_ref[0, hd, :, qs]
            qt_scr[hd, 0:DIFF_DH, 0:tq] = qt[0:DIFF_DH, :]
            qt_scr[hd, DIFF_DH:2 * DIFF_DH, tq:2 * tq] = qt[DIFF_DH:2 * DIFF_DH, :]
        m_scr[...] = jnp.full(m_scr.shape, -jnp.inf, F32)
        acc_scr[...] = jnp.zeros(acc_scr.shape, F32)

        per_trip = K_UNROLL * K_WIDE

        def off_diag(jj, c):
            ksteps([(jj * per_trip + u * K_WIDE, K_WIDE, False) for u in range(K_UNROLL)])
            return c

        lax.fori_loop(0, qi // per_trip, off_diag, 0)
        done = (qi // per_trip) * per_trip
        for left in range(per_trip):
            @pl.when(qi - done == left)
            def _(left=left):
                steps, off, size = [], 0, per_trip // 2
                while size >= 1:
                    if left & size:
                        nblk = min(size, K_WIDE)
                        steps += [(done + off + u * nblk, nblk, False) for u in range(size // nblk)]
                        off += size
                    size //= 2
                ksteps(steps + [(qi, 1, True)])

        for hd in range(nh):
            acc = acc_scr[hd]
            inv_l = 1.0 / acc[DIFF_DV:DIFF_DV + 1, :]
            o_t = (acc[0:DIFF_DV, 0:tq] * inv_l[:, 0:tq]
                   - lam * (acc[0:DIFF_DV, tq:2 * tq] * inv_l[:, tq:2 * tq]))
            ms = jnp.mean(o_t * o_t, axis=0, keepdims=True)
            o_t = o_t * lax.rsqrt(ms + EPS)
            o_ref[qs, hd * DIFF_DV:(hd + 1) * DIFF_DV] = (
                o_t.T * gsub * (1.0 - lam_init)).astype(BF16)
        return carry

    lax.fori_loop(0, nq, qblock, 0)


def _attn_prompt(slopes, lam_p, qdt, kdb, vdt, g_sub, *, tq, lam_init):
    nb, nh, t, _ = kdb.shape
    per_stream = lambda b, *_: (b, 0, 0, 0)
    grid_spec = pltpu.PrefetchScalarGridSpec(
        num_scalar_prefetch=1,
        grid=(nb,),
        in_specs=[pl.BlockSpec((4, DIFF_DH), lambda b, *_: (0, 0)),
                  pl.BlockSpec((1, nh, LANES, t), per_stream),
                  pl.BlockSpec((1, nh, t, LANES), per_stream),
                  pl.BlockSpec((1, nh, V_AUG_ROWS, t), per_stream),
                  pl.BlockSpec((1, DIFF_DV), lambda b, *_: (0, 0))],
        out_specs=pl.BlockSpec((t, nh * DIFF_DV), lambda b, *_: (b, 0)),
        scratch_shapes=[pltpu.VMEM((t, LANES), BF16),
                        pltpu.VMEM((nh, 2 * LANES, 2 * tq), BF16),
                        pltpu.VMEM((nh, tq, tq), F32),
                        pltpu.VMEM((nh, V_AUG_ROWS, 2 * tq), F32),
                        pltpu.VMEM((8, 2 * tq), F32),
                        pltpu.VMEM((S_SLOTS, K_WIDE * tq, 2 * tq), F32)],
    )
    return pl.pallas_call(
        functools.partial(_attn_prompt_kernel, tq=tq, lam_init=lam_init),
        grid_spec=grid_spec,
        out_shape=jax.ShapeDtypeStruct((nb * t, nh * DIFF_DV), BF16),
        compiler_params=pltpu.CompilerParams(
            dimension_semantics=("arbitrary",), vmem_limit_bytes=VMEM_LIMIT_BYTES),
        name="attn_prompt",
    )(slopes, lam_p, qdt, kdb, vdt, g_sub)


def _attn_sample_kernel(slopes_ref, lam_ref, q_ref, ck_ref, cv_ref, kn_ref, vn_ref, gsub_ref, o_ref,
                        *, tq, past, lam_init):
    nh = DIFF_HEADS
    nt = (((1,), (1,)), ((), ()))
    k_pos = lax.broadcasted_iota(jnp.int32, (1, past), 1).astype(F32)
    qi = lax.broadcasted_iota(jnp.int32, (tq, tq), 0)
    kj = lax.broadcasted_iota(jnp.int32, (tq, tq), 1)
    rel_new = (past + qi - jnp.abs(qi - kj)).astype(F32)
    lam = _lambda(lam_ref, lam_init)
    gsub = gsub_ref[...]
    for hd in range(nh):
        slope = slopes_ref[hd]
        old = pl.ds(hd, past, stride=nh)
        new = pl.ds(hd, tq, stride=nh)
        qq = _split_halves(q_ref[0, hd])
        s_c = lax.dot_general(qq, ck_ref[0, old, :].astype(BF16), nt, preferred_element_type=F32)
        s_c = s_c + slope * k_pos
        s_n = lax.dot_general(qq, kn_ref[0, new, :].astype(BF16), nt, preferred_element_type=F32)
        bias_n = slope * rel_new
        s_n = s_n + jnp.concatenate([bias_n, bias_n], axis=0)
        m = jnp.maximum(jnp.max(s_c, axis=-1, keepdims=True), jnp.max(s_n, axis=-1, keepdims=True))
        p_c = jnp.exp(s_c - m)
        p_n = jnp.exp(s_n - m)
        l = jnp.sum(p_c, axis=-1, keepdims=True) + jnp.sum(p_n, axis=-1, keepdims=True)
        acc = (jnp.dot(p_c.astype(BF16), cv_ref[0, old, :].astype(BF16), preferred_element_type=F32)
               + jnp.dot(p_n.astype(BF16), vn_ref[0, new, :].astype(BF16),
                         preferred_element_type=F32))
        o_ref[:, hd * DIFF_DV:(hd + 1) * DIFF_DV] = _finish_attention(
            acc, l, lam, gsub, lam_init, tq).astype(BF16)


def _attn_sample(slopes, lam_p, qd, cache_k, cache_v, k_new, v_new, g_sub, *, lam_init):
    nb, rows, _ = cache_k.shape
    nh = DIFF_HEADS
    past = rows // nh
    tq = k_new.shape[1] // (nb * nh)
    cache_spec = pl.BlockSpec((1, past * nh, LANES), lambda b, *_: (b, 0, 0))
    new_spec = pl.BlockSpec((1, tq * nh, LANES), lambda b, *_: (0, b, 0))
    grid_spec = pltpu.PrefetchScalarGridSpec(
        num_scalar_prefetch=1,
        grid=(nb,),
        in_specs=[pl.BlockSpec((4, DIFF_DH), lambda b, *_: (0, 0)),
                  pl.BlockSpec((1, nh, tq, LANES), lambda b, *_: (0, 0, b, 0)),
                  cache_spec, cache_spec, new_spec, new_spec,
                  pl.BlockSpec((1, DIFF_DV), lambda b, *_: (0, 0))],
        out_specs=pl.BlockSpec((tq, nh * DIFF_DV), lambda b, *_: (b, 0)),
    )
    return pl.pallas_call(
        functools.partial(_attn_sample_kernel, tq=tq, past=past, lam_init=lam_init),
        grid_spec=grid_spec,
        out_shape=jax.ShapeDtypeStruct((nb * tq, nh * DIFF_DV), BF16),
        compiler_params=pltpu.CompilerParams(
            dimension_semantics=("arbitrary",), vmem_limit_bytes=VMEM_LIMIT_BYTES),
        name="attn_sample",
    )(slopes, lam_p, qd, cache_k, cache_v, k_new, v_new, g_sub)


def _out_ffn_kernel(og_ref, od_ref, x_ref, wo_ref, wu_ref, wd_ref,
                    gpm_ref, gpf_ref, gqf_ref, y_ref, *, ff_chunk):
    mix = (jnp.dot(og_ref[...], wo_ref[:W_V, :], preferred_element_type=F32)
           + jnp.dot(od_ref[...], wo_ref[W_V:, :], preferred_element_type=F32))
    x1 = x_ref[...] + _rms(mix, gpm_ref[...])
    f = _rms(x1, gpf_ref[...]).astype(BF16)
    d_ff = wu_ref.shape[1]
    acc = jnp.zeros(x1.shape, F32)
    for c in range(d_ff // ff_chunk):
        sl = slice(c * ff_chunk, (c + 1) * ff_chunk)
        hid = jnp.dot(f, wu_ref[:, sl], preferred_element_type=F32)
        hid = jnp.square(jnp.maximum(hid, 0.0)).astype(BF16)
        acc = acc + jnp.dot(hid, wd_ref[sl, :], preferred_element_type=F32)
    y_ref[...] = x1 + _rms(acc, gqf_ref[...])


def _out_ffn(og, od, x2d, w_out, w_up, w_down, g_post_mix, g_pre_ffn, g_post_ffn, *, tm):
    n, d = x2d.shape
    row = lambda i: (i, 0)
    return pl.pallas_call(
        functools.partial(_out_ffn_kernel, ff_chunk=1024),
        grid=(n // tm,),
        in_specs=[pl.BlockSpec((tm, W_V), row), pl.BlockSpec((tm, W_V), row),
                  pl.BlockSpec((tm, d), row),
                  _const_spec(w_out.shape), _const_spec(w_up.shape), _const_spec(w_down.shape),
                  _const_spec((1, d)), _const_spec((1, d)), _const_spec((1, d))],
        out_specs=pl.BlockSpec((tm, d), row),
        out_shape=jax.ShapeDtypeStruct((n, d), F32),
        compiler_params=pltpu.CompilerParams(
            dimension_semantics=("arbitrary",), vmem_limit_bytes=VMEM_LIMIT_BYTES),
        name="out_ffn",
    )(og, od, x2d, w_out, w_up, w_down, g_post_mix, g_pre_ffn, g_post_ffn)


def _split_w_in(w):
    a0 = 2 * W_G + 2 * W_V
    a1 = a0 + GLA_RANK
    w_a = jnp.pad(w[:, a0:a1], ((0, 0), (0, LANES - GLA_RANK)))
    return w[:, :a0].astype(BF16), w[:, a1:].astype(BF16), w_a.astype(BF16)


def kernel(x_prompt, x_sample, cache_k, cache_v, state_gla, w_in, w_gate_up, b_gate, g_gla_out,
           lam_q1, lam_k1, lam_q2, lam_k2, g_subln, w_out, g_pre_mix, g_post_mix,
           g_pre_ffn, g_post_ffn, w_ff_up, w_ff_down):
    depth = w_in.shape[0]
    nb_p, t_p, d = x_prompt.shape
    nb_s, t_s, _ = x_sample.shape
    past = cache_k.shape[2]
    slopes = jnp.exp2(-8.0 / DIFF_HEADS * jnp.arange(1, DIFF_HEADS + 1, dtype=F32))
    yp = x_prompt.reshape(nb_p * t_p, d)
    ys = x_sample.reshape(nb_s * t_s, d)
    outs = [[] for _ in range(6)]
    for l in range(depth):
        lam_init = 0.8 - 0.6 * math.exp(-0.3 * l)
        w_parts = _split_w_in(w_in[l])
        wg_pad = jnp.concatenate(
            [w_gate_up[l], jnp.zeros((LANES - GLA_RANK, W_G), F32)], axis=0).astype(BF16)
        bg = b_gate[l][None, :]
        lam_p = jnp.stack([lam_q1[l], lam_k1[l], lam_q2[l], lam_k2[l]], axis=0)
        g_out = g_gla_out[l][None, :]
        g_sub = g_subln[l][None, :]
        wo = w_out[l].astype(BF16)
        wu = w_ff_up[l].astype(BF16)
        wd = w_ff_down[l].astype(BF16)
        gains = (g_post_mix[l][None, :], g_pre_ffn[l][None, :], g_post_ffn[l][None, :])
        g_pre = g_pre_mix[l][None, :]

        qg, kg, vg, rg, gate, qd, kd, kdb, vd, vdb = _in_proj(
            yp, g_pre, w_parts, wg_pad, bg, nb=nb_p, t=t_p, tm=1024, transposed=True)
        s0 = jnp.zeros((nb_p, GLA_HEADS, GLA_DK, GLA_DV), F32)
        og, s_p = _gla(qg, kg, vg, rg, gate, s0, g_out, nb=nb_p, t=t_p, L=CHUNK, CB=8,
                       carry_state=True)
        od = _attn_prompt(slopes, lam_p, qd, kdb, vdb, g_sub, tq=256, lam_init=lam_init)
        yp = _out_ffn(og, od, yp, wo, wu, wd, *gains, tm=1024)
        outs[0].append(kd.reshape(nb_p, t_p, DIFF_HEADS, 2 * DIFF_DH))
        outs[1].append(vd.reshape(nb_p, t_p, DIFF_HEADS, DIFF_DV))
        outs[2].append(s_p)

        n_s = nb_s * t_s
        qg, kg, vg, rg, gate, qd, kd, _, vd, _ = _in_proj(
            ys, g_pre, w_parts, wg_pad, bg, nb=1, t=n_s, tm=n_s, transposed=False)
        og, s_s = _gla(qg, kg, vg, rg, gate, state_gla[l], g_out, nb=nb_s, t=t_s, L=t_s,
                       CB=math.gcd(nb_s, 8), carry_state=False)
        od = _attn_sample(slopes, lam_p, qd,
                          cache_k[l].reshape(nb_s, past * DIFF_HEADS, 2 * DIFF_DH),
                          cache_v[l].reshape(nb_s, past * DIFF_HEADS, DIFF_DV),
                          kd, vd, g_sub, lam_init=lam_init)
        ys = _out_ffn(og, od, ys, wo, wu, wd, *gains, tm=n_s)
        outs[3].append(kd.reshape(nb_s, t_s, DIFF_HEADS, 2 * DIFF_DH))
        outs[4].append(vd.reshape(nb_s, t_s, DIFF_HEADS, DIFF_DV))
        outs[5].append(s_s)

    stack = lambda xs: jnp.stack(xs, axis=0)
    return (yp.reshape(nb_p, t_p, d), ys.reshape(nb_s, t_s, d),
            stack(outs[0]), stack(outs[1]), stack(outs[2]),
            stack(outs[3]), stack(outs[4]), stack(outs[5]))
```

```python
import functools
import math

import jax
import jax.numpy as jnp
from jax import lax
from jax.experimental import pallas as pl
from jax.experimental.pallas import tpu as pltpu

F32 = jnp.float32
BF16 = jnp.bfloat16

EPS = 1e-6
CHUNK = 64
GLA_HEADS = 4
GLA_DK = 64
GLA_DV = 128
GLA_RANK = 16
GLA_TAU = 16.0
DIFF_HEADS = 4
DIFF_DH = 64
DIFF_DV = 128

W_G = GLA_HEADS * GLA_DK
W_V = GLA_HEADS * GLA_DV
W_D = DIFF_HEADS * 2 * DIFF_DH
LANES = 128
OFF_QG = 0
OFF_KG = OFF_QG + W_G
OFF_VG = OFF_KG + W_G
OFF_RG = OFF_VG + W_V
OFF_QD = 0
OFF_KD = OFF_QD + W_D
OFF_VD = OFF_KD + W_D

BF16_SUBLANES = 16
V_AUG_ROWS = DIFF_DV + BF16_SUBLANES
LOG2E = math.log2(math.e)
K_WIDE = 4
K_UNROLL = 1
SCORE_AHEAD = 2
S_SLOTS = SCORE_AHEAD + 1

VMEM_LIMIT_BYTES = 56 * 1024 * 1024
IN_PROJ_ROWS = 1024
OUT_FFN_ROWS = 1024
FF_CHUNK = 1024
GLA_CHUNKS_PER_TILE = 8
ATTN_BLOCK = 256
GLA_SAFE_LOG_DECAY = -60.0


def _rms(x, g):
    ms = jnp.mean(x * x, axis=-1, keepdims=True)
    return x * lax.rsqrt(ms + EPS) * g


def _log_sigmoid(x):
    return jnp.minimum(x, 0.0) - jnp.log(1.0 + jnp.exp(-jnp.abs(x)))


def _div_pow2(x, d):
    assert d & (d - 1) == 0
    return lax.shift_right_arithmetic(x, d.bit_length() - 1)


def _const_spec(shape):
    zeros = (0,) * len(shape)
    return pl.BlockSpec(shape, lambda *_: zeros, pipeline_mode=pl.Buffered(1))


def _in_proj_kernel(x_ref, g_ref, w_gla_ref, w_diff_ref, w_a_ref, wg_ref, bg_ref,
                    qg_ref, kg_ref, vg_ref, rg_ref, gate_ref,
                    qd_ref, kd_ref, kdb_ref, vd_ref, vdb_ref, *, transposed):
    h = _rms(x_ref[...], g_ref[...]).astype(BF16)

    def proj(w_ref, off, width):
        return jnp.dot(h, w_ref[:, off:off + width], preferred_element_type=F32)

    qg_ref[...] = (proj(w_gla_ref, OFF_QG, W_G) * (GLA_DK ** -0.5)).astype(BF16)
    kg_ref[...] = proj(w_gla_ref, OFF_KG, W_G).astype(BF16)
    vg_ref[...] = proj(w_gla_ref, OFF_VG, W_V).astype(BF16)
    rg_ref[...] = proj(w_gla_ref, OFF_RG, W_V).astype(BF16)
    qd = proj(w_diff_ref, OFF_QD, W_D) * (DIFF_DH ** -0.5)
    kd = proj(w_diff_ref, OFF_KD, W_D)
    vd = proj(w_diff_ref, OFF_VD, W_D)
    tm = qd.shape[0]
    for hd in range(DIFF_HEADS):
        head_rows = pl.ds(hd, tm, stride=DIFF_HEADS)
        kd_ref[0, head_rows, :] = kd[:, hd * LANES:(hd + 1) * LANES]
        vd_ref[0, head_rows, :] = vd[:, hd * LANES:(hd + 1) * LANES]
    ones_rows = (lax.broadcasted_iota(jnp.int32, (V_AUG_ROWS - DIFF_DV, tm), 0) == 0).astype(BF16)
    for hd in range(DIFF_HEADS):
        sl = slice(hd * LANES, (hd + 1) * LANES)
        kdb_ref[0, hd] = kd[:, sl].astype(BF16)
        if transposed:
            qd_ref[0, hd] = (qd[:, sl] * LOG2E).T.astype(BF16)
            vdb_ref[0, hd, 0:DIFF_DV, :] = vd[:, sl].T.astype(BF16)
            vdb_ref[0, hd, DIFF_DV:V_AUG_ROWS, :] = ones_rows
        else:
            qd_ref[0, hd] = qd[:, sl].astype(BF16)
            vdb_ref[0, hd] = vd[:, sl].astype(BF16)
    ag = proj(w_a_ref, 0, LANES).astype(BF16)
    pre = jnp.dot(ag, wg_ref[...], preferred_element_type=F32) + bg_ref[...]
    gate_ref[...] = _log_sigmoid(pre) * (1.0 / GLA_TAU)


def _in_proj(x2d, g_pre, w_parts, wg_pad, b_gate, *, nb, t, tm, transposed):
    n, d = x2d.shape
    assert t % tm == 0 and n == nb * t
    steps_per_b = t // tm
    row = lambda i: (i, 0)
    hm = lambda i: (i // steps_per_b, 0, i % steps_per_b, 0)
    hm_t = lambda i: (i // steps_per_b, 0, 0, i % steps_per_b)
    hm_shape = jax.ShapeDtypeStruct((nb, DIFF_HEADS, t, LANES), BF16)
    hm_spec = pl.BlockSpec((1, DIFF_HEADS, tm, LANES), hm)
    kv_shape = jax.ShapeDtypeStruct((nb, t * DIFF_HEADS, LANES), F32)
    kv_spec = pl.BlockSpec((1, tm * DIFF_HEADS, LANES),
                           lambda i: (i // steps_per_b, i % steps_per_b, 0))
    if transposed:
        q_shape = jax.ShapeDtypeStruct((nb, DIFF_HEADS, LANES, t), BF16)
        q_spec = pl.BlockSpec((1, DIFF_HEADS, LANES, tm), hm_t)
        v_shape = jax.ShapeDtypeStruct((nb, DIFF_HEADS, V_AUG_ROWS, t), BF16)
        v_spec = pl.BlockSpec((1, DIFF_HEADS, V_AUG_ROWS, tm), hm_t)
    else:
        q_shape, q_spec, v_shape, v_spec = hm_shape, hm_spec, hm_shape, hm_spec
    out_shape = (
        jax.ShapeDtypeStruct((n, W_G), BF16), jax.ShapeDtypeStruct((n, W_G), BF16),
        jax.ShapeDtypeStruct((n, W_V), BF16), jax.ShapeDtypeStruct((n, W_V), BF16),
        jax.ShapeDtypeStruct((n, W_G), F32),
        q_shape,
        kv_shape, hm_shape,
        kv_shape, v_shape,
    )
    out_specs = (
        pl.BlockSpec((tm, W_G), row), pl.BlockSpec((tm, W_G), row),
        pl.BlockSpec((tm, W_V), row), pl.BlockSpec((tm, W_V), row),
        pl.BlockSpec((tm, W_G), row),
        q_spec,
        kv_spec, hm_spec,
        kv_spec, v_spec,
    )
    return pl.pallas_call(
        functools.partial(_in_proj_kernel, transposed=transposed),
        grid=(n // tm,),
        in_specs=[pl.BlockSpec((tm, d), row), _const_spec((1, d)),
                  *[_const_spec(w.shape) for w in w_parts],
                  _const_spec(wg_pad.shape), _const_spec((1, W_G))],
        out_specs=out_specs,
        out_shape=out_shape,
        compiler_params=pltpu.CompilerParams(
            dimension_semantics=("arbitrary",), vmem_limit_bytes=VMEM_LIMIT_BYTES),
        name="in_proj",
    )(x2d, g_pre, *w_parts, wg_pad, b_gate)


def _gla_kernel(q_ref, k_ref, v_ref, r_ref, gate_ref, s0_ref, gout_ref,
                o_ref, sout_ref, s_scr, oi_scr, qf_scr, kf_scr, bf_scr, b_scr,
                *, L, CB, carry_state):
    H, DK, DV = GLA_HEADS, GLA_DK, GLA_DV
    HL = H * L
    step = pl.program_id(1)

    if carry_state:
        @pl.when(step == 0)
        def _():
            s_scr[...] = s0_ref[0].reshape(H * DK, DV).T

    lane_head = _div_pow2(lax.broadcasted_iota(jnp.int32, (L, W_G), 1), DK)
    row_i = lax.broadcasted_iota(jnp.int32, (HL, HL), 0)
    col_i = lax.broadcasted_iota(jnp.int32, (HL, HL), 1)
    causal = (col_i <= row_i) & (col_i >= (row_i & ~(L - 1)))
    row_t = lax.broadcasted_iota(jnp.int32, (L, 1), 0)
    nt = (((1,), (1,)), ((), ()))
    tn = (((0,), (0,)), ((), ()))
    gout = gout_ref[...]

    def split_hi_lo(x):
        hi = x.astype(BF16)
        return hi, (x - hi.astype(F32)).astype(BF16)

    tri = jnp.where(lax.broadcasted_iota(jnp.int32, (L, L), 1)
                    <= lax.broadcasted_iota(jnp.int32, (L, L), 0), 1.0, 0.0).astype(BF16)
    totals = []
    for c in range(CB):
        c_hi, c_lo = split_hi_lo(gate_ref[c * L:(c + 1) * L, :])
        b_c = (jnp.dot(tri, c_hi, preferred_element_type=F32)
               + jnp.dot(tri, c_lo, preferred_element_type=F32))
        b_scr[c * L:(c + 1) * L, :] = b_c
        totals.append(b_c[L - 1:L, :])
    safe = jnp.min(jnp.concatenate(totals, axis=0)) >= GLA_SAFE_LOG_DECAY

    def stack_heads(a):
        return jnp.concatenate(
            [jnp.where(lane_head == hd, a, 0.0) for hd in range(H)], axis=0)

    def intra_factorised(rows, q, k, b, qs, vs):
        ks = stack_heads(k * jnp.exp(-b)).astype(BF16)
        a = lax.dot_general(qs, ks, nt, preferred_element_type=F32)
        a = jnp.where(causal, a, 0.0).astype(BF16)
        return jnp.dot(a, vs, preferred_element_type=F32)

    def intra_per_token(rows, q, k, b, qs, vs):
        qf_scr[...] = q
        kf_scr[...] = k
        bf_scr[...] = b

        def tok(t, carry_t):
            w = (qf_scr[pl.ds(t, 1), :] * kf_scr[...]
                 * jnp.exp(jnp.minimum(bf_scr[pl.ds(t, 1), :] - bf_scr[...], 0.0)))
            for hd in range(H):
                a_col = jnp.sum(jnp.where(lane_head == hd, w, 0.0), axis=-1, keepdims=True)
                a_col = jnp.where(row_t <= t, a_col, 0.0)
                vh = v_ref[rows, hd * DV:(hd + 1) * DV].astype(F32)
                oi_scr[pl.ds(hd * L + t, 1), :] = jnp.sum(a_col * vh, axis=0, keepdims=True)
            return carry_t

        lax.fori_loop(0, L, tok, 0)
        return oi_scr[...]

    def chunk(rows, s_old, intra):
        b = b_scr[rows, :]
        b_last = b[L - 1:L, :]
        q = q_ref[rows, :].astype(F32)
        k = k_ref[rows, :].astype(F32)
        v = v_ref[rows, :]
        qs = stack_heads(q * jnp.exp(b)).astype(BF16)
        kends = stack_heads(k * jnp.exp(b_last - b)).astype(BF16)
        vs = jnp.concatenate([v[:, hd * DV:(hd + 1) * DV] for hd in range(H)], axis=0)
        o = (lax.dot_general(qs, s_old.astype(BF16), nt, preferred_element_type=F32)
             + intra(rows, q, k, b, qs, vs))
        o = _rms(o, gout)
        r = r_ref[rows, :].astype(F32)
        for hd in range(H):
            rh = r[:, hd * DV:(hd + 1) * DV]
            o_ref[rows, hd * DV:(hd + 1) * DV] = (
                o[hd * L:(hd + 1) * L, :] * (rh * jax.nn.sigmoid(rh))).astype(BF16)

        kv_t = lax.dot_general(vs, kends, tn, preferred_element_type=F32)
        return jnp.exp(b_last) * s_old + kv_t

    def state_in(c):
        return s_scr[...] if carry_state else s0_ref[c].reshape(H * DK, DV).T

    def state_out(c, s):
        if carry_state:
            s_scr[...] = s
        else:
            sout_ref[c] = s.T.reshape(H, DK, DV)

    @pl.when(safe)
    def _():
        s = state_in(0)
        for c in range(CB):
            if not carry_state and c > 0:
                s = state_in(c)
            s = chunk(slice(c * L, (c + 1) * L), s, intra_factorised)
            if not carry_state or c == CB - 1:
                state_out(c, s)

    @pl.when(jnp.logical_not(safe))
    def _():
        def body(c, carry):
            rows = pl.ds(pl.multiple_of(c * L, L), L)
            state_out(c, chunk(rows, state_in(c), intra_per_token))
            return carry

        lax.fori_loop(0, CB, body, 0)

    if carry_state:
        @pl.when(step == pl.num_programs(1) - 1)
        def _():
            sout_ref[0] = s_scr[...].T.reshape(H, DK, DV)


def _gla(qg, kg, vg, rg, gate, s0, g_out, *, nb, t, L, CB, carry_state):
    n = qg.shape[0]
    tm = L * CB
    if carry_state:
        assert t % tm == 0
        grid = (nb, t // tm)
        steps = t // tm
        row = lambda b, s: (b * steps + s, 0)
        st = lambda b, s: (b, 0, 0, 0)
        state_block = (1, GLA_HEADS, GLA_DK, GLA_DV)
    else:
        assert t == L and nb % CB == 0
        grid = (nb // CB, 1)
        row = lambda b, s: (b, 0)
        st = lambda b, s: (b, 0, 0, 0)
        state_block = (CB, GLA_HEADS, GLA_DK, GLA_DV)
    return pl.pallas_call(
        functools.partial(_gla_kernel, L=L, CB=CB, carry_state=carry_state),
        grid=grid,
        in_specs=[pl.BlockSpec((tm, W_G), row), pl.BlockSpec((tm, W_G), row),
                  pl.BlockSpec((tm, W_V), row), pl.BlockSpec((tm, W_V), row),
                  pl.BlockSpec((tm, W_G), row), pl.BlockSpec(state_block, st),
                  pl.BlockSpec((1, GLA_DV), lambda b, s: (0, 0))],
        out_specs=(pl.BlockSpec((tm, W_V), row), pl.BlockSpec(state_block, st)),
        out_shape=(jax.ShapeDtypeStruct((n, W_V), BF16),
                   jax.ShapeDtypeStruct((nb, GLA_HEADS, GLA_DK, GLA_DV), F32)),
        scratch_shapes=[pltpu.VMEM((GLA_DV, GLA_HEADS * GLA_DK), F32),
                        pltpu.VMEM((GLA_HEADS * L, GLA_DV), F32),
                        pltpu.VMEM((L, W_G), F32), pltpu.VMEM((L, W_G), F32),
                        pltpu.VMEM((L, W_G), F32), pltpu.VMEM((tm, W_G), F32)],
        compiler_params=pltpu.CompilerParams(
            dimension_semantics=("arbitrary", "arbitrary"), vmem_limit_bytes=VMEM_LIMIT_BYTES),
        name="gla",
    )(qg, kg, vg, rg, gate, s0, g_out)


def _lambda(lam_ref, lam_init):
    lp = lam_ref[...]
    s1 = jnp.sum(lp[0:1, :] * lp[1:2, :], axis=-1, keepdims=True)
    s2 = jnp.sum(lp[2:3, :] * lp[3:4, :], axis=-1, keepdims=True)
    return jnp.exp(s1) - jnp.exp(s2) + lam_init


def _split_halves(q):
    lane = lax.broadcasted_iota(jnp.int32, q.shape, 1)
    zero = jnp.zeros_like(q)
    return jnp.concatenate(
        [jnp.where(lane < DIFF_DH, q, zero), jnp.where(lane >= DIFF_DH, q, zero)], axis=0)


def _finish_attention(acc, l, lam, gsub, lam_init, tq):
    o = acc[:tq] / l[:tq] - lam * (acc[tq:] / l[tq:])
    return _rms(o, gsub) * (1.0 - lam_init)


def _split3_bf16(x):
    hi = x.astype(BF16).astype(F32)
    r = x - hi
    mid = r.astype(BF16).astype(F32)
    lo = (r - mid).astype(BF16).astype(F32)
    return hi, mid, lo


def _attn_prompt_kernel(slopes_ref, lam_ref, qt_ref, k_ref, vt_ref, gsub_ref, o_ref,
                        aug_scr, qt_scr, corr_scr, acc_scr, m_scr, s_scr, *, tq, lam_init):
    nh = DIFF_HEADS
    tk = tq
    t = k_ref.shape[2]
    nq = t // tq
    slope2 = [slopes_ref[hd] * LOG2E for hd in range(nh)]
    j_lane = 3 * nh

    lane = lax.broadcasted_iota(jnp.int32, (tk, LANES), 1)
    k_loc = lax.broadcasted_iota(jnp.int32, (tk, LANES), 0).astype(F32)
    base = jnp.zeros((tk, LANES), F32)
    for hd in range(nh):
        for i, part in enumerate(_split3_bf16(slope2[hd] * k_loc)):
            base = jnp.where(lane == 3 * hd + i, part, base)
    blk_lanes = (lane >= j_lane) & (lane < j_lane + 3)
    for j in range(t // tk):
        aug_scr[j * tk:(j + 1) * tk, :] = jnp.where(blk_lanes, float(j), base).astype(BF16)

    row = lax.broadcasted_iota(jnp.int32, (LANES, 2 * tq), 0)
    kl = lax.broadcasted_iota(jnp.int32, (tk, tq), 0)
    ql = lax.broadcasted_iota(jnp.int32, (tk, tq), 1)
    same_chunk_or_earlier = _div_pow2(kl, CHUNK) <= _div_pow2(ql, CHUNK)
    ahead = jnp.maximum(kl - ql, 0).astype(F32)
    for hd in range(nh):
        rows = jnp.where((row >= 3 * hd) & (row < 3 * hd + 3), 1.0, 0.0)
        for i, part in enumerate(_split3_bf16(jnp.full((LANES, 2 * tq), slope2[hd] * tk, F32))):
            rows = jnp.where(row == j_lane + i, part, rows)
        qt_scr[hd, LANES:2 * LANES, :] = rows.astype(BF16)
        qt_scr[hd, DIFF_DH:2 * DIFF_DH, 0:tq] = jnp.zeros((DIFF_DH, tq), BF16)
        qt_scr[hd, 0:DIFF_DH, tq:2 * tq] = jnp.zeros((DIFF_DH, tq), BF16)
        corr_scr[hd] = jnp.where(same_chunk_or_earlier, (-2.0 * slope2[hd]) * ahead, -jnp.inf)

    lam = _lambda(lam_ref, lam_init)
    gsub = gsub_ref[...]

    def ksteps(steps):
        items = [(j0, nblk, diag, hd) for j0, nblk, diag in steps for hd in range(nh)]

        def key_rows(j0, nblk):
            return pl.ds(pl.multiple_of(j0 * tk, tk), nblk * tk)

        def scores(i):
            j0, nblk, diag, hd = items[i]
            ks = key_rows(j0, nblk)
            k_aug = jnp.concatenate([k_ref[0, hd, ks, :], aug_scr[ks, :]], axis=1)
            s_scr[i % S_SLOTS, 0:nblk * tk, :] = jnp.dot(
                k_aug, qt_scr[hd], preferred_element_type=F32)

        def softmax(i):
            j0, nblk, diag, hd = items[i]
            p_parts, alpha_parts = [], []
            for g in range(2 * tq // LANES):
                lanes = slice(g * LANES, (g + 1) * LANES)
                sg = s_scr[i % S_SLOTS, 0:nblk * tk, lanes]
                if diag:
                    c0 = (g * LANES) % tq
                    sg = sg + corr_scr[hd, :, c0:c0 + LANES]
                m_old = m_scr[hd:hd + 1, lanes]
                m_new = jnp.maximum(m_old, jnp.max(sg, axis=0, keepdims=True))
                m_scr[hd:hd + 1, lanes] = m_new
                p_parts.append(jnp.exp2(sg - m_new).astype(BF16))
                alpha_parts.append(jnp.exp2(m_old - m_new))
            return jnp.concatenate(p_parts, axis=1), jnp.concatenate(alpha_parts, axis=1)

        def accumulate(i, p, alpha):
            j0, nblk, diag, hd = items[i]
            acc_scr[hd] = alpha * acc_scr[hd] + jnp.dot(
                vt_ref[0, hd, :, key_rows(j0, nblk)], p, preferred_element_type=F32)

        n = len(items)
        for i in range(min(SCORE_AHEAD, n)):
            scores(i)
        pending = None
        for i in range(n):
            if i + SCORE_AHEAD < n:
                scores(i + SCORE_AHEAD)
            p_alpha = softmax(i)
            if pending is not None:
                accumulate(i - 1, *pending)
            pending = p_alpha
        accumulate(n - 1, *pending)

    def qblock(qi, carry):
        qs = pl.ds(pl.multiple_of(qi * tq, tq), tq)
        for hd in range(nh):
            qt = qt_ref[0, hd, :, qs]
            qt_scr[hd, 0:DIFF_DH, 0:tq] = qt[0:DIFF_DH, :]
            qt_scr[hd, DIFF_DH:2 * DIFF_DH, tq:2 * tq] = qt[DIFF_DH:2 * DIFF_DH, :]
        m_scr[...] = jnp.full(m_scr.shape, -jnp.inf, F32)
        acc_scr[...] = jnp.zeros(acc_scr.shape, F32)

        per_trip = K_UNROLL * K_WIDE

        def off_diag(jj, c):
            ksteps([(jj * per_trip + u * K_WIDE, K_WIDE, False) for u in range(K_UNROLL)])
            return c

        lax.fori_loop(0, qi // per_trip, off_diag, 0)
        done = (qi // per_trip) * per_trip
        for left in range(per_trip):
            @pl.when(qi - done == left)
            def _(left=left):
                steps, off, size = [], 0, per_trip // 2
                while size >= 1:
                    if left & size:
                        nblk = min(size, K_WIDE)
                        steps += [(done + off + u * nblk, nblk, False) for u in range(size // nblk)]
                        off += size
                    size //= 2
                ksteps(steps + [(qi, 1, True)])

        for hd in range(nh):
            acc = acc_scr[hd]
            inv_l = 1.0 / acc[DIFF_DV:DIFF_DV + 1, :]
            o_t = (acc[0:DIFF_DV, 0:tq] * inv_l[:, 0:tq]
                   - lam * (acc[0:DIFF_DV, tq:2 * tq] * inv_l[:, tq:2 * tq]))
            ms = jnp.mean(o_t * o_t, axis=0, keepdims=True)
            o_t = o_t * lax.rsqrt(ms + EPS)
            o_ref[qs, hd * DIFF_DV:(hd + 1) * DIFF_DV] = (
                o_t.T * gsub * (1.0 - lam_init)).astype(BF16)
        return carry

    lax.fori_loop(0, nq, qblock, 0)


def _attn_prompt(slopes, lam_p, qdt, kdb, vdt, g_sub, *, tq, lam_init):
    nb, nh, t, _ = kdb.shape
    assert t % tq == 0 and tq % CHUNK == 0 and nh == DIFF_HEADS
    per_stream = lambda b, *_: (b, 0, 0, 0)
    grid_spec = pltpu.PrefetchScalarGridSpec(
        num_scalar_prefetch=1,
        grid=(nb,),
        in_specs=[pl.BlockSpec((4, DIFF_DH), lambda b, *_: (0, 0)),
                  pl.BlockSpec((1, nh, LANES, t), per_stream),
                  pl.BlockSpec((1, nh, t, LANES), per_stream),
                  pl.BlockSpec((1, nh, V_AUG_ROWS, t), per_stream),
                  pl.BlockSpec((1, DIFF_DV), lambda b, *_: (0, 0))],
        out_specs=pl.BlockSpec((t, nh * DIFF_DV), lambda b, *_: (b, 0)),
        scratch_shapes=[pltpu.VMEM((t, LANES), BF16),
                        pltpu.VMEM((nh, 2 * LANES, 2 * tq), BF16),
                        pltpu.VMEM((nh, tq, tq), F32),
                        pltpu.VMEM((nh, V_AUG_ROWS, 2 * tq), F32),
                        pltpu.VMEM((8, 2 * tq), F32),
                        pltpu.VMEM((S_SLOTS, K_WIDE * tq, 2 * tq), F32)],
    )
    return pl.pallas_call(
        functools.partial(_attn_prompt_kernel, tq=tq, lam_init=lam_init),
        grid_spec=grid_spec,
        out_shape=jax.ShapeDtypeStruct((nb * t, nh * DIFF_DV), BF16),
        compiler_params=pltpu.CompilerParams(
            dimension_semantics=("arbitrary",), vmem_limit_bytes=VMEM_LIMIT_BYTES),
        name="attn_prompt",
    )(slopes, lam_p, qdt, kdb, vdt, g_sub)


def _attn_sample_kernel(slopes_ref, lam_ref, q_ref, ck_ref, cv_ref, kn_ref, vn_ref, gsub_ref, o_ref,
                        *, tq, past, lam_init):
    nh = DIFF_HEADS
    nt = (((1,), (1,)), ((), ()))
    k_pos = lax.broadcasted_iota(jnp.int32, (1, past), 1).astype(F32)
    qi = lax.broadcasted_iota(jnp.int32, (tq, tq), 0)
    kj = lax.broadcasted_iota(jnp.int32, (tq, tq), 1)
    rel_new = (past + qi - jnp.abs(qi - kj)).astype(F32)
    lam = _lambda(lam_ref, lam_init)
    gsub = gsub_ref[...]
    for hd in range(nh):
        slope = slopes_ref[hd]
        old = pl.ds(hd, past, stride=nh)
        new = pl.ds(hd, tq, stride=nh)
        qq = _split_halves(q_ref[0, hd])
        s_c = lax.dot_general(qq, ck_ref[0, old, :].astype(BF16), nt, preferred_element_type=F32)
        s_c = s_c + slope * k_pos
        s_n = lax.dot_general(qq, kn_ref[0, new, :].astype(BF16), nt, preferred_element_type=F32)
        bias_n = slope * rel_new
        s_n = s_n + jnp.concatenate([bias_n, bias_n], axis=0)
        m = jnp.maximum(jnp.max(s_c, axis=-1, keepdims=True), jnp.max(s_n, axis=-1, keepdims=True))
        p_c = jnp.exp(s_c - m)
        p_n = jnp.exp(s_n - m)
        l = jnp.sum(p_c, axis=-1, keepdims=True) + jnp.sum(p_n, axis=-1, keepdims=True)
        acc = (jnp.dot(p_c.astype(BF16), cv_ref[0, old, :].astype(BF16), preferred_element_type=F32)
               + jnp.dot(p_n.astype(BF16), vn_ref[0, new, :].astype(BF16),
                         preferred_element_type=F32))
        o_ref[:, hd * DIFF_DV:(hd + 1) * DIFF_DV] = _finish_attention(
            acc, l, lam, gsub, lam_init, tq).astype(BF16)


def _attn_sample(slopes, lam_p, qd, cache_k, cache_v, k_new, v_new, g_sub, *, lam_init):
    nb, rows, _ = cache_k.shape
    nh = DIFF_HEADS
    past = rows // nh
    tq = k_new.shape[1] // (nb * nh)
    cache_spec = pl.BlockSpec((1, past * nh, LANES), lambda b, *_: (b, 0, 0))
    new_spec = pl.BlockSpec((1, tq * nh, LANES), lambda b, *_: (0, b, 0))
    grid_spec = pltpu.PrefetchScalarGridSpec(
        num_scalar_prefetch=1,
        grid=(nb,),
        in_specs=[pl.BlockSpec((4, DIFF_DH), lambda b, *_: (0, 0)),
                  pl.BlockSpec((1, nh, tq, LANES), lambda b, *_: (0, 0, b, 0)),
                  cache_spec, cache_spec, new_spec, new_spec,
                  pl.BlockSpec((1, DIFF_DV), lambda b, *_: (0, 0))],
        out_specs=pl.BlockSpec((tq, nh * DIFF_DV), lambda b, *_: (b, 0)),
    )
    return pl.pallas_call(
        functools.partial(_attn_sample_kernel, tq=tq, past=past, lam_init=lam_init),
        grid_spec=grid_spec,
        out_shape=jax.ShapeDtypeStruct((nb * tq, nh * DIFF_DV), BF16),
        compiler_params=pltpu.CompilerParams(
            dimension_semantics=("arbitrary",), vmem_limit_bytes=VMEM_LIMIT_BYTES),
        name="attn_sample",
    )(slopes, lam_p, qd, cache_k, cache_v, k_new, v_new, g_sub)


def _out_ffn_kernel(og_ref, od_ref, x_ref, wo_ref, wu_ref, wd_ref,
                    gpm_ref, gpf_ref, gqf_ref, y_ref, *, ff_chunk):
    mix = (jnp.dot(og_ref[...], wo_ref[:W_V, :], preferred_element_type=F32)
           + jnp.dot(od_ref[...], wo_ref[W_V:, :], preferred_element_type=F32))
    x1 = x_ref[...] + _rms(mix, gpm_ref[...])
    f = _rms(x1, gpf_ref[...]).astype(BF16)
    d_ff = wu_ref.shape[1]
    acc = jnp.zeros(x1.shape, F32)
    for c in range(d_ff // ff_chunk):
        sl = slice(c * ff_chunk, (c + 1) * ff_chunk)
        hid = jnp.dot(f, wu_ref[:, sl], preferred_element_type=F32)
        hid = jnp.square(jnp.maximum(hid, 0.0)).astype(BF16)
        acc = acc + jnp.dot(hid, wd_ref[sl, :], preferred_element_type=F32)
    y_ref[...] = x1 + _rms(acc, gqf_ref[...])


def _out_ffn(og, od, x2d, w_out, w_up, w_down, g_post_mix, g_pre_ffn, g_post_ffn, *, tm):
    n, d = x2d.shape
    assert n % tm == 0 and w_up.shape[1] % FF_CHUNK == 0
    row = lambda i: (i, 0)
    return pl.pallas_call(
        functools.partial(_out_ffn_kernel, ff_chunk=FF_CHUNK),
        grid=(n // tm,),
        in_specs=[pl.BlockSpec((tm, W_V), row), pl.BlockSpec((tm, W_V), row),
                  pl.BlockSpec((tm, d), row),
                  _const_spec(w_out.shape), _const_spec(w_up.shape), _const_spec(w_down.shape),
                  _const_spec((1, d)), _const_spec((1, d)), _const_spec((1, d))],
        out_specs=pl.BlockSpec((tm, d), row),
        out_shape=jax.ShapeDtypeStruct((n, d), F32),
        compiler_params=pltpu.CompilerParams(
            dimension_semantics=("arbitrary",), vmem_limit_bytes=VMEM_LIMIT_BYTES),
        name="out_ffn",
    )(og, od, x2d, w_out, w_up, w_down, g_post_mix, g_pre_ffn, g_post_ffn)


def _split_w_in(w):
    a0 = 2 * W_G + 2 * W_V
    a1 = a0 + GLA_RANK
    w_a = jnp.pad(w[:, a0:a1], ((0, 0), (0, LANES - GLA_RANK)))
    return w[:, :a0].astype(BF16), w[:, a1:].astype(BF16), w_a.astype(BF16)


def kernel(x_prompt, x_sample, cache_k, cache_v, state_gla, w_in, w_gate_up, b_gate, g_gla_out,
           lam_q1, lam_k1, lam_q2, lam_k2, g_subln, w_out, g_pre_mix, g_post_mix,
           g_pre_ffn, g_post_ffn, w_ff_up, w_ff_down):
    depth = w_in.shape[0]
    nb_p, t_p, d = x_prompt.shape
    nb_s, t_s, _ = x_sample.shape
    past = cache_k.shape[2]
    slopes = jnp.exp2(-8.0 / DIFF_HEADS * jnp.arange(1, DIFF_HEADS + 1, dtype=F32))
    yp = x_prompt.reshape(nb_p * t_p, d)
    ys = x_sample.reshape(nb_s * t_s, d)
    outs = [[] for _ in range(6)]
    for l in range(depth):
        lam_init = 0.8 - 0.6 * math.exp(-0.3 * l)
        w_parts = _split_w_in(w_in[l])
        wg_pad = jnp.concatenate(
            [w_gate_up[l], jnp.zeros((LANES - GLA_RANK, W_G), F32)], axis=0).astype(BF16)
        bg = b_gate[l][None, :]
        lam_p = jnp.stack([lam_q1[l], lam_k1[l], lam_q2[l], lam_k2[l]], axis=0)
        g_out = g_gla_out[l][None, :]
        g_sub = g_subln[l][None, :]
        wo = w_out[l].astype(BF16)
        wu = w_ff_up[l].astype(BF16)
        wd = w_ff_down[l].astype(BF16)
        gains = (g_post_mix[l][None, :], g_pre_ffn[l][None, :], g_post_ffn[l][None, :])
        g_pre = g_pre_mix[l][None, :]

        qg, kg, vg, rg, gate, qd, kd, kdb, vd, vdb = _in_proj(
            yp, g_pre, w_parts, wg_pad, bg, nb=nb_p, t=t_p, tm=min(IN_PROJ_ROWS, t_p),
            transposed=True)
        s0 = jnp.zeros((nb_p, GLA_HEADS, GLA_DK, GLA_DV), F32)
        og, s_p = _gla(qg, kg, vg, rg, gate, s0, g_out, nb=nb_p, t=t_p, L=CHUNK,
                       CB=GLA_CHUNKS_PER_TILE, carry_state=True)
        od = _attn_prompt(slopes, lam_p, qd, kdb, vdb, g_sub, tq=ATTN_BLOCK, lam_init=lam_init)
        yp = _out_ffn(og, od, yp, wo, wu, wd, *gains, tm=min(OUT_FFN_ROWS, nb_p * t_p))
        outs[0].append(kd.reshape(nb_p, t_p, DIFF_HEADS, 2 * DIFF_DH))
        outs[1].append(vd.reshape(nb_p, t_p, DIFF_HEADS, DIFF_DV))
        outs[2].append(s_p)

        n_s = nb_s * t_s
        qg, kg, vg, rg, gate, qd, kd, _, vd, _ = _in_proj(
            ys, g_pre, w_parts, wg_pad, bg, nb=1, t=n_s, tm=n_s, transposed=False)
        og, s_s = _gla(qg, kg, vg, rg, gate, state_gla[l], g_out, nb=nb_s, t=t_s, L=t_s,
                       CB=math.gcd(nb_s, GLA_CHUNKS_PER_TILE), carry_state=False)
        od = _attn_sample(slopes, lam_p, qd,
                          cache_k[l].reshape(nb_s, past * DIFF_HEADS, 2 * DIFF_DH),
                          cache_v[l].reshape(nb_s, past * DIFF_HEADS, DIFF_DV),
                          kd, vd, g_sub, lam_init=lam_init)
        ys = _out_ffn(og, od, ys, wo, wu, wd, *gains, tm=n_s)
        outs[3].append(kd.reshape(nb_s, t_s, DIFF_HEADS, 2 * DIFF_DH))
        outs[4].append(vd.reshape(nb_s, t_s, DIFF_HEADS, DIFF_DV))
        outs[5].append(s_s)

    stack = lambda xs: jnp.stack(xs, axis=0)
    return (yp.reshape(nb_p, t_p, d), ys.reshape(nb_s, t_s, d),
            stack(outs[0]), stack(outs[1]), stack(outs[2]),
            stack(outs[3]), stack(outs[4]), stack(outs[5]))
```

```python
import functools
import math

import jax
import jax.numpy as jnp
from jax import lax
from jax.experimental import pallas as pl
from jax.experimental.pallas import tpu as pltpu

F32 = jnp.float32
BF16 = jnp.bfloat16

EPS = 1e-6
CHUNK = 64
GLA_HEADS = 4
GLA_DK = 64
GLA_DV = 128
GLA_RANK = 16
GLA_TAU = 16.0
DIFF_HEADS = 4
DIFF_DH = 64
DIFF_DV = 128

W_G = GLA_HEADS * GLA_DK
W_V = GLA_HEADS * GLA_DV
W_D = DIFF_HEADS * 2 * DIFF_DH
LANES = 128
OFF_QG = 0
OFF_KG = OFF_QG + W_G
OFF_VG = OFF_KG + W_G
OFF_RG = OFF_VG + W_V
OFF_QD = 0
OFF_KD = OFF_QD + W_D
OFF_VD = OFF_KD + W_D

BF16_SUBLANES = 16
V_AUG_ROWS = DIFF_DV + BF16_SUBLANES
LOG2E = math.log2(math.e)
K_WIDE = 4
K_UNROLL = 1
SCORE_AHEAD = 2
S_SLOTS = SCORE_AHEAD + 1

VMEM_LIMIT_BYTES = 56 * 1024 * 1024
IN_PROJ_ROWS = 1024
OUT_FFN_ROWS = 1024
FF_CHUNK = 1024
GLA_CHUNKS_PER_TILE = 8
ATTN_BLOCK = 256
GLA_SAFE_LOG_DECAY = -60.0


def _rms(x, g):
    ms = jnp.mean(x * x, axis=-1, keepdims=True)
    return x * lax.rsqrt(ms + EPS) * g


def _log_sigmoid(x):
    return jnp.minimum(x, 0.0) - jnp.log(1.0 + jnp.exp(-jnp.abs(x)))


def _div_pow2(x, d):
    assert d & (d - 1) == 0
    return lax.shift_right_arithmetic(x, d.bit_length() - 1)


def _const_spec(shape):
    zeros = (0,) * len(shape)
    return pl.BlockSpec(shape, lambda *_: zeros, pipeline_mode=pl.Buffered(1))


def _in_proj_kernel(x_ref, g_ref, w_gla_ref, w_diff_ref, w_a_ref, wg_ref, bg_ref,
                    qg_ref, kg_ref, vg_ref, rg_ref, gate_ref,
                    qd_ref, kd_ref, kdb_ref, vd_ref, vdb_ref, *, transposed):
    h = _rms(x_ref[...], g_ref[...]).astype(BF16)

    def proj(w_ref, off, width):
        return jnp.dot(h, w_ref[:, off:off + width], preferred_element_type=F32)

    qg_ref[...] = (proj(w_gla_ref, OFF_QG, W_G) * (GLA_DK ** -0.5)).astype(BF16)
    kg_ref[...] = proj(w_gla_ref, OFF_KG, W_G).astype(BF16)
    vg_ref[...] = proj(w_gla_ref, OFF_VG, W_V).astype(BF16)
    rg_ref[...] = proj(w_gla_ref, OFF_RG, W_V).astype(BF16)
    qd = proj(w_diff_ref, OFF_QD, W_D) * (DIFF_DH ** -0.5)
    kd = proj(w_diff_ref, OFF_KD, W_D)
    vd = proj(w_diff_ref, OFF_VD, W_D)
    tm = qd.shape[0]
    for hd in range(DIFF_HEADS):
        head_rows = pl.ds(hd, tm, stride=DIFF_HEADS)
        kd_ref[0, head_rows, :] = kd[:, hd * LANES:(hd + 1) * LANES]
        vd_ref[0, head_rows, :] = vd[:, hd * LANES:(hd + 1) * LANES]
    ones_rows = (lax.broadcasted_iota(jnp.int32, (V_AUG_ROWS - DIFF_DV, tm), 0) == 0).astype(BF16)
    for hd in range(DIFF_HEADS):
        sl = slice(hd * LANES, (hd + 1) * LANES)
        kdb_ref[0, hd] = kd[:, sl].astype(BF16)
        if transposed:
            qd_ref[0, hd] = (qd[:, sl] * LOG2E).T.astype(BF16)
            vdb_ref[0, hd, 0:DIFF_DV, :] = vd[:, sl].T.astype(BF16)
            vdb_ref[0, hd, DIFF_DV:V_AUG_ROWS, :] = ones_rows
        else:
            qd_ref[0, hd] = qd[:, sl].astype(BF16)
            vdb_ref[0, hd] = vd[:, sl].astype(BF16)
    ag = proj(w_a_ref, 0, LANES).astype(BF16)
    pre = jnp.dot(ag, wg_ref[...], preferred_element_type=F32) + bg_ref[...]
    gate_ref[...] = _log_sigmoid(pre) * (1.0 / GLA_TAU)


def _in_proj(x2d, g_pre, w_parts, wg_pad, b_gate, *, nb, t, tm, transposed):
    n, d = x2d.shape
    assert t % tm == 0 and n == nb * t
    steps_per_b = t // tm
    row = lambda i: (i, 0)
    hm = lambda i: (i // steps_per_b, 0, i % steps_per_b, 0)
    hm_t = lambda i: (i // steps_per_b, 0, 0, i % steps_per_b)
    hm_shape = jax.ShapeDtypeStruct((nb, DIFF_HEADS, t, LANES), BF16)
    hm_spec = pl.BlockSpec((1, DIFF_HEADS, tm, LANES), hm)
    kv_shape = jax.ShapeDtypeStruct((nb, t * DIFF_HEADS, LANES), F32)
    kv_spec = pl.BlockSpec((1, tm * DIFF_HEADS, LANES),
                           lambda i: (i // steps_per_b, i % steps_per_b, 0))
    if transposed:
        q_shape = jax.ShapeDtypeStruct((nb, DIFF_HEADS, LANES, t), BF16)
        q_spec = pl.BlockSpec((1, DIFF_HEADS, LANES, tm), hm_t)
        v_shape = jax.ShapeDtypeStruct((nb, DIFF_HEADS, V_AUG_ROWS, t), BF16)
        v_spec = pl.BlockSpec((1, DIFF_HEADS, V_AUG_ROWS, tm), hm_t)
    else:
        q_shape, q_spec, v_shape, v_spec = hm_shape, hm_spec, hm_shape, hm_spec
    out_shape = (
        jax.ShapeDtypeStruct((n, W_G), BF16), jax.ShapeDtypeStruct((n, W_G), BF16),
        jax.ShapeDtypeStruct((n, W_V), BF16), jax.ShapeDtypeStruct((n, W_V), BF16),
        jax.ShapeDtypeStruct((n, W_G), F32),
        q_shape,
        kv_shape, hm_shape,
        kv_shape, v_shape,
    )
    out_specs = (
        pl.BlockSpec((tm, W_G), row), pl.BlockSpec((tm, W_G), row),
        pl.BlockSpec((tm, W_V), row), pl.BlockSpec((tm, W_V), row),
        pl.BlockSpec((tm, W_G), row),
        q_spec,
        kv_spec, hm_spec,
        kv_spec, v_spec,
    )
    return pl.pallas_call(
        functools.partial(_in_proj_kernel, transposed=transposed),
        grid=(n // tm,),
        in_specs=[pl.BlockSpec((tm, d), row), _const_spec((1, d)),
                  *[_const_spec(w.shape) for w in w_parts],
                  _const_spec(wg_pad.shape), _const_spec((1, W_G))],
        out_specs=out_specs,
        out_shape=out_shape,
        compiler_params=pltpu.CompilerParams(
            dimension_semantics=("arbitrary",), vmem_limit_bytes=VMEM_LIMIT_BYTES),
        name="in_proj",
    )(x2d, g_pre, *w_parts, wg_pad, b_gate)


def _gla_kernel(q_ref, k_ref, v_ref, r_ref, gate_ref, s0_ref, gout_ref,
                o_ref, sout_ref, s_scr, oi_scr, qf_scr, kf_scr, bf_scr, b_scr,
                *, L, CB, carry_state):
    H, DK, DV = GLA_HEADS, GLA_DK, GLA_DV
    HL = H * L
    step = pl.program_id(1)

    if carry_state:
        @pl.when(step == 0)
        def _():
            s_scr[...] = s0_ref[0].reshape(H * DK, DV).T

    lane_head = _div_pow2(lax.broadcasted_iota(jnp.int32, (L, W_G), 1), DK)
    row_i = lax.broadcasted_iota(jnp.int32, (HL, HL), 0)
    col_i = lax.broadcasted_iota(jnp.int32, (HL, HL), 1)
    causal = (col_i <= row_i) & (col_i >= (row_i & ~(L - 1)))
    row_t = lax.broadcasted_iota(jnp.int32, (L, 1), 0)
    nt = (((1,), (1,)), ((), ()))
    tn = (((0,), (0,)), ((), ()))
    gout = gout_ref[...]

    def split_hi_lo(x):
        hi = x.astype(BF16)
        return hi, (x - hi.astype(F32)).astype(BF16)

    tri = jnp.where(lax.broadcasted_iota(jnp.int32, (L, L), 1)
                    <= lax.broadcasted_iota(jnp.int32, (L, L), 0), 1.0, 0.0).astype(BF16)
    totals = []
    for c in range(CB):
        c_hi, c_lo = split_hi_lo(gate_ref[c * L:(c + 1) * L, :])
        b_c = (jnp.dot(tri, c_hi, preferred_element_type=F32)
               + jnp.dot(tri, c_lo, preferred_element_type=F32))
        b_scr[c * L:(c + 1) * L, :] = b_c
        totals.append(b_c[L - 1:L, :])
    safe = jnp.min(jnp.concatenate(totals, axis=0)) >= GLA_SAFE_LOG_DECAY

    def stack_heads(a):
        return jnp.concatenate(
            [jnp.where(lane_head == hd, a, 0.0) for hd in range(H)], axis=0)

    def prepare(rows):
        b = b_scr[rows, :]
        b_last = b[L - 1:L, :]
        q = q_ref[rows, :].astype(F32)
        k = k_ref[rows, :].astype(F32)
        v = v_ref[rows, :]
        qs = stack_heads(q * jnp.exp(b)).astype(BF16)
        kends = stack_heads(k * jnp.exp(b_last - b)).astype(BF16)
        vs = jnp.concatenate([v[:, hd * DV:(hd + 1) * DV] for hd in range(H)], axis=0)
        kv_t = lax.dot_general(vs, kends, tn, preferred_element_type=F32)
        return dict(rows=rows, q=q, k=k, b=b, b_last=b_last, qs=qs, vs=vs, kv_t=kv_t)

    def intra_scores(p):
        ks = stack_heads(p["k"] * jnp.exp(-p["b"])).astype(BF16)
        return lax.dot_general(p["qs"], ks, nt, preferred_element_type=F32)

    def intra_apply(p, a):
        a = jnp.where(causal, a, 0.0).astype(BF16)
        return jnp.dot(a, p["vs"], preferred_element_type=F32)

    def intra_per_token(p):
        rows = p["rows"]
        qf_scr[...] = p["q"]
        kf_scr[...] = p["k"]
        bf_scr[...] = p["b"]

        def tok(t, carry_t):
            w = (qf_scr[pl.ds(t, 1), :] * kf_scr[...]
                 * jnp.exp(jnp.minimum(bf_scr[pl.ds(t, 1), :] - bf_scr[...], 0.0)))
            for hd in range(H):
                a_col = jnp.sum(jnp.where(lane_head == hd, w, 0.0), axis=-1, keepdims=True)
                a_col = jnp.where(row_t <= t, a_col, 0.0)
                vh = v_ref[rows, hd * DV:(hd + 1) * DV].astype(F32)
                oi_scr[pl.ds(hd * L + t, 1), :] = jnp.sum(a_col * vh, axis=0, keepdims=True)
            return carry_t

        lax.fori_loop(0, L, tok, 0)
        return oi_scr[...]

    def finish(p, o_intra, s_old):
        rows = p["rows"]
        o = (lax.dot_general(p["qs"], s_old.astype(BF16), nt, preferred_element_type=F32)
             + o_intra)
        o = _rms(o, gout)
        r = r_ref[rows, :].astype(F32)
        for hd in range(H):
            rh = r[:, hd * DV:(hd + 1) * DV]
            o_ref[rows, hd * DV:(hd + 1) * DV] = (
                o[hd * L:(hd + 1) * L, :] * (rh * jax.nn.sigmoid(rh))).astype(BF16)
        return jnp.exp(p["b_last"]) * s_old + p["kv_t"]

    def state_in(c):
        return s_scr[...] if carry_state else s0_ref[c].reshape(H * DK, DV).T

    def state_out(c, s):
        if carry_state:
            s_scr[...] = s
        else:
            sout_ref[c] = s.T.reshape(H, DK, DV)

    @pl.when(safe)
    def _():
        prep, raw, o_intra = {}, {}, {}
        s = state_in(0)
        for c in range(CB + 2):
            if c < CB:
                prep[c] = prepare(slice(c * L, (c + 1) * L))
                raw[c] = intra_scores(prep[c])
            if 1 <= c <= CB:
                o_intra[c - 1] = intra_apply(prep[c - 1], raw.pop(c - 1))
            if c >= 2:
                d = c - 2
                if not carry_state and d > 0:
                    s = state_in(d)
                s = finish(prep.pop(d), o_intra.pop(d), s)
                if not carry_state or d == CB - 1:
                    state_out(d, s)

    @pl.when(jnp.logical_not(safe))
    def _():
        def body(c, carry):
            p = prepare(pl.ds(pl.multiple_of(c * L, L), L))
            state_out(c, finish(p, intra_per_token(p), state_in(c)))
            return carry

        lax.fori_loop(0, CB, body, 0)

    if carry_state:
        @pl.when(step == pl.num_programs(1) - 1)
        def _():
            sout_ref[0] = s_scr[...].T.reshape(H, DK, DV)


def _gla(qg, kg, vg, rg, gate, s0, g_out, *, nb, t, L, CB, carry_state):
    n = qg.shape[0]
    tm = L * CB
    if carry_state:
        assert t % tm == 0
        grid = (nb, t // tm)
        steps = t // tm
        row = lambda b, s: (b * steps + s, 0)
        st = lambda b, s: (b, 0, 0, 0)
        state_block = (1, GLA_HEADS, GLA_DK, GLA_DV)
    else:
        assert t == L and nb % CB == 0
        grid = (nb // CB, 1)
        row = lambda b, s: (b, 0)
        st = lambda b, s: (b, 0, 0, 0)
        state_block = (CB, GLA_HEADS, GLA_DK, GLA_DV)
    return pl.pallas_call(
        functools.partial(_gla_kernel, L=L, CB=CB, carry_state=carry_state),
        grid=grid,
        in_specs=[pl.BlockSpec((tm, W_G), row), pl.BlockSpec((tm, W_G), row),
                  pl.BlockSpec((tm, W_V), row), pl.BlockSpec((tm, W_V), row),
                  pl.BlockSpec((tm, W_G), row), pl.BlockSpec(state_block, st),
                  pl.BlockSpec((1, GLA_DV), lambda b, s: (0, 0))],
        out_specs=(pl.BlockSpec((tm, W_V), row), pl.BlockSpec(state_block, st)),
        out_shape=(jax.ShapeDtypeStruct((n, W_V), BF16),
                   jax.ShapeDtypeStruct((nb, GLA_HEADS, GLA_DK, GLA_DV), F32)),
        scratch_shapes=[pltpu.VMEM((GLA_DV, GLA_HEADS * GLA_DK), F32),
                        pltpu.VMEM((GLA_HEADS * L, GLA_DV), F32),
                        pltpu.VMEM((L, W_G), F32), pltpu.VMEM((L, W_G), F32),
                        pltpu.VMEM((L, W_G), F32), pltpu.VMEM((tm, W_G), F32)],
        compiler_params=pltpu.CompilerParams(
            dimension_semantics=("arbitrary", "arbitrary"), vmem_limit_bytes=VMEM_LIMIT_BYTES),
        name="gla",
    )(qg, kg, vg, rg, gate, s0, g_out)


def _lambda(lam_ref, lam_init):
    lp = lam_ref[...]
    s1 = jnp.sum(lp[0:1, :] * lp[1:2, :], axis=-1, keepdims=True)
    s2 = jnp.sum(lp[2:3, :] * lp[3:4, :], axis=-1, keepdims=True)
    return jnp.exp(s1) - jnp.exp(s2) + lam_init


def _split_halves(q):
    lane = lax.broadcasted_iota(jnp.int32, q.shape, 1)
    zero = jnp.zeros_like(q)
    return jnp.concatenate(
        [jnp.where(lane < DIFF_DH, q, zero), jnp.where(lane >= DIFF_DH, q, zero)], axis=0)


def _finish_attention(acc, l, lam, gsub, lam_init, tq):
    o = acc[:tq] / l[:tq] - lam * (acc[tq:] / l[tq:])
    return _rms(o, gsub) * (1.0 - lam_init)


def _split3_bf16(x):
    hi = x.astype(BF16).astype(F32)
    r = x - hi
    mid = r.astype(BF16).astype(F32)
    lo = (r - mid).astype(BF16).astype(F32)
    return hi, mid, lo


def _attn_prompt_kernel(slopes_ref, lam_ref, qt_ref, k_ref, vt_ref, gsub_ref, o_ref,
                        aug_scr, qt_scr, corr_scr, acc_scr, m_scr, s_scr, *, tq, lam_init):
    nh = DIFF_HEADS
    tk = tq
    t = k_ref.shape[2]
    nq = t // tq
    slope2 = [slopes_ref[hd] * LOG2E for hd in range(nh)]
    j_lane = 3 * nh

    lane = lax.broadcasted_iota(jnp.int32, (tk, LANES), 1)
    k_loc = lax.broadcasted_iota(jnp.int32, (tk, LANES), 0).astype(F32)
    base = jnp.zeros((tk, LANES), F32)
    for hd in range(nh):
        for i, part in enumerate(_split3_bf16(slope2[hd] * k_loc)):
            base = jnp.where(lane == 3 * hd + i, part, base)
    blk_lanes = (lane >= j_lane) & (lane < j_lane + 3)
    for j in range(t // tk):
        aug_scr[j * tk:(j + 1) * tk, :] = jnp.where(blk_lanes, float(j), base).astype(BF16)

    row = lax.broadcasted_iota(jnp.int32, (LANES, 2 * tq), 0)
    kl = lax.broadcasted_iota(jnp.int32, (tk, tq), 0)
    ql = lax.broadcasted_iota(jnp.int32, (tk, tq), 1)
    same_chunk_or_earlier = _div_pow2(kl, CHUNK) <= _div_pow2(ql, CHUNK)
    ahead = jnp.maximum(kl - ql, 0).astype(F32)
    for hd in range(nh):
        rows = jnp.where((row >= 3 * hd) & (row < 3 * hd + 3), 1.0, 0.0)
        for i, part in enumerate(_split3_bf16(jnp.full((LANES, 2 * tq), slope2[hd] * tk, F32))):
            rows = jnp.where(row == j_lane + i, part, rows)
        qt_scr[hd, LANES:2 * LANES, :] = rows.astype(BF16)
        qt_scr[hd, DIFF_DH:2 * DIFF_DH, 0:tq] = jnp.zeros((DIFF_DH, tq), BF16)
        qt_scr[hd, 0:DIFF_DH, tq:2 * tq] = jnp.zeros((DIFF_DH, tq), BF16)
        corr_scr[hd] = jnp.where(same_chunk_or_earlier, (-2.0 * slope2[hd]) * ahead, -jnp.inf)

    lam = _lambda(lam_ref, lam_init)
    gsub = gsub_ref[...]

    def ksteps(steps):
        items = [(j0, nblk, diag, hd) for j0, nblk, diag in steps for hd in range(nh)]

        def key_rows(j0, nblk):
            return pl.ds(pl.multiple_of(j0 * tk, tk), nblk * tk)

        def scores(i):
            j0, nblk, diag, hd = items[i]
            ks = key_rows(j0, nblk)
            k_aug = jnp.concatenate([k_ref[0, hd, ks, :], aug_scr[ks, :]], axis=1)
            s_scr[i % S_SLOTS, 0:nblk * tk, :] = jnp.dot(
                k_aug, qt_scr[hd], preferred_element_type=F32)

        def softmax(i):
            j0, nblk, diag, hd = items[i]
            p_parts, alpha_parts = [], []
            for g in range(2 * tq // LANES):
                lanes = slice(g * LANES, (g + 1) * LANES)
                sg = s_scr[i % S_SLOTS, 0:nblk * tk, lanes]
                if diag:
                    c0 = (g * LANES) % tq
                    sg = sg + corr_scr[hd, :, c0:c0 + LANES]
                m_old = m_scr[hd:hd + 1, lanes]
                m_new = jnp.maximum(m_old, jnp.max(sg, axis=0, keepdims=True))
                m_scr[hd:hd + 1, lanes] = m_new
                p_parts.append(jnp.exp2(sg - m_new).astype(BF16))
                alpha_parts.append(jnp.exp2(m_old - m_new))
            return jnp.concatenate(p_parts, axis=1), jnp.concatenate(alpha_parts, axis=1)

        def accumulate(i, p, alpha):
            j0, nblk, diag, hd = items[i]
            acc_scr[hd] = alpha * acc_scr[hd] + jnp.dot(
                vt_ref[0, hd, :, key_rows(j0, nblk)], p, preferred_element_type=F32)

        n = len(items)
        for i in range(min(SCORE_AHEAD, n)):
            scores(i)
        pending = None
        for i in range(n):
            if i + SCORE_AHEAD < n:
                scores(i + SCORE_AHEAD)
            p_alpha = softmax(i)
            if pending is not None:
                accumulate(i - 1, *pending)
            pending = p_alpha
        accumulate(n - 1, *pending)

    def qblock(qi, carry):
        qs = pl.ds(pl.multiple_of(qi * tq, tq), tq)
        for hd in range(nh):
            qt = qt_ref[0, hd, :, qs]
            qt_scr[hd, 0:DIFF_DH, 0:tq] = qt[0:DIFF_DH, :]
            qt_scr[hd, DIFF_DH:2 * DIFF_DH, tq:2 * tq] = qt[DIFF_DH:2 * DIFF_DH, :]
        m_scr[...] = jnp.full(m_scr.shape, -jnp.inf, F32)
        acc_scr[...] = jnp.zeros(acc_scr.shape, F32)

        per_trip = K_UNROLL * K_WIDE

        def off_diag(jj, c):
            ksteps([(jj * per_trip + u * K_WIDE, K_WIDE, False) for u in range(K_UNROLL)])
            return c

        lax.fori_loop(0, qi // per_trip, off_diag, 0)
        done = (qi // per_trip) * per_trip
        for left in range(per_trip):
            @pl.when(qi - done == left)
            def _(left=left):
                steps, off, size = [], 0, per_trip // 2
                while size >= 1:
                    if left & size:
                        nblk = min(size, K_WIDE)
                        steps += [(done + off + u * nblk, nblk, False) for u in range(size // nblk)]
                        off += size
                    size //= 2
                ksteps(steps + [(qi, 1, True)])

        for hd in range(nh):
            acc = acc_scr[hd]
            inv_l = 1.0 / acc[DIFF_DV:DIFF_DV + 1, :]
            o_t = (acc[0:DIFF_DV, 0:tq] * inv_l[:, 0:tq]
                   - lam * (acc[0:DIFF_DV, tq:2 * tq] * inv_l[:, tq:2 * tq]))
            ms = jnp.mean(o_t * o_t, axis=0, keepdims=True)
            o_t = o_t * lax.rsqrt(ms + EPS)
            o_ref[qs, hd * DIFF_DV:(hd + 1) * DIFF_DV] = (
                o_t.T * gsub * (1.0 - lam_init)).astype(BF16)
        return carry

    lax.fori_loop(0, nq, qblock, 0)


def _attn_prompt(slopes, lam_p, qdt, kdb, vdt, g_sub, *, tq, lam_init):
    nb, nh, t, _ = kdb.shape
    assert t % tq == 0 and tq % CHUNK == 0 and nh == DIFF_HEADS
    per_stream = lambda b, *_: (b, 0, 0, 0)
    grid_spec = pltpu.PrefetchScalarGridSpec(
        num_scalar_prefetch=1,
        grid=(nb,),
        in_specs=[pl.BlockSpec((4, DIFF_DH), lambda b, *_: (0, 0)),
                  pl.BlockSpec((1, nh, LANES, t), per_stream),
                  pl.BlockSpec((1, nh, t, LANES), per_stream),
                  pl.BlockSpec((1, nh, V_AUG_ROWS, t), per_stream),
                  pl.BlockSpec((1, DIFF_DV), lambda b, *_: (0, 0))],
        out_specs=pl.BlockSpec((t, nh * DIFF_DV), lambda b, *_: (b, 0)),
        scratch_shapes=[pltpu.VMEM((t, LANES), BF16),
                        pltpu.VMEM((nh, 2 * LANES, 2 * tq), BF16),
                        pltpu.VMEM((nh, tq, tq), F32),
                        pltpu.VMEM((nh, V_AUG_ROWS, 2 * tq), F32),
                        pltpu.VMEM((8, 2 * tq), F32),
                        pltpu.VMEM((S_SLOTS, K_WIDE * tq, 2 * tq), F32)],
    )
    return pl.pallas_call(
        functools.partial(_attn_prompt_kernel, tq=tq, lam_init=lam_init),
        grid_spec=grid_spec,
        out_shape=jax.ShapeDtypeStruct((nb * t, nh * DIFF_DV), BF16),
        compiler_params=pltpu.CompilerParams(
            dimension_semantics=("arbitrary",), vmem_limit_bytes=VMEM_LIMIT_BYTES),
        name="attn_prompt",
    )(slopes, lam_p, qdt, kdb, vdt, g_sub)


def _attn_sample_kernel(slopes_ref, lam_ref, q_ref, ck_ref, cv_ref, kn_ref, vn_ref, gsub_ref, o_ref,
                        *, tq, past, lam_init):
    nh = DIFF_HEADS
    nt = (((1,), (1,)), ((), ()))
    k_pos = lax.broadcasted_iota(jnp.int32, (1, past), 1).astype(F32)
    qi = lax.broadcasted_iota(jnp.int32, (tq, tq), 0)
    kj = lax.broadcasted_iota(jnp.int32, (tq, tq), 1)
    rel_new = (past + qi - jnp.abs(qi - kj)).astype(F32)
    lam = _lambda(lam_ref, lam_init)
    gsub = gsub_ref[...]
    for hd in range(nh):
        slope = slopes_ref[hd]
        old = pl.ds(hd, past, stride=nh)
        new = pl.ds(hd, tq, stride=nh)
        qq = _split_halves(q_ref[0, hd])
        s_c = lax.dot_general(qq, ck_ref[0, old, :].astype(BF16), nt, preferred_element_type=F32)
        s_c = s_c + slope * k_pos
        s_n = lax.dot_general(qq, kn_ref[0, new, :].astype(BF16), nt, preferred_element_type=F32)
        bias_n = slope * rel_new
        s_n = s_n + jnp.concatenate([bias_n, bias_n], axis=0)
        m = jnp.maximum(jnp.max(s_c, axis=-1, keepdims=True), jnp.max(s_n, axis=-1, keepdims=True))
        p_c = jnp.exp(s_c - m)
        p_n = jnp.exp(s_n - m)
        l = jnp.sum(p_c, axis=-1, keepdims=True) + jnp.sum(p_n, axis=-1, keepdims=True)
        acc = (jnp.dot(p_c.astype(BF16), cv_ref[0, old, :].astype(BF16), preferred_element_type=F32)
               + jnp.dot(p_n.astype(BF16), vn_ref[0, new, :].astype(BF16),
                         preferred_element_type=F32))
        o_ref[:, hd * DIFF_DV:(hd + 1) * DIFF_DV] = _finish_attention(
            acc, l, lam, gsub, lam_init, tq).astype(BF16)


def _attn_sample(slopes, lam_p, qd, cache_k, cache_v, k_new, v_new, g_sub, *, lam_init):
    nb, rows, _ = cache_k.shape
    nh = DIFF_HEADS
    past = rows // nh
    tq = k_new.shape[1] // (nb * nh)
    cache_spec = pl.BlockSpec((1, past * nh, LANES), lambda b, *_: (b, 0, 0))
    new_spec = pl.BlockSpec((1, tq * nh, LANES), lambda b, *_: (0, b, 0))
    grid_spec = pltpu.PrefetchScalarGridSpec(
        num_scalar_prefetch=1,
        grid=(nb,),
        in_specs=[pl.BlockSpec((4, DIFF_DH), lambda b, *_: (0, 0)),
                  pl.BlockSpec((1, nh, tq, LANES), lambda b, *_: (0, 0, b, 0)),
                  cache_spec, cache_spec, new_spec, new_spec,
                  pl.BlockSpec((1, DIFF_DV), lambda b, *_: (0, 0))],
        out_specs=pl.BlockSpec((tq, nh * DIFF_DV), lambda b, *_: (b, 0)),
    )
    return pl.pallas_call(
        functools.partial(_attn_sample_kernel, tq=tq, past=past, lam_init=lam_init),
        grid_spec=grid_spec,
        out_shape=jax.ShapeDtypeStruct((nb * tq, nh * DIFF_DV), BF16),
        compiler_params=pltpu.CompilerParams(
            dimension_semantics=("arbitrary",), vmem_limit_bytes=VMEM_LIMIT_BYTES),
        name="attn_sample",
    )(slopes, lam_p, qd, cache_k, cache_v, k_new, v_new, g_sub)


def _out_ffn_kernel(og_ref, od_ref, x_ref, wo_ref, wu_ref, wd_ref,
                    gpm_ref, gpf_ref, gqf_ref, y_ref, *, ff_chunk):
    mix = (jnp.dot(og_ref[...], wo_ref[:W_V, :], preferred_element_type=F32)
           + jnp.dot(od_ref[...], wo_ref[W_V:, :], preferred_element_type=F32))
    x1 = x_ref[...] + _rms(mix, gpm_ref[...])
    f = _rms(x1, gpf_ref[...]).astype(BF16)
    d_ff = wu_ref.shape[1]
    acc = jnp.zeros(x1.shape, F32)
    for c in range(d_ff // ff_chunk):
        sl = slice(c * ff_chunk, (c + 1) * ff_chunk)
        hid = jnp.dot(f, wu_ref[:, sl], preferred_element_type=F32)
        hid = jnp.square(jnp.maximum(hid, 0.0)).astype(BF16)
        acc = acc + jnp.dot(hid, wd_ref[sl, :], preferred_element_type=F32)
    y_ref[...] = x1 + _rms(acc, gqf_ref[...])


def _out_ffn(og, od, x2d, w_out, w_up, w_down, g_post_mix, g_pre_ffn, g_post_ffn, *, tm):
    n, d = x2d.shape
    assert n % tm == 0 and w_up.shape[1] % FF_CHUNK == 0
    row = lambda i: (i, 0)
    return pl.pallas_call(
        functools.partial(_out_ffn_kernel, ff_chunk=FF_CHUNK),
        grid=(n // tm,),
        in_specs=[pl.BlockSpec((tm, W_V), row), pl.BlockSpec((tm, W_V), row),
                  pl.BlockSpec((tm, d), row),
                  _const_spec(w_out.shape), _const_spec(w_up.shape), _const_spec(w_down.shape),
                  _const_spec((1, d)), _const_spec((1, d)), _const_spec((1, d))],
        out_specs=pl.BlockSpec((tm, d), row),
        out_shape=jax.ShapeDtypeStruct((n, d), F32),
        compiler_params=pltpu.CompilerParams(
            dimension_semantics=("arbitrary",), vmem_limit_bytes=VMEM_LIMIT_BYTES),
        name="out_ffn",
    )(og, od, x2d, w_out, w_up, w_down, g_post_mix, g_pre_ffn, g_post_ffn)


def _split_w_in(w):
    a0 = 2 * W_G + 2 * W_V
    a1 = a0 + GLA_RANK
    w_a = jnp.pad(w[:, a0:a1], ((0, 0), (0, LANES - GLA_RANK)))
    return w[:, :a0].astype(BF16), w[:, a1:].astype(BF16), w_a.astype(BF16)


def kernel(x_prompt, x_sample, cache_k, cache_v, state_gla, w_in, w_gate_up, b_gate, g_gla_out,
           lam_q1, lam_k1, lam_q2, lam_k2, g_subln, w_out, g_pre_mix, g_post_mix,
           g_pre_ffn, g_post_ffn, w_ff_up, w_ff_down):
    depth = w_in.shape[0]
    nb_p, t_p, d = x_prompt.shape
    nb_s, t_s, _ = x_sample.shape
    past = cache_k.shape[2]
    slopes = jnp.exp2(-8.0 / DIFF_HEADS * jnp.arange(1, DIFF_HEADS + 1, dtype=F32))
    yp = x_prompt.reshape(nb_p * t_p, d)
    ys = x_sample.reshape(nb_s * t_s, d)
    outs = [[] for _ in range(6)]
    for l in range(depth):
        lam_init = 0.8 - 0.6 * math.exp(-0.3 * l)
        w_parts = _split_w_in(w_in[l])
        wg_pad = jnp.concatenate(
            [w_gate_up[l], jnp.zeros((LANES - GLA_RANK, W_G), F32)], axis=0).astype(BF16)
        bg = b_gate[l][None, :]
        lam_p = jnp.stack([lam_q1[l], lam_k1[l], lam_q2[l], lam_k2[l]], axis=0)
        g_out = g_gla_out[l][None, :]
        g_sub = g_subln[l][None, :]
        wo = w_out[l].astype(BF16)
        wu = w_ff_up[l].astype(BF16)
        wd = w_ff_down[l].astype(BF16)
        gains = (g_post_mix[l][None, :], g_pre_ffn[l][None, :], g_post_ffn[l][None, :])
        g_pre = g_pre_mix[l][None, :]

        qg, kg, vg, rg, gate, qd, kd, kdb, vd, vdb = _in_proj(
            yp, g_pre, w_parts, wg_pad, bg, nb=nb_p, t=t_p, tm=min(IN_PROJ_ROWS, t_p),
            transposed=True)
        s0 = jnp.zeros((nb_p, GLA_HEADS, GLA_DK, GLA_DV), F32)
        og, s_p = _gla(qg, kg, vg, rg, gate, s0, g_out, nb=nb_p, t=t_p, L=CHUNK,
                       CB=GLA_CHUNKS_PER_TILE, carry_state=True)
        od = _attn_prompt(slopes, lam_p, qd, kdb, vdb, g_sub, tq=ATTN_BLOCK, lam_init=lam_init)
        yp = _out_ffn(og, od, yp, wo, wu, wd, *gains, tm=min(OUT_FFN_ROWS, nb_p * t_p))
        outs[0].append(kd.reshape(nb_p, t_p, DIFF_HEADS, 2 * DIFF_DH))
        outs[1].append(vd.reshape(nb_p, t_p, DIFF_HEADS, DIFF_DV))
        outs[2].append(s_p)

        n_s = nb_s * t_s
        qg, kg, vg, rg, gate, qd, kd, _, vd, _ = _in_proj(
            ys, g_pre, w_parts, wg_pad, bg, nb=1, t=n_s, tm=n_s, transposed=False)
        og, s_s = _gla(qg, kg, vg, rg, gate, state_gla[l], g_out, nb=nb_s, t=t_s, L=t_s,
                       CB=math.gcd(nb_s, GLA_CHUNKS_PER_TILE), carry_state=False)
        od = _attn_sample(slopes, lam_p, qd,
                          cache_k[l].reshape(nb_s, past * DIFF_HEADS, 2 * DIFF_DH),
                          cache_v[l].reshape(nb_s, past * DIFF_HEADS, DIFF_DV),
                          kd, vd, g_sub, lam_init=lam_init)
        ys = _out_ffn(og, od, ys, wo, wu, wd, *gains, tm=n_s)
        outs[3].append(kd.reshape(nb_s, t_s, DIFF_HEADS, 2 * DIFF_DH))
        outs[4].append(vd.reshape(nb_s, t_s, DIFF_HEADS, DIFF_DV))
        outs[5].append(s_s)

    stack = lambda xs: jnp.stack(xs, axis=0)
    return (yp.reshape(nb_p, t_p, d), ys.reshape(nb_s, t_s, d),
            stack(outs[0]), stack(outs[1]), stack(outs[2]),
            stack(outs[3]), stack(outs[4]), stack(outs[5]))
```

```python
import functools
import math

import jax
import jax.numpy as jnp
from jax import lax
from jax.experimental import pallas as pl
from jax.experimental.pallas import tpu as pltpu

F32 = jnp.float32
BF16 = jnp.bfloat16

EPS = 1e-6
CHUNK = 64
GLA_HEADS = 4
GLA_DK = 64
GLA_DV = 128
GLA_RANK = 16
GLA_TAU = 16.0
DIFF_HEADS = 4
DIFF_DH = 64
DIFF_DV = 128

W_G = GLA_HEADS * GLA_DK
W_V = GLA_HEADS * GLA_DV
W_D = DIFF_HEADS * 2 * DIFF_DH
LANES = 128
OFF_QG = 0
OFF_KG = OFF_QG + W_G
OFF_VG = OFF_KG + W_G
OFF_RG = OFF_VG + W_V
OFF_QD = 0
OFF_KD = OFF_QD + W_D
OFF_VD = OFF_KD + W_D

BF16_SUBLANES = 16
V_AUG_ROWS = DIFF_DV + BF16_SUBLANES
LOG2E = math.log2(math.e)
K_WIDE = 4
K_UNROLL = 1
SCORE_AHEAD = 2
S_SLOTS = SCORE_AHEAD + 1

VMEM_LIMIT_BYTES = 56 * 1024 * 1024
IN_PROJ_ROWS = 1024
OUT_FFN_ROWS = 1024
FF_CHUNK = 1024
GLA_CHUNKS_PER_TILE = 8
ATTN_BLOCK = 256
GLA_SAFE_LOG_DECAY = -60.0


def _rms(x, g):
    ms = jnp.mean(x * x, axis=-1, keepdims=True)
    return x * lax.rsqrt(ms + EPS) * g


def _log_sigmoid(x):
    return jnp.minimum(x, 0.0) - jnp.log(1.0 + jnp.exp(-jnp.abs(x)))


def _div_pow2(x, d):
    assert d & (d - 1) == 0
    return lax.shift_right_arithmetic(x, d.bit_length() - 1)


def _const_spec(shape):
    zeros = (0,) * len(shape)
    return pl.BlockSpec(shape, lambda *_: zeros, pipeline_mode=pl.Buffered(1))


def _in_proj_kernel(x_ref, g_ref, w_gla_ref, w_diff_ref, w_a_ref, wg_ref, bg_ref,
                    qg_ref, kg_ref, vg_ref, rg_ref, gate_ref,
                    qd_ref, kd_ref, kdb_ref, vd_ref, vdb_ref, *, transposed):
    h = _rms(x_ref[...], g_ref[...]).astype(BF16)

    def proj(w_ref, off, width):
        return jnp.dot(h, w_ref[:, off:off + width], preferred_element_type=F32)

    qg_ref[...] = (proj(w_gla_ref, OFF_QG, W_G) * (GLA_DK ** -0.5)).astype(BF16)
    kg_ref[...] = proj(w_gla_ref, OFF_KG, W_G).astype(BF16)
    vg_ref[...] = proj(w_gla_ref, OFF_VG, W_V).astype(BF16)
    rg_ref[...] = proj(w_gla_ref, OFF_RG, W_V).astype(BF16)
    qd = proj(w_diff_ref, OFF_QD, W_D) * (DIFF_DH ** -0.5)
    kd = proj(w_diff_ref, OFF_KD, W_D)
    vd = proj(w_diff_ref, OFF_VD, W_D)
    tm = qd.shape[0]
    for hd in range(DIFF_HEADS):
        head_rows = pl.ds(hd, tm, stride=DIFF_HEADS)
        kd_ref[0, head_rows, :] = kd[:, hd * LANES:(hd + 1) * LANES]
        vd_ref[0, head_rows, :] = vd[:, hd * LANES:(hd + 1) * LANES]
    ones_rows = (lax.broadcasted_iota(jnp.int32, (V_AUG_ROWS - DIFF_DV, tm), 0) == 0).astype(BF16)
    for hd in range(DIFF_HEADS):
        sl = slice(hd * LANES, (hd + 1) * LANES)
        kdb_ref[0, hd] = kd[:, sl].astype(BF16)
        if transposed:
            qd_ref[0, hd] = (qd[:, sl] * LOG2E).T.astype(BF16)
            vdb_ref[0, hd, 0:DIFF_DV, :] = vd[:, sl].T.astype(BF16)
            vdb_ref[0, hd, DIFF_DV:V_AUG_ROWS, :] = ones_rows
        else:
            qd_ref[0, hd] = qd[:, sl].astype(BF16)
            vdb_ref[0, hd] = vd[:, sl].astype(BF16)
    ag = proj(w_a_ref, 0, LANES).astype(BF16)
    pre = jnp.dot(ag, wg_ref[...], preferred_element_type=F32) + bg_ref[...]
    gate_ref[...] = _log_sigmoid(pre) * (1.0 / GLA_TAU)


def _in_proj(x2d, g_pre, w_parts, wg_pad, b_gate, *, nb, t, tm, transposed):
    n, d = x2d.shape
    assert t % tm == 0 and n == nb * t
    steps_per_b = t // tm
    row = lambda i: (i, 0)
    hm = lambda i: (i // steps_per_b, 0, i % steps_per_b, 0)
    hm_t = lambda i: (i // steps_per_b, 0, 0, i % steps_per_b)
    hm_shape = jax.ShapeDtypeStruct((nb, DIFF_HEADS, t, LANES), BF16)
    hm_spec = pl.BlockSpec((1, DIFF_HEADS, tm, LANES), hm)
    kv_shape = jax.ShapeDtypeStruct((nb, t * DIFF_HEADS, LANES), F32)
    kv_spec = pl.BlockSpec((1, tm * DIFF_HEADS, LANES),
                           lambda i: (i // steps_per_b, i % steps_per_b, 0))
    if transposed:
        q_shape = jax.ShapeDtypeStruct((nb, DIFF_HEADS, LANES, t), BF16)
        q_spec = pl.BlockSpec((1, DIFF_HEADS, LANES, tm), hm_t)
        v_shape = jax.ShapeDtypeStruct((nb, DIFF_HEADS, V_AUG_ROWS, t), BF16)
        v_spec = pl.BlockSpec((1, DIFF_HEADS, V_AUG_ROWS, tm), hm_t)
    else:
        q_shape, q_spec, v_shape, v_spec = hm_shape, hm_spec, hm_shape, hm_spec
    out_shape = (
        jax.ShapeDtypeStruct((n, W_G), BF16), jax.ShapeDtypeStruct((n, W_G), BF16),
        jax.ShapeDtypeStruct((n, W_V), BF16), jax.ShapeDtypeStruct((n, W_V), BF16),
        jax.ShapeDtypeStruct((n, W_G), F32),
        q_shape,
        kv_shape, hm_shape,
        kv_shape, v_shape,
    )
    out_specs = (
        pl.BlockSpec((tm, W_G), row), pl.BlockSpec((tm, W_G), row),
        pl.BlockSpec((tm, W_V), row), pl.BlockSpec((tm, W_V), row),
        pl.BlockSpec((tm, W_G), row),
        q_spec,
        kv_spec, hm_spec,
        kv_spec, v_spec,
    )
    return pl.pallas_call(
        functools.partial(_in_proj_kernel, transposed=transposed),
        grid=(n // tm,),
        in_specs=[pl.BlockSpec((tm, d), row), _const_spec((1, d)),
                  *[_const_spec(w.shape) for w in w_parts],
                  _const_spec(wg_pad.shape), _const_spec((1, W_G))],
        out_specs=out_specs,
        out_shape=out_shape,
        compiler_params=pltpu.CompilerParams(
            dimension_semantics=("arbitrary",), vmem_limit_bytes=VMEM_LIMIT_BYTES),
        name="in_proj",
    )(x2d, g_pre, *w_parts, wg_pad, b_gate)


def _gla_kernel(q_ref, k_ref, v_ref, r_ref, gate_ref, s0_ref, gout_ref,
                o_ref, sout_ref, s_scr, oi_scr, qf_scr, kf_scr, bf_scr, b_scr,
                *, L, CB, carry_state):
    H, DK, DV = GLA_HEADS, GLA_DK, GLA_DV
    HL = H * L
    step = pl.program_id(1)

    if carry_state:
        @pl.when(step == 0)
        def _():
            s_scr[...] = s0_ref[0].reshape(H * DK, DV).T

    lane_head = _div_pow2(lax.broadcasted_iota(jnp.int32, (L, W_G), 1), DK)
    row_i = lax.broadcasted_iota(jnp.int32, (HL, HL), 0)
    col_i = lax.broadcasted_iota(jnp.int32, (HL, HL), 1)
    causal = (col_i <= row_i) & (col_i >= (row_i & ~(L - 1)))
    row_t = lax.broadcasted_iota(jnp.int32, (L, 1), 0)
    nt = (((1,), (1,)), ((), ()))
    tn = (((0,), (0,)), ((), ()))
    gout = gout_ref[...]

    def split_hi_lo(x):
        hi = x.astype(BF16)
        return hi, (x - hi.astype(F32)).astype(BF16)

    tri = jnp.where(lax.broadcasted_iota(jnp.int32, (L, L), 1)
                    <= lax.broadcasted_iota(jnp.int32, (L, L), 0), 1.0, 0.0).astype(BF16)
    totals = []
    for c in range(CB):
        g_c = gate_ref[c * L:(c + 1) * L, :]
        c_hi, c_lo = split_hi_lo(g_c)
        b_scr[c * L:(c + 1) * L, :] = (jnp.dot(tri, c_hi, preferred_element_type=F32)
                                       + jnp.dot(tri, c_lo, preferred_element_type=F32))
        totals.append(jnp.sum(g_c, axis=0, keepdims=True))
    safe = jnp.min(jnp.concatenate(totals, axis=0)) >= GLA_SAFE_LOG_DECAY

    def stack_heads(a):
        return jnp.concatenate(
            [jnp.where(lane_head == hd, a, 0.0) for hd in range(H)], axis=0)

    def prepare(rows):
        b = b_scr[rows, :]
        b_last = b[L - 1:L, :]
        q = q_ref[rows, :].astype(F32)
        k = k_ref[rows, :].astype(F32)
        v = v_ref[rows, :]
        qs = stack_heads(q * jnp.exp(b)).astype(BF16)
        kends = stack_heads(k * jnp.exp(b_last - b)).astype(BF16)
        vs = jnp.concatenate([v[:, hd * DV:(hd + 1) * DV] for hd in range(H)], axis=0)
        kv_t = lax.dot_general(vs, kends, tn, preferred_element_type=F32)
        return dict(rows=rows, q=q, k=k, b=b, b_last=b_last, qs=qs, vs=vs, kv_t=kv_t)

    def intra_scores(p):
        ks = stack_heads(p["k"] * jnp.exp(-p["b"])).astype(BF16)
        return lax.dot_general(p["qs"], ks, nt, preferred_element_type=F32)

    def intra_apply(p, a):
        a = jnp.where(causal, a, 0.0).astype(BF16)
        return jnp.dot(a, p["vs"], preferred_element_type=F32)

    def intra_per_token(p):
        rows = p["rows"]
        qf_scr[...] = p["q"]
        kf_scr[...] = p["k"]
        bf_scr[...] = p["b"]

        def tok(t, carry_t):
            w = (qf_scr[pl.ds(t, 1), :] * kf_scr[...]
                 * jnp.exp(jnp.minimum(bf_scr[pl.ds(t, 1), :] - bf_scr[...], 0.0)))
            for hd in range(H):
                a_col = jnp.sum(jnp.where(lane_head == hd, w, 0.0), axis=-1, keepdims=True)
                a_col = jnp.where(row_t <= t, a_col, 0.0)
                vh = v_ref[rows, hd * DV:(hd + 1) * DV].astype(F32)
                oi_scr[pl.ds(hd * L + t, 1), :] = jnp.sum(a_col * vh, axis=0, keepdims=True)
            return carry_t

        lax.fori_loop(0, L, tok, 0)
        return oi_scr[...]

    def finish(p, o_intra, s_old):
        rows = p["rows"]
        o = (lax.dot_general(p["qs"], s_old.astype(BF16), nt, preferred_element_type=F32)
             + o_intra)
        o = _rms(o, gout)
        r = r_ref[rows, :].astype(F32)
        for hd in range(H):
            rh = r[:, hd * DV:(hd + 1) * DV]
            o_ref[rows, hd * DV:(hd + 1) * DV] = (
                o[hd * L:(hd + 1) * L, :] * (rh * jax.nn.sigmoid(rh))).astype(BF16)
        return jnp.exp(p["b_last"]) * s_old + p["kv_t"]

    def state_in(c):
        return s_scr[...] if carry_state else s0_ref[c].reshape(H * DK, DV).T

    def state_out(c, s):
        if carry_state:
            s_scr[...] = s
        else:
            sout_ref[c] = s.T.reshape(H, DK, DV)

    @pl.when(safe)
    def _():
        prep, raw, o_intra = {}, {}, {}
        s = state_in(0)
        for c in range(CB + 2):
            if c < CB:
                prep[c] = prepare(slice(c * L, (c + 1) * L))
                raw[c] = intra_scores(prep[c])
            if 1 <= c <= CB:
                o_intra[c - 1] = intra_apply(prep[c - 1], raw.pop(c - 1))
            if c >= 2:
                d = c - 2
                if not carry_state and d > 0:
                    s = state_in(d)
                s = finish(prep.pop(d), o_intra.pop(d), s)
                if not carry_state or d == CB - 1:
                    state_out(d, s)

    @pl.when(jnp.logical_not(safe))
    def _():
        def body(c, carry):
            p = prepare(pl.ds(pl.multiple_of(c * L, L), L))
            state_out(c, finish(p, intra_per_token(p), state_in(c)))
            return carry

        lax.fori_loop(0, CB, body, 0)

    if carry_state:
        @pl.when(step == pl.num_programs(1) - 1)
        def _():
            sout_ref[0] = s_scr[...].T.reshape(H, DK, DV)


def _gla(qg, kg, vg, rg, gate, s0, g_out, *, nb, t, L, CB, carry_state):
    n = qg.shape[0]
    tm = L * CB
    if carry_state:
        assert t % tm == 0
        grid = (nb, t // tm)
        steps = t // tm
        row = lambda b, s: (b * steps + s, 0)
        st = lambda b, s: (b, 0, 0, 0)
        state_block = (1, GLA_HEADS, GLA_DK, GLA_DV)
    else:
        assert t == L and nb % CB == 0
        grid = (nb // CB, 1)
        row = lambda b, s: (b, 0)
        st = lambda b, s: (b, 0, 0, 0)
        state_block = (CB, GLA_HEADS, GLA_DK, GLA_DV)
    return pl.pallas_call(
        functools.partial(_gla_kernel, L=L, CB=CB, carry_state=carry_state),
        grid=grid,
        in_specs=[pl.BlockSpec((tm, W_G), row), pl.BlockSpec((tm, W_G), row),
                  pl.BlockSpec((tm, W_V), row), pl.BlockSpec((tm, W_V), row),
                  pl.BlockSpec((tm, W_G), row), pl.BlockSpec(state_block, st),
                  pl.BlockSpec((1, GLA_DV), lambda b, s: (0, 0))],
        out_specs=(pl.BlockSpec((tm, W_V), row), pl.BlockSpec(state_block, st)),
        out_shape=(jax.ShapeDtypeStruct((n, W_V), BF16),
                   jax.ShapeDtypeStruct((nb, GLA_HEADS, GLA_DK, GLA_DV), F32)),
        scratch_shapes=[pltpu.VMEM((GLA_DV, GLA_HEADS * GLA_DK), F32),
                        pltpu.VMEM((GLA_HEADS * L, GLA_DV), F32),
                        pltpu.VMEM((L, W_G), F32), pltpu.VMEM((L, W_G), F32),
                        pltpu.VMEM((L, W_G), F32), pltpu.VMEM((tm, W_G), F32)],
        compiler_params=pltpu.CompilerParams(
            dimension_semantics=("arbitrary", "arbitrary"), vmem_limit_bytes=VMEM_LIMIT_BYTES),
        name="gla",
    )(qg, kg, vg, rg, gate, s0, g_out)


def _lambda(lam_ref, lam_init):
    lp = lam_ref[...]
    s1 = jnp.sum(lp[0:1, :] * lp[1:2, :], axis=-1, keepdims=True)
    s2 = jnp.sum(lp[2:3, :] * lp[3:4, :], axis=-1, keepdims=True)
    return jnp.exp(s1) - jnp.exp(s2) + lam_init


def _split_halves(q):
    lane = lax.broadcasted_iota(jnp.int32, q.shape, 1)
    zero = jnp.zeros_like(q)
    return jnp.concatenate(
        [jnp.where(lane < DIFF_DH, q, zero), jnp.where(lane >= DIFF_DH, q, zero)], axis=0)


def _finish_attention(acc, l, lam, gsub, lam_init, tq):
    o = acc[:tq] / l[:tq] - lam * (acc[tq:] / l[tq:])
    return _rms(o, gsub) * (1.0 - lam_init)


def _split3_bf16(x):
    hi = x.astype(BF16).astype(F32)
    r = x - hi
    mid = r.astype(BF16).astype(F32)
    lo = (r - mid).astype(BF16).astype(F32)
    return hi, mid, lo


def _attn_prompt_kernel(slopes_ref, lam_ref, qt_ref, k_ref, vt_ref, gsub_ref, o_ref,
                        aug_scr, qt_scr, corr_scr, acc_scr, m_scr, s_scr, *, tq, lam_init):
    nh = DIFF_HEADS
    tk = tq
    t = k_ref.shape[2]
    nq = t // tq
    slope2 = [slopes_ref[hd] * LOG2E for hd in range(nh)]
    j_lane = 3 * nh

    lane = lax.broadcasted_iota(jnp.int32, (tk, LANES), 1)
    k_loc = lax.broadcasted_iota(jnp.int32, (tk, LANES), 0).astype(F32)
    base = jnp.zeros((tk, LANES), F32)
    for hd in range(nh):
        for i, part in enumerate(_split3_bf16(slope2[hd] * k_loc)):
            base = jnp.where(lane == 3 * hd + i, part, base)
    blk_lanes = (lane >= j_lane) & (lane < j_lane + 3)
    for j in range(t // tk):
        aug_scr[j * tk:(j + 1) * tk, :] = jnp.where(blk_lanes, float(j), base).astype(BF16)

    row = lax.broadcasted_iota(jnp.int32, (LANES, 2 * tq), 0)
    kl = lax.broadcasted_iota(jnp.int32, (tk, tq), 0)
    ql = lax.broadcasted_iota(jnp.int32, (tk, tq), 1)
    same_chunk_or_earlier = _div_pow2(kl, CHUNK) <= _div_pow2(ql, CHUNK)
    ahead = jnp.maximum(kl - ql, 0).astype(F32)
    for hd in range(nh):
        rows = jnp.where((row >= 3 * hd) & (row < 3 * hd + 3), 1.0, 0.0)
        for i, part in enumerate(_split3_bf16(jnp.full((LANES, 2 * tq), slope2[hd] * tk, F32))):
            rows = jnp.where(row == j_lane + i, part, rows)
        qt_scr[hd, LANES:2 * LANES, :] = rows.astype(BF16)
        qt_scr[hd, DIFF_DH:2 * DIFF_DH, 0:tq] = jnp.zeros((DIFF_DH, tq), BF16)
        qt_scr[hd, 0:DIFF_DH, tq:2 * tq] = jnp.zeros((DIFF_DH, tq), BF16)
        corr_scr[hd] = jnp.where(same_chunk_or_earlier, (-2.0 * slope2[hd]) * ahead, -jnp.inf)

    lam = _lambda(lam_ref, lam_init)
    gsub = gsub_ref[...]

    def ksteps(steps):
        items = [(j0, nblk, diag, hd) for j0, nblk, diag in steps for hd in range(nh)]

        def key_rows(j0, nblk):
            return pl.ds(pl.multiple_of(j0 * tk, tk), nblk * tk)

        def scores(i):
            j0, nblk, diag, hd = items[i]
            ks = key_rows(j0, nblk)
            k_aug = jnp.concatenate([k_ref[0, hd, ks, :], aug_scr[ks, :]], axis=1)
            s_scr[i % S_SLOTS, 0:nblk * tk, :] = jnp.dot(
                k_aug, qt_scr[hd], preferred_element_type=F32)

        def softmax(i):
            j0, nblk, diag, hd = items[i]
            p_parts, alpha_parts = [], []
            for g in range(2 * tq // LANES):
                lanes = slice(g * LANES, (g + 1) * LANES)
                sg = s_scr[i % S_SLOTS, 0:nblk * tk, lanes]
                if diag:
                    c0 = (g * LANES) % tq
                    sg = sg + corr_scr[hd, :, c0:c0 + LANES]
                m_old = m_scr[hd:hd + 1, lanes]
                m_new = jnp.maximum(m_old, jnp.max(sg, axis=0, keepdims=True))
                m_scr[hd:hd + 1, lanes] = m_new
                p_parts.append(jnp.exp2(sg - m_new).astype(BF16))
                alpha_parts.append(jnp.exp2(m_old - m_new))
            return jnp.concatenate(p_parts, axis=1), jnp.concatenate(alpha_parts, axis=1)

        def accumulate(i, p, alpha):
            j0, nblk, diag, hd = items[i]
            acc_scr[hd] = alpha * acc_scr[hd] + jnp.dot(
                vt_ref[0, hd, :, key_rows(j0, nblk)], p, preferred_element_type=F32)

        n = len(items)
        for i in range(min(SCORE_AHEAD, n)):
            scores(i)
        pending = None
        for i in range(n):
            if i + SCORE_AHEAD < n:
                scores(i + SCORE_AHEAD)
            p_alpha = softmax(i)
            if pending is not None:
                accumulate(i - 1, *pending)
            pending = p_alpha
        accumulate(n - 1, *pending)

    def qblock(qi, carry):
        qs = pl.ds(pl.multiple_of(qi * tq, tq), tq)
        for hd in range(nh):
            qt = qt_ref[0, hd, :, qs]
            qt_scr[hd, 0:DIFF_DH, 0:tq] = qt[0:DIFF_DH, :]
            qt_scr[hd, DIFF_DH:2 * DIFF_DH, tq:2 * tq] = qt[DIFF_DH:2 * DIFF_DH, :]
        m_scr[...] = jnp.full(m_scr.shape, -jnp.inf, F32)
        acc_scr[...] = jnp.zeros(acc_scr.shape, F32)

        per_trip = K_UNROLL * K_WIDE

        def off_diag(jj, c):
            ksteps([(jj * per_trip + u * K_WIDE, K_WIDE, False) for u in range(K_UNROLL)])
            return c

        lax.fori_loop(0, qi // per_trip, off_diag, 0)
        done = (qi // per_trip) * per_trip
        for left in range(per_trip):
            @pl.when(qi - done == left)
            def _(left=left):
                steps, off, size = [], 0, per_trip // 2
                while size >= 1:
                    if left & size:
                        nblk = min(size, K_WIDE)
                        steps += [(done + off + u * nblk, nblk, False) for u in range(size // nblk)]
                        off += size
                    size //= 2
                ksteps(steps + [(qi, 1, True)])

        for hd in range(nh):
            acc = acc_scr[hd]
            inv_l = 1.0 / acc[DIFF_DV:DIFF_DV + 1, :]
            o_t = (acc[0:DIFF_DV, 0:tq] * inv_l[:, 0:tq]
                   - lam * (acc[0:DIFF_DV, tq:2 * tq] * inv_l[:, tq:2 * tq]))
            ms = jnp.mean(o_t * o_t, axis=0, keepdims=True)
            o_t = o_t * lax.rsqrt(ms + EPS)
            o_ref[qs, hd * DIFF_DV:(hd + 1) * DIFF_DV] = (
                o_t.T * gsub * (1.0 - lam_init)).astype(BF16)
        return carry

    lax.fori_loop(0, nq, qblock, 0)


def _attn_prompt(slopes, lam_p, qdt, kdb, vdt, g_sub, *, tq, lam_init):
    nb, nh, t, _ = kdb.shape
    assert t % tq == 0 and tq % CHUNK == 0 and nh == DIFF_HEADS
    per_stream = lambda b, *_: (b, 0, 0, 0)
    grid_spec = pltpu.PrefetchScalarGridSpec(
        num_scalar_prefetch=1,
        grid=(nb,),
        in_specs=[pl.BlockSpec((4, DIFF_DH), lambda b, *_: (0, 0)),
                  pl.BlockSpec((1, nh, LANES, t), per_stream),
                  pl.BlockSpec((1, nh, t, LANES), per_stream),
                  pl.BlockSpec((1, nh, V_AUG_ROWS, t), per_stream),
                  pl.BlockSpec((1, DIFF_DV), lambda b, *_: (0, 0))],
        out_specs=pl.BlockSpec((t, nh * DIFF_DV), lambda b, *_: (b, 0)),
        scratch_shapes=[pltpu.VMEM((t, LANES), BF16),
                        pltpu.VMEM((nh, 2 * LANES, 2 * tq), BF16),
                        pltpu.VMEM((nh, tq, tq), F32),
                        pltpu.VMEM((nh, V_AUG_ROWS, 2 * tq), F32),
                        pltpu.VMEM((8, 2 * tq), F32),
                        pltpu.VMEM((S_SLOTS, K_WIDE * tq, 2 * tq), F32)],
    )
    return pl.pallas_call(
        functools.partial(_attn_prompt_kernel, tq=tq, lam_init=lam_init),
        grid_spec=grid_spec,
        out_shape=jax.ShapeDtypeStruct((nb * t, nh * DIFF_DV), BF16),
        compiler_params=pltpu.CompilerParams(
            dimension_semantics=("arbitrary",), vmem_limit_bytes=VMEM_LIMIT_BYTES),
        name="attn_prompt",
    )(slopes, lam_p, qdt, kdb, vdt, g_sub)


def _attn_sample_kernel(slopes_ref, lam_ref, q_ref, ck_ref, cv_ref, kn_ref, vn_ref, gsub_ref, o_ref,
                        *, tq, past, lam_init):
    nh = DIFF_HEADS
    nt = (((1,), (1,)), ((), ()))
    k_pos = lax.broadcasted_iota(jnp.int32, (1, past), 1).astype(F32)
    qi = lax.broadcasted_iota(jnp.int32, (tq, tq), 0)
    kj = lax.broadcasted_iota(jnp.int32, (tq, tq), 1)
    rel_new = (past + qi - jnp.abs(qi - kj)).astype(F32)
    lam = _lambda(lam_ref, lam_init)
    gsub = gsub_ref[...]
    old = [pl.ds(hd, past, stride=nh) for hd in range(nh)]
    new = [pl.ds(hd, tq, stride=nh) for hd in range(nh)]

    def scores(hd):
        slope = slopes_ref[hd]
        qq = _split_halves(q_ref[0, hd])
        s_c = lax.dot_general(qq, ck_ref[0, old[hd], :].astype(BF16), nt,
                              preferred_element_type=F32)
        s_n = lax.dot_general(qq, kn_ref[0, new[hd], :].astype(BF16), nt,
                              preferred_element_type=F32)
        bias_n = slope * rel_new
        return s_c + slope * k_pos, s_n + jnp.concatenate([bias_n, bias_n], axis=0)

    def softmax(s_c, s_n):
        m = jnp.maximum(jnp.max(s_c, axis=-1, keepdims=True), jnp.max(s_n, axis=-1, keepdims=True))
        p_c = jnp.exp(s_c - m)
        p_n = jnp.exp(s_n - m)
        l = jnp.sum(p_c, axis=-1, keepdims=True) + jnp.sum(p_n, axis=-1, keepdims=True)
        return p_c.astype(BF16), p_n.astype(BF16), l

    def output(hd, p_c, p_n, l):
        acc = (jnp.dot(p_c, cv_ref[0, old[hd], :].astype(BF16), preferred_element_type=F32)
               + jnp.dot(p_n, vn_ref[0, new[hd], :].astype(BF16), preferred_element_type=F32))
        o_ref[:, hd * DIFF_DV:(hd + 1) * DIFF_DV] = _finish_attention(
            acc, l, lam, gsub, lam_init, tq).astype(BF16)

    s_next = scores(0)
    for hd in range(nh):
        s_cur, s_next = s_next, (scores(hd + 1) if hd + 1 < nh else None)
        output(hd, *softmax(*s_cur))


def _attn_sample(slopes, lam_p, qd, cache_k, cache_v, k_new, v_new, g_sub, *, lam_init):
    nb, rows, _ = cache_k.shape
    nh = DIFF_HEADS
    past = rows // nh
    tq = k_new.shape[1] // (nb * nh)
    cache_spec = pl.BlockSpec((1, past * nh, LANES), lambda b, *_: (b, 0, 0))
    new_spec = pl.BlockSpec((1, tq * nh, LANES), lambda b, *_: (0, b, 0))
    grid_spec = pltpu.PrefetchScalarGridSpec(
        num_scalar_prefetch=1,
        grid=(nb,),
        in_specs=[pl.BlockSpec((4, DIFF_DH), lambda b, *_: (0, 0)),
                  pl.BlockSpec((1, nh, tq, LANES), lambda b, *_: (0, 0, b, 0)),
                  cache_spec, cache_spec, new_spec, new_spec,
                  pl.BlockSpec((1, DIFF_DV), lambda b, *_: (0, 0))],
        out_specs=pl.BlockSpec((tq, nh * DIFF_DV), lambda b, *_: (b, 0)),
    )
    return pl.pallas_call(
        functools.partial(_attn_sample_kernel, tq=tq, past=past, lam_init=lam_init),
        grid_spec=grid_spec,
        out_shape=jax.ShapeDtypeStruct((nb * tq, nh * DIFF_DV), BF16),
        compiler_params=pltpu.CompilerParams(
            dimension_semantics=("arbitrary",), vmem_limit_bytes=VMEM_LIMIT_BYTES),
        name="attn_sample",
    )(slopes, lam_p, qd, cache_k, cache_v, k_new, v_new, g_sub)


def _out_ffn_kernel(og_ref, od_ref, x_ref, wo_ref, wu_ref, wd_ref,
                    gpm_ref, gpf_ref, gqf_ref, y_ref, *, ff_chunk):
    mix = (jnp.dot(og_ref[...], wo_ref[:W_V, :], preferred_element_type=F32)
           + jnp.dot(od_ref[...], wo_ref[W_V:, :], preferred_element_type=F32))
    x1 = x_ref[...] + _rms(mix, gpm_ref[...])
    f = _rms(x1, gpf_ref[...]).astype(BF16)
    d_ff = wu_ref.shape[1]
    acc = jnp.zeros(x1.shape, F32)
    for c in range(d_ff // ff_chunk):
        sl = slice(c * ff_chunk, (c + 1) * ff_chunk)
        hid = jnp.dot(f, wu_ref[:, sl], preferred_element_type=F32)
        hid = jnp.square(jnp.maximum(hid, 0.0)).astype(BF16)
        acc = acc + jnp.dot(hid, wd_ref[sl, :], preferred_element_type=F32)
    y_ref[...] = x1 + _rms(acc, gqf_ref[...])


def _out_ffn(og, od, x2d, w_out, w_up, w_down, g_post_mix, g_pre_ffn, g_post_ffn, *, tm):
    n, d = x2d.shape
    assert n % tm == 0 and w_up.shape[1] % FF_CHUNK == 0
    row = lambda i: (i, 0)
    return pl.pallas_call(
        functools.partial(_out_ffn_kernel, ff_chunk=FF_CHUNK),
        grid=(n // tm,),
        in_specs=[pl.BlockSpec((tm, W_V), row), pl.BlockSpec((tm, W_V), row),
                  pl.BlockSpec((tm, d), row),
                  _const_spec(w_out.shape), _const_spec(w_up.shape), _const_spec(w_down.shape),
                  _const_spec((1, d)), _const_spec((1, d)), _const_spec((1, d))],
        out_specs=pl.BlockSpec((tm, d), row),
        out_shape=jax.ShapeDtypeStruct((n, d), F32),
        compiler_params=pltpu.CompilerParams(
            dimension_semantics=("arbitrary",), vmem_limit_bytes=VMEM_LIMIT_BYTES),
        name="out_ffn",
    )(og, od, x2d, w_out, w_up, w_down, g_post_mix, g_pre_ffn, g_post_ffn)


def _split_w_in(w):
    a0 = 2 * W_G + 2 * W_V
    a1 = a0 + GLA_RANK
    w_a = jnp.pad(w[:, a0:a1], ((0, 0), (0, LANES - GLA_RANK)))
    return w[:, :a0].astype(BF16), w[:, a1:].astype(BF16), w_a.astype(BF16)


def kernel(x_prompt, x_sample, cache_k, cache_v, state_gla, w_in, w_gate_up, b_gate, g_gla_out,
           lam_q1, lam_k1, lam_q2, lam_k2, g_subln, w_out, g_pre_mix, g_post_mix,
           g_pre_ffn, g_post_ffn, w_ff_up, w_ff_down):
    depth = w_in.shape[0]
    nb_p, t_p, d = x_prompt.shape
    nb_s, t_s, _ = x_sample.shape
    past = cache_k.shape[2]
    slopes = jnp.exp2(-8.0 / DIFF_HEADS * jnp.arange(1, DIFF_HEADS + 1, dtype=F32))
    yp = x_prompt.reshape(nb_p * t_p, d)
    ys = x_sample.reshape(nb_s * t_s, d)
    outs = [[] for _ in range(6)]
    for l in range(depth):
        lam_init = 0.8 - 0.6 * math.exp(-0.3 * l)
        w_parts = _split_w_in(w_in[l])
        wg_pad = jnp.concatenate(
            [w_gate_up[l], jnp.zeros((LANES - GLA_RANK, W_G), F32)], axis=0).astype(BF16)
        bg = b_gate[l][None, :]
        lam_p = jnp.stack([lam_q1[l], lam_k1[l], lam_q2[l], lam_k2[l]], axis=0)
        g_out = g_gla_out[l][None, :]
        g_sub = g_subln[l][None, :]
        wo = w_out[l].astype(BF16)
        wu = w_ff_up[l].astype(BF16)
        wd = w_ff_down[l].astype(BF16)
        gains = (g_post_mix[l][None, :], g_pre_ffn[l][None, :], g_post_ffn[l][None, :])
        g_pre = g_pre_mix[l][None, :]

        qg, kg, vg, rg, gate, qd, kd, kdb, vd, vdb = _in_proj(
            yp, g_pre, w_parts, wg_pad, bg, nb=nb_p, t=t_p, tm=min(IN_PROJ_ROWS, t_p),
            transposed=True)
        s0 = jnp.zeros((nb_p, GLA_HEADS, GLA_DK, GLA_DV), F32)
        og, s_p = _gla(qg, kg, vg, rg, gate, s0, g_out, nb=nb_p, t=t_p, L=CHUNK,
                       CB=GLA_CHUNKS_PER_TILE, carry_state=True)
        od = _attn_prompt(slopes, lam_p, qd, kdb, vdb, g_sub, tq=ATTN_BLOCK, lam_init=lam_init)
        yp = _out_ffn(og, od, yp, wo, wu, wd, *gains, tm=min(OUT_FFN_ROWS, nb_p * t_p))
        outs[0].append(kd.reshape(nb_p, t_p, DIFF_HEADS, 2 * DIFF_DH))
        outs[1].append(vd.reshape(nb_p, t_p, DIFF_HEADS, DIFF_DV))
        outs[2].append(s_p)

        n_s = nb_s * t_s
        qg, kg, vg, rg, gate, qd, kd, _, vd, _ = _in_proj(
            ys, g_pre, w_parts, wg_pad, bg, nb=1, t=n_s, tm=n_s, transposed=False)
        og, s_s = _gla(qg, kg, vg, rg, gate, state_gla[l], g_out, nb=nb_s, t=t_s, L=t_s,
                       CB=math.gcd(nb_s, GLA_CHUNKS_PER_TILE), carry_state=False)
        od = _attn_sample(slopes, lam_p, qd,
                          cache_k[l].reshape(nb_s, past * DIFF_HEADS, 2 * DIFF_DH),
                          cache_v[l].reshape(nb_s, past * DIFF_HEADS, DIFF_DV),
                          kd, vd, g_sub, lam_init=lam_init)
        ys = _out_ffn(og, od, ys, wo, wu, wd, *gains, tm=n_s)
        outs[3].append(kd.reshape(nb_s, t_s, DIFF_HEADS, 2 * DIFF_DH))
        outs[4].append(vd.reshape(nb_s, t_s, DIFF_HEADS, DIFF_DV))
        outs[5].append(s_s)

    stack = lambda xs: jnp.stack(xs, axis=0)
    return (yp.reshape(nb_p, t_p, d), ys.reshape(nb_s, t_s, d),
            stack(outs[0]), stack(outs[1]), stack(outs[2]),
            stack(outs[3]), stack(outs[4]), stack(outs[5]))
```

```python
import functools
import math

import jax
import jax.numpy as jnp
from jax import lax
from jax.experimental import pallas as pl
from jax.experimental.pallas import tpu as pltpu

F32 = jnp.float32
BF16 = jnp.bfloat16

EPS = 1e-6
CHUNK = 64
GLA_HEADS = 4
GLA_DK = 64
GLA_DV = 128
GLA_RANK = 16
GLA_TAU = 16.0
DIFF_HEADS = 4
DIFF_DH = 64
DIFF_DV = 128

W_G = GLA_HEADS * GLA_DK
W_V = GLA_HEADS * GLA_DV
W_D = DIFF_HEADS * 2 * DIFF_DH
LANES = 128
OFF_QG = 0
OFF_KG = OFF_QG + W_G
OFF_VG = OFF_KG + W_G
OFF_RG = OFF_VG + W_V
OFF_QD = 0
OFF_KD = OFF_QD + W_D
OFF_VD = OFF_KD + W_D

BF16_SUBLANES = 16
V_AUG_ROWS = DIFF_DV + BF16_SUBLANES
LOG2E = math.log2(math.e)
K_WIDE = 4
K_UNROLL = 1
SCORE_AHEAD = 2
S_SLOTS = SCORE_AHEAD + 1

VMEM_LIMIT_BYTES = 56 * 1024 * 1024
IN_PROJ_ROWS = 1024
OUT_FFN_ROWS = 1024
FF_CHUNK = 1024
GLA_CHUNKS_PER_TILE = 8
ATTN_BLOCK = 256
GLA_SAFE_LOG_DECAY = -60.0


def _rms(x, g):
    ms = jnp.mean(x * x, axis=-1, keepdims=True)
    return x * lax.rsqrt(ms + EPS) * g


def _log_sigmoid(x):
    return jnp.minimum(x, 0.0) - jnp.log(1.0 + jnp.exp(-jnp.abs(x)))


def _div_pow2(x, d):
    assert d & (d - 1) == 0
    return lax.shift_right_arithmetic(x, d.bit_length() - 1)


def _const_spec(shape):
    zeros = (0,) * len(shape)
    return pl.BlockSpec(shape, lambda *_: zeros, pipeline_mode=pl.Buffered(1))


def _in_proj_kernel(x_ref, g_ref, w_gla_ref, w_diff_ref, w_a_ref, wg_ref, bg_ref,
                    qg_ref, kg_ref, vg_ref, rg_ref, gate_ref,
                    qd_ref, kd_ref, kdb_ref, vd_ref, vdb_ref, *, transposed):
    h = _rms(x_ref[...], g_ref[...]).astype(BF16)

    def proj(w_ref, off, width):
        return jnp.dot(h, w_ref[:, off:off + width], preferred_element_type=F32)

    ag = proj(w_a_ref, 0, LANES).astype(BF16)
    pre = jnp.dot(ag, wg_ref[...], preferred_element_type=F32) + bg_ref[...]
    gate_ref[...] = _log_sigmoid(pre) * (1.0 / GLA_TAU)
    qd = proj(w_diff_ref, OFF_QD, W_D) * (DIFF_DH ** -0.5)
    vd = proj(w_diff_ref, OFF_VD, W_D)
    kd = proj(w_diff_ref, OFF_KD, W_D)
    tm = qd.shape[0]
    for hd in range(DIFF_HEADS):
        head_rows = pl.ds(hd, tm, stride=DIFF_HEADS)
        kd_ref[0, head_rows, :] = kd[:, hd * LANES:(hd + 1) * LANES]
        vd_ref[0, head_rows, :] = vd[:, hd * LANES:(hd + 1) * LANES]
    ones_rows = (lax.broadcasted_iota(jnp.int32, (V_AUG_ROWS - DIFF_DV, tm), 0) == 0).astype(BF16)
    for hd in range(DIFF_HEADS):
        sl = slice(hd * LANES, (hd + 1) * LANES)
        kdb_ref[0, hd] = kd[:, sl].astype(BF16)
        if transposed:
            qd_ref[0, hd] = (qd[:, sl] * LOG2E).T.astype(BF16)
            vdb_ref[0, hd, 0:DIFF_DV, :] = vd[:, sl].T.astype(BF16)
            vdb_ref[0, hd, DIFF_DV:V_AUG_ROWS, :] = ones_rows
        else:
            qd_ref[0, hd] = qd[:, sl].astype(BF16)
            vdb_ref[0, hd] = vd[:, sl].astype(BF16)
    qg_ref[...] = (proj(w_gla_ref, OFF_QG, W_G) * (GLA_DK ** -0.5)).astype(BF16)
    kg_ref[...] = proj(w_gla_ref, OFF_KG, W_G).astype(BF16)
    vg_ref[...] = proj(w_gla_ref, OFF_VG, W_V).astype(BF16)
    rg_ref[...] = proj(w_gla_ref, OFF_RG, W_V).astype(BF16)


def _in_proj(x2d, g_pre, w_parts, wg_pad, b_gate, *, nb, t, tm, transposed):
    n, d = x2d.shape
    assert t % tm == 0 and n == nb * t
    steps_per_b = t // tm
    row = lambda i: (i, 0)
    hm = lambda i: (i // steps_per_b, 0, i % steps_per_b, 0)
    hm_t = lambda i: (i // steps_per_b, 0, 0, i % steps_per_b)
    hm_shape = jax.ShapeDtypeStruct((nb, DIFF_HEADS, t, LANES), BF16)
    hm_spec = pl.BlockSpec((1, DIFF_HEADS, tm, LANES), hm)
    kv_shape = jax.ShapeDtypeStruct((nb, t * DIFF_HEADS, LANES), F32)
    kv_spec = pl.BlockSpec((1, tm * DIFF_HEADS, LANES),
                           lambda i: (i // steps_per_b, i % steps_per_b, 0))
    if transposed:
        q_shape = jax.ShapeDtypeStruct((nb, DIFF_HEADS, LANES, t), BF16)
        q_spec = pl.BlockSpec((1, DIFF_HEADS, LANES, tm), hm_t)
        v_shape = jax.ShapeDtypeStruct((nb, DIFF_HEADS, V_AUG_ROWS, t), BF16)
        v_spec = pl.BlockSpec((1, DIFF_HEADS, V_AUG_ROWS, tm), hm_t)
    else:
        q_shape, q_spec, v_shape, v_spec = hm_shape, hm_spec, hm_shape, hm_spec
    out_shape = (
        jax.ShapeDtypeStruct((n, W_G), BF16), jax.ShapeDtypeStruct((n, W_G), BF16),
        jax.ShapeDtypeStruct((n, W_V), BF16), jax.ShapeDtypeStruct((n, W_V), BF16),
        jax.ShapeDtypeStruct((n, W_G), F32),
        q_shape,
        kv_shape, hm_shape,
        kv_shape, v_shape,
    )
    out_specs = (
        pl.BlockSpec((tm, W_G), row), pl.BlockSpec((tm, W_G), row),
        pl.BlockSpec((tm, W_V), row), pl.BlockSpec((tm, W_V), row),
        pl.BlockSpec((tm, W_G), row),
        q_spec,
        kv_spec, hm_spec,
        kv_spec, v_spec,
    )
    return pl.pallas_call(
        functools.partial(_in_proj_kernel, transposed=transposed),
        grid=(n // tm,),
        in_specs=[pl.BlockSpec((tm, d), row), _const_spec((1, d)),
                  *[_const_spec(w.shape) for w in w_parts],
                  _const_spec(wg_pad.shape), _const_spec((1, W_G))],
        out_specs=out_specs,
        out_shape=out_shape,
        compiler_params=pltpu.CompilerParams(
            dimension_semantics=("arbitrary",), vmem_limit_bytes=VMEM_LIMIT_BYTES),
        name="in_proj",
    )(x2d, g_pre, *w_parts, wg_pad, b_gate)


def _gla_kernel(q_ref, k_ref, v_ref, r_ref, gate_ref, s0_ref, gout_ref,
                o_ref, sout_ref, s_scr, oi_scr, qf_scr, kf_scr, bf_scr, b_scr,
                *, L, CB, carry_state):
    H, DK, DV = GLA_HEADS, GLA_DK, GLA_DV
    HL = H * L
    step = pl.program_id(1)

    if carry_state:
        @pl.when(step == 0)
        def _():
            s_scr[...] = s0_ref[0].reshape(H * DK, DV).T

    lane_head = _div_pow2(lax.broadcasted_iota(jnp.int32, (L, W_G), 1), DK)
    row_i = lax.broadcasted_iota(jnp.int32, (HL, HL), 0)
    col_i = lax.broadcasted_iota(jnp.int32, (HL, HL), 1)
    causal = (col_i <= row_i) & (col_i >= (row_i & ~(L - 1)))
    row_t = lax.broadcasted_iota(jnp.int32, (L, 1), 0)
    nt = (((1,), (1,)), ((), ()))
    tn = (((0,), (0,)), ((), ()))
    gout = gout_ref[...]

    def split_hi_lo(x):
        hi = x.astype(BF16)
        return hi, (x - hi.astype(F32)).astype(BF16)

    tri = jnp.where(lax.broadcasted_iota(jnp.int32, (L, L), 1)
                    <= lax.broadcasted_iota(jnp.int32, (L, L), 0), 1.0, 0.0).astype(BF16)
    totals = []
    for c in range(CB):
        g_c = gate_ref[c * L:(c + 1) * L, :]
        c_hi, c_lo = split_hi_lo(g_c)
        b_scr[c * L:(c + 1) * L, :] = (jnp.dot(tri, c_hi, preferred_element_type=F32)
                                       + jnp.dot(tri, c_lo, preferred_element_type=F32))
        totals.append(jnp.sum(g_c, axis=0, keepdims=True))
    safe = jnp.min(jnp.concatenate(totals, axis=0)) >= GLA_SAFE_LOG_DECAY

    def stack_heads(a):
        return jnp.concatenate(
            [jnp.where(lane_head == hd, a, 0.0) for hd in range(H)], axis=0)

    def prepare(rows):
        b = b_scr[rows, :]
        b_last = b[L - 1:L, :]
        q = q_ref[rows, :].astype(F32)
        k = k_ref[rows, :].astype(F32)
        v = v_ref[rows, :]
        qs = stack_heads(q * jnp.exp(b)).astype(BF16)
        kends = stack_heads(k * jnp.exp(b_last - b)).astype(BF16)
        vs = jnp.concatenate([v[:, hd * DV:(hd + 1) * DV] for hd in range(H)], axis=0)
        kv_t = lax.dot_general(vs, kends, tn, preferred_element_type=F32)
        return dict(rows=rows, q=q, k=k, b=b, b_last=b_last, qs=qs, vs=vs, kv_t=kv_t)

    def intra_scores(p):
        ks = stack_heads(p["k"] * jnp.exp(-p["b"])).astype(BF16)
        return lax.dot_general(p["qs"], ks, nt, preferred_element_type=F32)

    def intra_apply(p, a):
        a = jnp.where(causal, a, 0.0).astype(BF16)
        return jnp.dot(a, p["vs"], preferred_element_type=F32)

    def intra_per_token(p):
        rows = p["rows"]
        qf_scr[...] = p["q"]
        kf_scr[...] = p["k"]
        bf_scr[...] = p["b"]

        def tok(t, carry_t):
            w = (qf_scr[pl.ds(t, 1), :] * kf_scr[...]
                 * jnp.exp(jnp.minimum(bf_scr[pl.ds(t, 1), :] - bf_scr[...], 0.0)))
            for hd in range(H):
                a_col = jnp.sum(jnp.where(lane_head == hd, w, 0.0), axis=-1, keepdims=True)
                a_col = jnp.where(row_t <= t, a_col, 0.0)
                vh = v_ref[rows, hd * DV:(hd + 1) * DV].astype(F32)
                oi_scr[pl.ds(hd * L + t, 1), :] = jnp.sum(a_col * vh, axis=0, keepdims=True)
            return carry_t

        lax.fori_loop(0, L, tok, 0)
        return oi_scr[...]

    def finish(p, o_intra, s_old):
        rows = p["rows"]
        o = (lax.dot_general(p["qs"], s_old.astype(BF16), nt, preferred_element_type=F32)
             + o_intra)
        o = _rms(o, gout)
        r = r_ref[rows, :].astype(F32)
        for hd in range(H):
            rh = r[:, hd * DV:(hd + 1) * DV]
            o_ref[rows, hd * DV:(hd + 1) * DV] = (
                o[hd * L:(hd + 1) * L, :] * (rh * jax.nn.sigmoid(rh))).astype(BF16)
        return jnp.exp(p["b_last"]) * s_old + p["kv_t"]

    def state_in(c):
        return s_scr[...] if carry_state else s0_ref[c].reshape(H * DK, DV).T

    def state_out(c, s):
        if carry_state:
            s_scr[...] = s
        else:
            sout_ref[c] = s.T.reshape(H, DK, DV)

    @pl.when(safe)
    def _():
        prep, raw, o_intra = {}, {}, {}
        s = state_in(0)
        for c in range(CB + 2):
            if c < CB:
                prep[c] = prepare(slice(c * L, (c + 1) * L))
                raw[c] = intra_scores(prep[c])
            if 1 <= c <= CB:
                o_intra[c - 1] = intra_apply(prep[c - 1], raw.pop(c - 1))
            if c >= 2:
                d = c - 2
                if not carry_state and d > 0:
                    s = state_in(d)
                s = finish(prep.pop(d), o_intra.pop(d), s)
                if not carry_state or d == CB - 1:
                    state_out(d, s)

    @pl.when(jnp.logical_not(safe))
    def _():
        def body(c, carry):
            p = prepare(pl.ds(pl.multiple_of(c * L, L), L))
            state_out(c, finish(p, intra_per_token(p), state_in(c)))
            return carry

        lax.fori_loop(0, CB, body, 0)

    if carry_state:
        @pl.when(step == pl.num_programs(1) - 1)
        def _():
            sout_ref[0] = s_scr[...].T.reshape(H, DK, DV)


def _gla(qg, kg, vg, rg, gate, s0, g_out, *, nb, t, L, CB, carry_state):
    n = qg.shape[0]
    tm = L * CB
    if carry_state:
        assert t % tm == 0
        grid = (nb, t // tm)
        steps = t // tm
        row = lambda b, s: (b * steps + s, 0)
        st = lambda b, s: (b, 0, 0, 0)
        state_block = (1, GLA_HEADS, GLA_DK, GLA_DV)
    else:
        assert t == L and nb % CB == 0
        grid = (nb // CB, 1)
        row = lambda b, s: (b, 0)
        st = lambda b, s: (b, 0, 0, 0)
        state_block = (CB, GLA_HEADS, GLA_DK, GLA_DV)
    return pl.pallas_call(
        functools.partial(_gla_kernel, L=L, CB=CB, carry_state=carry_state),
        grid=grid,
        in_specs=[pl.BlockSpec((tm, W_G), row), pl.BlockSpec((tm, W_G), row),
                  pl.BlockSpec((tm, W_V), row), pl.BlockSpec((tm, W_V), row),
                  pl.BlockSpec((tm, W_G), row), pl.BlockSpec(state_block, st),
                  pl.BlockSpec((1, GLA_DV), lambda b, s: (0, 0))],
        out_specs=(pl.BlockSpec((tm, W_V), row), pl.BlockSpec(state_block, st)),
        out_shape=(jax.ShapeDtypeStruct((n, W_V), BF16),
                   jax.ShapeDtypeStruct((nb, GLA_HEADS, GLA_DK, GLA_DV), F32)),
        scratch_shapes=[pltpu.VMEM((GLA_DV, GLA_HEADS * GLA_DK), F32),
                        pltpu.VMEM((GLA_HEADS * L, GLA_DV), F32),
                        pltpu.VMEM((L, W_G), F32), pltpu.VMEM((L, W_G), F32),
                        pltpu.VMEM((L, W_G), F32), pltpu.VMEM((tm, W_G), F32)],
        compiler_params=pltpu.CompilerParams(
            dimension_semantics=("arbitrary", "arbitrary"), vmem_limit_bytes=VMEM_LIMIT_BYTES),
        name="gla",
    )(qg, kg, vg, rg, gate, s0, g_out)


def _lambda(lam_ref, lam_init):
    lp = lam_ref[...]
    s1 = jnp.sum(lp[0:1, :] * lp[1:2, :], axis=-1, keepdims=True)
    s2 = jnp.sum(lp[2:3, :] * lp[3:4, :], axis=-1, keepdims=True)
    return jnp.exp(s1) - jnp.exp(s2) + lam_init


def _split_halves(q):
    lane = lax.broadcasted_iota(jnp.int32, q.shape, 1)
    zero = jnp.zeros_like(q)
    return jnp.concatenate(
        [jnp.where(lane < DIFF_DH, q, zero), jnp.where(lane >= DIFF_DH, q, zero)], axis=0)


def _finish_attention(acc, l, lam, gsub, lam_init, tq):
    o = acc[:tq] / l[:tq] - lam * (acc[tq:] / l[tq:])
    return _rms(o, gsub) * (1.0 - lam_init)


def _split3_bf16(x):
    hi = x.astype(BF16).astype(F32)
    r = x - hi
    mid = r.astype(BF16).astype(F32)
    lo = (r - mid).astype(BF16).astype(F32)
    return hi, mid, lo


def _attn_prompt_kernel(slopes_ref, lam_ref, qt_ref, k_ref, vt_ref, gsub_ref, o_ref,
                        aug_scr, qt_scr, corr_scr, acc_scr, m_scr, s_scr, *, tq, lam_init):
    nh = DIFF_HEADS
    tk = tq
    t = k_ref.shape[2]
    nq = t // tq
    slope2 = [slopes_ref[hd] * LOG2E for hd in range(nh)]
    j_lane = 3 * nh

    lane = lax.broadcasted_iota(jnp.int32, (tk, LANES), 1)
    k_loc = lax.broadcasted_iota(jnp.int32, (tk, LANES), 0).astype(F32)
    base = jnp.zeros((tk, LANES), F32)
    for hd in range(nh):
        for i, part in enumerate(_split3_bf16(slope2[hd] * k_loc)):
            base = jnp.where(lane == 3 * hd + i, part, base)
    blk_lanes = (lane >= j_lane) & (lane < j_lane + 3)
    for j in range(t // tk):
        aug_scr[j * tk:(j + 1) * tk, :] = jnp.where(blk_lanes, float(j), base).astype(BF16)

    row = lax.broadcasted_iota(jnp.int32, (LANES, 2 * tq), 0)
    kl = lax.broadcasted_iota(jnp.int32, (tk, tq), 0)
    ql = lax.broadcasted_iota(jnp.int32, (tk, tq), 1)
    same_chunk_or_earlier = _div_pow2(kl, CHUNK) <= _div_pow2(ql, CHUNK)
    ahead = jnp.maximum(kl - ql, 0).astype(F32)
    for hd in range(nh):
        rows = jnp.where((row >= 3 * hd) & (row < 3 * hd + 3), 1.0, 0.0)
        for i, part in enumerate(_split3_bf16(jnp.full((LANES, 2 * tq), slope2[hd] * tk, F32))):
            rows = jnp.where(row == j_lane + i, part, rows)
        qt_scr[hd, LANES:2 * LANES, :] = rows.astype(BF16)
        qt_scr[hd, DIFF_DH:2 * DIFF_DH, 0:tq] = jnp.zeros((DIFF_DH, tq), BF16)
        qt_scr[hd, 0:DIFF_DH, tq:2 * tq] = jnp.zeros((DIFF_DH, tq), BF16)
        corr_scr[hd] = jnp.where(same_chunk_or_earlier, (-2.0 * slope2[hd]) * ahead, -jnp.inf)

    lam = _lambda(lam_ref, lam_init)
    gsub = gsub_ref[...]

    def ksteps(steps, write_output=None):
        items = [(j0, nblk, diag, hd) for j0, nblk, diag in steps for hd in range(nh)]

        def key_rows(j0, nblk):
            return pl.ds(pl.multiple_of(j0 * tk, tk), nblk * tk)

        def scores(i):
            j0, nblk, diag, hd = items[i]
            ks = key_rows(j0, nblk)
            k_aug = jnp.concatenate([k_ref[0, hd, ks, :], aug_scr[ks, :]], axis=1)
            s_scr[i % S_SLOTS, 0:nblk * tk, :] = jnp.dot(
                k_aug, qt_scr[hd], preferred_element_type=F32)

        def softmax(i):
            j0, nblk, diag, hd = items[i]
            p_parts, alpha_parts = [], []
            for g in range(2 * tq // LANES):
                lanes = slice(g * LANES, (g + 1) * LANES)
                sg = s_scr[i % S_SLOTS, 0:nblk * tk, lanes]
                if diag:
                    c0 = (g * LANES) % tq
                    sg = sg + corr_scr[hd, :, c0:c0 + LANES]
                m_old = m_scr[hd:hd + 1, lanes]
                m_new = jnp.maximum(m_old, jnp.max(sg, axis=0, keepdims=True))
                m_scr[hd:hd + 1, lanes] = m_new
                p_parts.append(jnp.exp2(sg - m_new).astype(BF16))
                alpha_parts.append(jnp.exp2(m_old - m_new))
            return jnp.concatenate(p_parts, axis=1), jnp.concatenate(alpha_parts, axis=1)

        def accumulate(i, p, alpha):
            j0, nblk, diag, hd = items[i]
            acc_scr[hd] = alpha * acc_scr[hd] + jnp.dot(
                vt_ref[0, hd, :, key_rows(j0, nblk)], p, preferred_element_type=F32)

        def retire(i, p, alpha):
            accumulate(i, p, alpha)
            if items[i][2]:
                write_output(items[i][3])

        n = len(items)
        for i in range(min(SCORE_AHEAD, n)):
            scores(i)
        pending = None
        for i in range(n):
            if i + SCORE_AHEAD < n:
                scores(i + SCORE_AHEAD)
            p_alpha = softmax(i)
            if pending is not None:
                retire(i - 1, *pending)
            pending = p_alpha
        retire(n - 1, *pending)

    def qblock(qi, carry):
        qs = pl.ds(pl.multiple_of(qi * tq, tq), tq)
        for hd in range(nh):
            qt = qt_ref[0, hd, :, qs]
            qt_scr[hd, 0:DIFF_DH, 0:tq] = qt[0:DIFF_DH, :]
            qt_scr[hd, DIFF_DH:2 * DIFF_DH, tq:2 * tq] = qt[DIFF_DH:2 * DIFF_DH, :]
        m_scr[...] = jnp.full(m_scr.shape, -jnp.inf, F32)
        acc_scr[...] = jnp.zeros(acc_scr.shape, F32)

        def write_output(hd):
            acc = acc_scr[hd]
            inv_l = 1.0 / acc[DIFF_DV:DIFF_DV + 1, :]
            o_t = (acc[0:DIFF_DV, 0:tq] * inv_l[:, 0:tq]
                   - lam * (acc[0:DIFF_DV, tq:2 * tq] * inv_l[:, tq:2 * tq]))
            ms = jnp.mean(o_t * o_t, axis=0, keepdims=True)
            o_t = o_t * lax.rsqrt(ms + EPS)
            o_ref[qs, hd * DIFF_DV:(hd + 1) * DIFF_DV] = (
                o_t.T * gsub * (1.0 - lam_init)).astype(BF16)

        per_trip = K_UNROLL * K_WIDE

        def off_diag(jj, c):
            ksteps([(jj * per_trip + u * K_WIDE, K_WIDE, False) for u in range(K_UNROLL)])
            return c

        lax.fori_loop(0, qi // per_trip, off_diag, 0)
        done = (qi // per_trip) * per_trip
        for left in range(per_trip):
            @pl.when(qi - done == left)
            def _(left=left):
                steps, off, size = [], 0, per_trip // 2
                while size >= 1:
                    if left & size:
                        nblk = min(size, K_WIDE)
                        steps += [(done + off + u * nblk, nblk, False) for u in range(size // nblk)]
                        off += size
                    size //= 2
                ksteps(steps + [(qi, 1, True)], write_output)

        return carry

    lax.fori_loop(0, nq, qblock, 0)


def _attn_prompt(slopes, lam_p, qdt, kdb, vdt, g_sub, *, tq, lam_init):
    nb, nh, t, _ = kdb.shape
    assert t % tq == 0 and tq % CHUNK == 0 and nh == DIFF_HEADS
    per_stream = lambda b, *_: (b, 0, 0, 0)
    grid_spec = pltpu.PrefetchScalarGridSpec(
        num_scalar_prefetch=1,
        grid=(nb,),
        in_specs=[pl.BlockSpec((4, DIFF_DH), lambda b, *_: (0, 0)),
                  pl.BlockSpec((1, nh, LANES, t), per_stream),
                  pl.BlockSpec((1, nh, t, LANES), per_stream),
                  pl.BlockSpec((1, nh, V_AUG_ROWS, t), per_stream),
                  pl.BlockSpec((1, DIFF_DV), lambda b, *_: (0, 0))],
        out_specs=pl.BlockSpec((t, nh * DIFF_DV), lambda b, *_: (b, 0)),
        scratch_shapes=[pltpu.VMEM((t, LANES), BF16),
                        pltpu.VMEM((nh, 2 * LANES, 2 * tq), BF16),
                        pltpu.VMEM((nh, tq, tq), F32),
                        pltpu.VMEM((nh, V_AUG_ROWS, 2 * tq), F32),
                        pltpu.VMEM((8, 2 * tq), F32),
                        pltpu.VMEM((S_SLOTS, K_WIDE * tq, 2 * tq), F32)],
    )
    return pl.pallas_call(
        functools.partial(_attn_prompt_kernel, tq=tq, lam_init=lam_init),
        grid_spec=grid_spec,
        out_shape=jax.ShapeDtypeStruct((nb * t, nh * DIFF_DV), BF16),
        compiler_params=pltpu.CompilerParams(
            dimension_semantics=("arbitrary",), vmem_limit_bytes=VMEM_LIMIT_BYTES),
        name="attn_prompt",
    )(slopes, lam_p, qdt, kdb, vdt, g_sub)


def _attn_sample_kernel(slopes_ref, lam_ref, q_ref, ck_ref, cv_ref, kn_ref, vn_ref, gsub_ref, o_ref,
                        *, tq, past, lam_init):
    nh = DIFF_HEADS
    nt = (((1,), (1,)), ((), ()))
    k_pos = lax.broadcasted_iota(jnp.int32, (1, past), 1).astype(F32)
    qi = lax.broadcasted_iota(jnp.int32, (tq, tq), 0)
    kj = lax.broadcasted_iota(jnp.int32, (tq, tq), 1)
    rel_new = (past + qi - jnp.abs(qi - kj)).astype(F32)
    lam = _lambda(lam_ref, lam_init)
    gsub = gsub_ref[...]
    old = [pl.ds(hd, past, stride=nh) for hd in range(nh)]
    new = [pl.ds(hd, tq, stride=nh) for hd in range(nh)]

    def scores(hd):
        slope = slopes_ref[hd]
        qq = _split_halves(q_ref[0, hd])
        s_c = lax.dot_general(qq, ck_ref[0, old[hd], :].astype(BF16), nt,
                              preferred_element_type=F32)
        s_n = lax.dot_general(qq, kn_ref[0, new[hd], :].astype(BF16), nt,
                              preferred_element_type=F32)
        bias_n = slope * rel_new
        return s_c + slope * k_pos, s_n + jnp.concatenate([bias_n, bias_n], axis=0)

    def softmax(s_c, s_n):
        m = jnp.maximum(jnp.max(s_c, axis=-1, keepdims=True), jnp.max(s_n, axis=-1, keepdims=True))
        p_c = jnp.exp(s_c - m)
        p_n = jnp.exp(s_n - m)
        l = jnp.sum(p_c, axis=-1, keepdims=True) + jnp.sum(p_n, axis=-1, keepdims=True)
        return p_c.astype(BF16), p_n.astype(BF16), l

    def output(hd, p_c, p_n, l):
        acc = (jnp.dot(p_c, cv_ref[0, old[hd], :].astype(BF16), preferred_element_type=F32)
               + jnp.dot(p_n, vn_ref[0, new[hd], :].astype(BF16), preferred_element_type=F32))
        o_ref[:, hd * DIFF_DV:(hd + 1) * DIFF_DV] = _finish_attention(
            acc, l, lam, gsub, lam_init, tq).astype(BF16)

    s_next = scores(0)
    for hd in range(nh):
        s_cur, s_next = s_next, (scores(hd + 1) if hd + 1 < nh else None)
        output(hd, *softmax(*s_cur))


def _attn_sample(slopes, lam_p, qd, cache_k, cache_v, k_new, v_new, g_sub, *, lam_init):
    nb, rows, _ = cache_k.shape
    nh = DIFF_HEADS
    past = rows // nh
    tq = k_new.shape[1] // (nb * nh)
    cache_spec = pl.BlockSpec((1, past * nh, LANES), lambda b, *_: (b, 0, 0))
    new_spec = pl.BlockSpec((1, tq * nh, LANES), lambda b, *_: (0, b, 0))
    grid_spec = pltpu.PrefetchScalarGridSpec(
        num_scalar_prefetch=1,
        grid=(nb,),
        in_specs=[pl.BlockSpec((4, DIFF_DH), lambda b, *_: (0, 0)),
                  pl.BlockSpec((1, nh, tq, LANES), lambda b, *_: (0, 0, b, 0)),
                  cache_spec, cache_spec, new_spec, new_spec,
                  pl.BlockSpec((1, DIFF_DV), lambda b, *_: (0, 0))],
        out_specs=pl.BlockSpec((tq, nh * DIFF_DV), lambda b, *_: (b, 0)),
    )
    return pl.pallas_call(
        functools.partial(_attn_sample_kernel, tq=tq, past=past, lam_init=lam_init),
        grid_spec=grid_spec,
        out_shape=jax.ShapeDtypeStruct((nb * tq, nh * DIFF_DV), BF16),
        compiler_params=pltpu.CompilerParams(
            dimension_semantics=("arbitrary",), vmem_limit_bytes=VMEM_LIMIT_BYTES),
        name="attn_sample",
    )(slopes, lam_p, qd, cache_k, cache_v, k_new, v_new, g_sub)


def _out_ffn_kernel(og_ref, od_ref, x_ref, wo_ref, wu_ref, wd_ref,
                    gpm_ref, gpf_ref, gqf_ref, y_ref, *, ff_chunk):
    mix = (jnp.dot(og_ref[...], wo_ref[:W_V, :], preferred_element_type=F32)
           + jnp.dot(od_ref[...], wo_ref[W_V:, :], preferred_element_type=F32))
    x1 = x_ref[...] + _rms(mix, gpm_ref[...])
    f = _rms(x1, gpf_ref[...]).astype(BF16)
    d_ff = wu_ref.shape[1]
    acc = jnp.zeros(x1.shape, F32)
    for c in range(d_ff // ff_chunk):
        sl = slice(c * ff_chunk, (c + 1) * ff_chunk)
        hid = jnp.dot(f, wu_ref[:, sl], preferred_element_type=F32)
        hid = jnp.square(jnp.maximum(hid, 0.0)).astype(BF16)
        acc = acc + jnp.dot(hid, wd_ref[sl, :], preferred_element_type=F32)
    y_ref[...] = x1 + _rms(acc, gqf_ref[...])


def _out_ffn(og, od, x2d, w_out, w_up, w_down, g_post_mix, g_pre_ffn, g_post_ffn, *, tm):
    n, d = x2d.shape
    assert n % tm == 0 and w_up.shape[1] % FF_CHUNK == 0
    row = lambda i: (i, 0)
    return pl.pallas_call(
        functools.partial(_out_ffn_kernel, ff_chunk=FF_CHUNK),
        grid=(n // tm,),
        in_specs=[pl.BlockSpec((tm, W_V), row), pl.BlockSpec((tm, W_V), row),
                  pl.BlockSpec((tm, d), row),
                  _const_spec(w_out.shape), _const_spec(w_up.shape), _const_spec(w_down.shape),
                  _const_spec((1, d)), _const_spec((1, d)), _const_spec((1, d))],
        out_specs=pl.BlockSpec((tm, d), row),
        out_shape=jax.ShapeDtypeStruct((n, d), F32),
        compiler_params=pltpu.CompilerParams(
            dimension_semantics=("arbitrary",), vmem_limit_bytes=VMEM_LIMIT_BYTES),
        name="out_ffn",
    )(og, od, x2d, w_out, w_up, w_down, g_post_mix, g_pre_ffn, g_post_ffn)


def _split_w_in(w):
    a0 = 2 * W_G + 2 * W_V
    a1 = a0 + GLA_RANK
    w_a = jnp.pad(w[:, a0:a1], ((0, 0), (0, LANES - GLA_RANK)))
    return w[:, :a0].astype(BF16), w[:, a1:].astype(BF16), w_a.astype(BF16)


def kernel(x_prompt, x_sample, cache_k, cache_v, state_gla, w_in, w_gate_up, b_gate, g_gla_out,
           lam_q1, lam_k1, lam_q2, lam_k2, g_subln, w_out, g_pre_mix, g_post_mix,
           g_pre_ffn, g_post_ffn, w_ff_up, w_ff_down):
    depth = w_in.shape[0]
    nb_p, t_p, d = x_prompt.shape
    nb_s, t_s, _ = x_sample.shape
    past = cache_k.shape[2]
    slopes = jnp.exp2(-8.0 / DIFF_HEADS * jnp.arange(1, DIFF_HEADS + 1, dtype=F32))
    yp = x_prompt.reshape(nb_p * t_p, d)
    ys = x_sample.reshape(nb_s * t_s, d)
    outs = [[] for _ in range(6)]
    for l in range(depth):
        lam_init = 0.8 - 0.6 * math.exp(-0.3 * l)
        w_parts = _split_w_in(w_in[l])
        wg_pad = jnp.concatenate(
            [w_gate_up[l], jnp.zeros((LANES - GLA_RANK, W_G), F32)], axis=0).astype(BF16)
        bg = b_gate[l][None, :]
        lam_p = jnp.stack([lam_q1[l], lam_k1[l], lam_q2[l], lam_k2[l]], axis=0)
        g_out = g_gla_out[l][None, :]
        g_sub = g_subln[l][None, :]
        wo = w_out[l].astype(BF16)
        wu = w_ff_up[l].astype(BF16)
        wd = w_ff_down[l].astype(BF16)
        gains = (g_post_mix[l][None, :], g_pre_ffn[l][None, :], g_post_ffn[l][None, :])
        g_pre = g_pre_mix[l][None, :]

        qg, kg, vg, rg, gate, qd, kd, kdb, vd, vdb = _in_proj(
            yp, g_pre, w_parts, wg_pad, bg, nb=nb_p, t=t_p, tm=min(IN_PROJ_ROWS, t_p),
            transposed=True)
        s0 = jnp.zeros((nb_p, GLA_HEADS, GLA_DK, GLA_DV), F32)
        og, s_p = _gla(qg, kg, vg, rg, gate, s0, g_out, nb=nb_p, t=t_p, L=CHUNK,
                       CB=GLA_CHUNKS_PER_TILE, carry_state=True)
        od = _attn_prompt(slopes, lam_p, qd, kdb, vdb, g_sub, tq=ATTN_BLOCK, lam_init=lam_init)
        yp = _out_ffn(og, od, yp, wo, wu, wd, *gains, tm=min(OUT_FFN_ROWS, nb_p * t_p))
        outs[0].append(kd.reshape(nb_p, t_p, DIFF_HEADS, 2 * DIFF_DH))
        outs[1].append(vd.reshape(nb_p, t_p, DIFF_HEADS, DIFF_DV))
        outs[2].append(s_p)

        n_s = nb_s * t_s
        qg, kg, vg, rg, gate, qd, kd, _, vd, _ = _in_proj(
            ys, g_pre, w_parts, wg_pad, bg, nb=1, t=n_s, tm=n_s, transposed=False)
        og, s_s = _gla(qg, kg, vg, rg, gate, state_gla[l], g_out, nb=nb_s, t=t_s, L=t_s,
                       CB=math.gcd(nb_s, GLA_CHUNKS_PER_TILE), carry_state=False)
        od = _attn_sample(slopes, lam_p, qd,
                          cache_k[l].reshape(nb_s, past * DIFF_HEADS, 2 * DIFF_DH),
                          cache_v[l].reshape(nb_s, past * DIFF_HEADS, DIFF_DV),
                          kd, vd, g_sub, lam_init=lam_init)
        ys = _out_ffn(og, od, ys, wo, wu, wd, *gains, tm=n_s)
        outs[3].append(kd.reshape(nb_s, t_s, DIFF_HEADS, 2 * DIFF_DH))
        outs[4].append(vd.reshape(nb_s, t_s, DIFF_HEADS, DIFF_DV))
        outs[5].append(s_s)

    stack = lambda xs: jnp.stack(xs, axis=0)
    return (yp.reshape(nb_p, t_p, d), ys.reshape(nb_s, t_s, d),
            stack(outs[0]), stack(outs[1]), stack(outs[2]),
            stack(outs[3]), stack(outs[4]), stack(outs[5]))
```

```python
import functools
import math

import jax
import jax.numpy as jnp
from jax import lax
from jax.experimental import pallas as pl
from jax.experimental.pallas import tpu as pltpu

F32 = jnp.float32
BF16 = jnp.bfloat16

EPS = 1e-6
CHUNK = 64
GLA_HEADS = 4
GLA_DK = 64
GLA_DV = 128
GLA_RANK = 16
GLA_TAU = 16.0
DIFF_HEADS = 4
DIFF_DH = 64
DIFF_DV = 128

W_G = GLA_HEADS * GLA_DK
W_V = GLA_HEADS * GLA_DV
W_D = DIFF_HEADS * 2 * DIFF_DH
LANES = 128
OFF_QG = 0
OFF_KG = OFF_QG + W_G
OFF_VG = OFF_KG + W_G
OFF_RG = OFF_VG + W_V
OFF_QD = 0
OFF_KD = OFF_QD + W_D
OFF_VD = OFF_KD + W_D

BF16_SUBLANES = 16
V_AUG_ROWS = DIFF_DV + BF16_SUBLANES
LOG2E = math.log2(math.e)
K_WIDE = 4
K_UNROLL = 1
SCORE_AHEAD = 2
S_SLOTS = SCORE_AHEAD + 1

VMEM_LIMIT_BYTES = 56 * 1024 * 1024
IN_PROJ_ROWS = 1024
OUT_FFN_ROWS = 1024
FF_CHUNK = 1024
GLA_CHUNKS_PER_TILE = 16
ATTN_BLOCK = 256
GLA_SAFE_LOG_DECAY = -60.0


def _rms(x, g):
    ms = jnp.mean(x * x, axis=-1, keepdims=True)
    return x * lax.rsqrt(ms + EPS) * g


def _log_sigmoid(x):
    return jnp.minimum(x, 0.0) - jnp.log(1.0 + jnp.exp(-jnp.abs(x)))


def _div_pow2(x, d):
    assert d & (d - 1) == 0
    return lax.shift_right_arithmetic(x, d.bit_length() - 1)


def _const_spec(shape):
    zeros = (0,) * len(shape)
    return pl.BlockSpec(shape, lambda *_: zeros, pipeline_mode=pl.Buffered(1))


def _in_proj_kernel(x_ref, g_ref, w_gla_ref, w_diff_ref, w_a_ref, wg_ref, bg_ref,
                    qg_ref, kg_ref, vg_ref, rg_ref, gate_ref,
                    qd_ref, kd_ref, kdb_ref, vd_ref, vdb_ref, *, transposed):
    h = _rms(x_ref[...], g_ref[...]).astype(BF16)

    def proj(w_ref, off, width):
        return jnp.dot(h, w_ref[:, off:off + width], preferred_element_type=F32)

    ag = proj(w_a_ref, 0, LANES).astype(BF16)
    pre = jnp.dot(ag, wg_ref[...], preferred_element_type=F32) + bg_ref[...]
    gate_ref[...] = _log_sigmoid(pre) * (1.0 / GLA_TAU)
    qd = proj(w_diff_ref, OFF_QD, W_D) * (DIFF_DH ** -0.5)
    vd = proj(w_diff_ref, OFF_VD, W_D)
    kd = proj(w_diff_ref, OFF_KD, W_D)
    tm = qd.shape[0]
    for hd in range(DIFF_HEADS):
        head_rows = pl.ds(hd, tm, stride=DIFF_HEADS)
        kd_ref[0, head_rows, :] = kd[:, hd * LANES:(hd + 1) * LANES]
        vd_ref[0, head_rows, :] = vd[:, hd * LANES:(hd + 1) * LANES]
    ones_rows = (lax.broadcasted_iota(jnp.int32, (V_AUG_ROWS - DIFF_DV, tm), 0) == 0).astype(BF16)
    for hd in range(DIFF_HEADS):
        sl = slice(hd * LANES, (hd + 1) * LANES)
        kdb_ref[0, hd] = kd[:, sl].astype(BF16)
        if transposed:
            qd_ref[0, hd] = (qd[:, sl] * LOG2E).T.astype(BF16)
            vdb_ref[0, hd, 0:DIFF_DV, :] = vd[:, sl].T.astype(BF16)
            vdb_ref[0, hd, DIFF_DV:V_AUG_ROWS, :] = ones_rows
        else:
            qd_ref[0, hd] = qd[:, sl].astype(BF16)
            vdb_ref[0, hd] = vd[:, sl].astype(BF16)
    qg_ref[...] = (proj(w_gla_ref, OFF_QG, W_G) * (GLA_DK ** -0.5)).astype(BF16)
    kg_ref[...] = proj(w_gla_ref, OFF_KG, W_G).astype(BF16)
    vg_ref[...] = proj(w_gla_ref, OFF_VG, W_V).astype(BF16)
    rg_ref[...] = proj(w_gla_ref, OFF_RG, W_V).astype(BF16)


def _in_proj(x2d, g_pre, w_parts, wg_pad, b_gate, *, nb, t, tm, transposed):
    n, d = x2d.shape
    assert t % tm == 0 and n == nb * t
    steps_per_b = t // tm
    row = lambda i: (i, 0)
    hm = lambda i: (i // steps_per_b, 0, i % steps_per_b, 0)
    hm_t = lambda i: (i // steps_per_b, 0, 0, i % steps_per_b)
    hm_shape = jax.ShapeDtypeStruct((nb, DIFF_HEADS, t, LANES), BF16)
    hm_spec = pl.BlockSpec((1, DIFF_HEADS, tm, LANES), hm)
    kv_shape = jax.ShapeDtypeStruct((nb, t * DIFF_HEADS, LANES), F32)
    kv_spec = pl.BlockSpec((1, tm * DIFF_HEADS, LANES),
                           lambda i: (i // steps_per_b, i % steps_per_b, 0))
    if transposed:
        q_shape = jax.ShapeDtypeStruct((nb, DIFF_HEADS, LANES, t), BF16)
        q_spec = pl.BlockSpec((1, DIFF_HEADS, LANES, tm), hm_t)
        v_shape = jax.ShapeDtypeStruct((nb, DIFF_HEADS, V_AUG_ROWS, t), BF16)
        v_spec = pl.BlockSpec((1, DIFF_HEADS, V_AUG_ROWS, tm), hm_t)
    else:
        q_shape, q_spec, v_shape, v_spec = hm_shape, hm_spec, hm_shape, hm_spec
    out_shape = (
        jax.ShapeDtypeStruct((n, W_G), BF16), jax.ShapeDtypeStruct((n, W_G), BF16),
        jax.ShapeDtypeStruct((n, W_V), BF16), jax.ShapeDtypeStruct((n, W_V), BF16),
        jax.ShapeDtypeStruct((n, W_G), F32),
        q_shape,
        kv_shape, hm_shape,
        kv_shape, v_shape,
    )
    out_specs = (
        pl.BlockSpec((tm, W_G), row), pl.BlockSpec((tm, W_G), row),
        pl.BlockSpec((tm, W_V), row), pl.BlockSpec((tm, W_V), row),
        pl.BlockSpec((tm, W_G), row),
        q_spec,
        kv_spec, hm_spec,
        kv_spec, v_spec,
    )
    return pl.pallas_call(
        functools.partial(_in_proj_kernel, transposed=transposed),
        grid=(n // tm,),
        in_specs=[pl.BlockSpec((tm, d), row), _const_spec((1, d)),
                  *[_const_spec(w.shape) for w in w_parts],
                  _const_spec(wg_pad.shape), _const_spec((1, W_G))],
        out_specs=out_specs,
        out_shape=out_shape,
        compiler_params=pltpu.CompilerParams(
            dimension_semantics=("arbitrary",), vmem_limit_bytes=VMEM_LIMIT_BYTES),
        name="in_proj",
    )(x2d, g_pre, *w_parts, wg_pad, b_gate)


def _gla_kernel(q_ref, k_ref, v_ref, r_ref, gate_ref, s0_ref, gout_ref,
                o_ref, sout_ref, s_scr, oi_scr, qf_scr, kf_scr, bf_scr, b_scr,
                *, L, CB, carry_state):
    H, DK, DV = GLA_HEADS, GLA_DK, GLA_DV
    HL = H * L
    step = pl.program_id(1)

    if carry_state:
        @pl.when(step == 0)
        def _():
            s_scr[...] = s0_ref[0].reshape(H * DK, DV).T

    lane_head = _div_pow2(lax.broadcasted_iota(jnp.int32, (L, W_G), 1), DK)
    row_i = lax.broadcasted_iota(jnp.int32, (HL, HL), 0)
    col_i = lax.broadcasted_iota(jnp.int32, (HL, HL), 1)
    causal = (col_i <= row_i) & (col_i >= (row_i & ~(L - 1)))
    row_t = lax.broadcasted_iota(jnp.int32, (L, 1), 0)
    nt = (((1,), (1,)), ((), ()))
    tn = (((0,), (0,)), ((), ()))
    gout = gout_ref[...]

    def split_hi_lo(x):
        hi = x.astype(BF16)
        return hi, (x - hi.astype(F32)).astype(BF16)

    tri = jnp.where(lax.broadcasted_iota(jnp.int32, (L, L), 1)
                    <= lax.broadcasted_iota(jnp.int32, (L, L), 0), 1.0, 0.0).astype(BF16)
    totals = []
    for c in range(CB):
        g_c = gate_ref[c * L:(c + 1) * L, :]
        c_hi, c_lo = split_hi_lo(g_c)
        b_scr[c * L:(c + 1) * L, :] = (jnp.dot(tri, c_hi, preferred_element_type=F32)
                                       + jnp.dot(tri, c_lo, preferred_element_type=F32))
        totals.append(jnp.sum(g_c, axis=0, keepdims=True))
    safe = jnp.min(jnp.concatenate(totals, axis=0)) >= GLA_SAFE_LOG_DECAY

    def stack_heads(a):
        return jnp.concatenate(
            [jnp.where(lane_head == hd, a, 0.0) for hd in range(H)], axis=0)

    def prepare(rows):
        b = b_scr[rows, :]
        b_last = b[L - 1:L, :]
        q = q_ref[rows, :].astype(F32)
        k = k_ref[rows, :].astype(F32)
        v = v_ref[rows, :]
        qs = stack_heads(q * jnp.exp(b)).astype(BF16)
        kends = stack_heads(k * jnp.exp(b_last - b)).astype(BF16)
        vs = jnp.concatenate([v[:, hd * DV:(hd + 1) * DV] for hd in range(H)], axis=0)
        kv_t = lax.dot_general(vs, kends, tn, preferred_element_type=F32)
        return dict(rows=rows, q=q, k=k, b=b, b_last=b_last, qs=qs, vs=vs, kv_t=kv_t)

    def intra_scores(p):
        ks = stack_heads(p["k"] * jnp.exp(-p["b"])).astype(BF16)
        return lax.dot_general(p["qs"], ks, nt, preferred_element_type=F32)

    def intra_apply(p, a):
        a = jnp.where(causal, a, 0.0).astype(BF16)
        return jnp.dot(a, p["vs"], preferred_element_type=F32)

    def intra_per_token(p):
        rows = p["rows"]
        qf_scr[...] = p["q"]
        kf_scr[...] = p["k"]
        bf_scr[...] = p["b"]

        def tok(t, carry_t):
            w = (qf_scr[pl.ds(t, 1), :] * kf_scr[...]
                 * jnp.exp(jnp.minimum(bf_scr[pl.ds(t, 1), :] - bf_scr[...], 0.0)))
            for hd in range(H):
                a_col = jnp.sum(jnp.where(lane_head == hd, w, 0.0), axis=-1, keepdims=True)
                a_col = jnp.where(row_t <= t, a_col, 0.0)
                vh = v_ref[rows, hd * DV:(hd + 1) * DV].astype(F32)
                oi_scr[pl.ds(hd * L + t, 1), :] = jnp.sum(a_col * vh, axis=0, keepdims=True)
            return carry_t

        lax.fori_loop(0, L, tok, 0)
        return oi_scr[...]

    def finish(p, o_intra, s_old):
        rows = p["rows"]
        o = (lax.dot_general(p["qs"], s_old.astype(BF16), nt, preferred_element_type=F32)
             + o_intra)
        o = _rms(o, gout)
        r = r_ref[rows, :].astype(F32)
        for hd in range(H):
            rh = r[:, hd * DV:(hd + 1) * DV]
            o_ref[rows, hd * DV:(hd + 1) * DV] = (
                o[hd * L:(hd + 1) * L, :] * (rh * jax.nn.sigmoid(rh))).astype(BF16)
        return jnp.exp(p["b_last"]) * s_old + p["kv_t"]

    def state_in(c):
        return s_scr[...] if carry_state else s0_ref[c].reshape(H * DK, DV).T

    def state_out(c, s):
        if carry_state:
            s_scr[...] = s
        else:
            sout_ref[c] = s.T.reshape(H, DK, DV)

    @pl.when(safe)
    def _():
        prep, raw, o_intra = {}, {}, {}
        s = state_in(0)
        for c in range(CB + 2):
            if c < CB:
                prep[c] = prepare(slice(c * L, (c + 1) * L))
                raw[c] = intra_scores(prep[c])
            if 1 <= c <= CB:
                o_intra[c - 1] = intra_apply(prep[c - 1], raw.pop(c - 1))
            if c >= 2:
                d = c - 2
                if not carry_state and d > 0:
                    s = state_in(d)
                s = finish(prep.pop(d), o_intra.pop(d), s)
                if not carry_state or d == CB - 1:
                    state_out(d, s)

    @pl.when(jnp.logical_not(safe))
    def _():
        def body(c, carry):
            p = prepare(pl.ds(pl.multiple_of(c * L, L), L))
            state_out(c, finish(p, intra_per_token(p), state_in(c)))
            return carry

        lax.fori_loop(0, CB, body, 0)

    if carry_state:
        @pl.when(step == pl.num_programs(1) - 1)
        def _():
            sout_ref[0] = s_scr[...].T.reshape(H, DK, DV)


def _gla(qg, kg, vg, rg, gate, s0, g_out, *, nb, t, L, CB, carry_state):
    n = qg.shape[0]
    tm = L * CB
    if carry_state:
        assert t % tm == 0
        grid = (nb, t // tm)
        steps = t // tm
        row = lambda b, s: (b * steps + s, 0)
        st = lambda b, s: (b, 0, 0, 0)
        state_block = (1, GLA_HEADS, GLA_DK, GLA_DV)
    else:
        assert t == L and nb % CB == 0
        grid = (nb // CB, 1)
        row = lambda b, s: (b, 0)
        st = lambda b, s: (b, 0, 0, 0)
        state_block = (CB, GLA_HEADS, GLA_DK, GLA_DV)
    return pl.pallas_call(
        functools.partial(_gla_kernel, L=L, CB=CB, carry_state=carry_state),
        grid=grid,
        in_specs=[pl.BlockSpec((tm, W_G), row), pl.BlockSpec((tm, W_G), row),
                  pl.BlockSpec((tm, W_V), row), pl.BlockSpec((tm, W_V), row),
                  pl.BlockSpec((tm, W_G), row), pl.BlockSpec(state_block, st),
                  pl.BlockSpec((1, GLA_DV), lambda b, s: (0, 0))],
        out_specs=(pl.BlockSpec((tm, W_V), row), pl.BlockSpec(state_block, st)),
        out_shape=(jax.ShapeDtypeStruct((n, W_V), BF16),
                   jax.ShapeDtypeStruct((nb, GLA_HEADS, GLA_DK, GLA_DV), F32)),
        scratch_shapes=[pltpu.VMEM((GLA_DV, GLA_HEADS * GLA_DK), F32),
                        pltpu.VMEM((GLA_HEADS * L, GLA_DV), F32),
                        pltpu.VMEM((L, W_G), F32), pltpu.VMEM((L, W_G), F32),
                        pltpu.VMEM((L, W_G), F32), pltpu.VMEM((tm, W_G), F32)],
        compiler_params=pltpu.CompilerParams(
            dimension_semantics=("arbitrary", "arbitrary"), vmem_limit_bytes=VMEM_LIMIT_BYTES),
        name="gla",
    )(qg, kg, vg, rg, gate, s0, g_out)


def _lambda(lam_ref, lam_init):
    lp = lam_ref[...]
    s1 = jnp.sum(lp[0:1, :] * lp[1:2, :], axis=-1, keepdims=True)
    s2 = jnp.sum(lp[2:3, :] * lp[3:4, :], axis=-1, keepdims=True)
    return jnp.exp(s1) - jnp.exp(s2) + lam_init


def _split_halves(q):
    lane = lax.broadcasted_iota(jnp.int32, q.shape, 1)
    zero = jnp.zeros_like(q)
    return jnp.concatenate(
        [jnp.where(lane < DIFF_DH, q, zero), jnp.where(lane >= DIFF_DH, q, zero)], axis=0)


def _finish_attention(acc, l, lam, gsub, lam_init, tq):
    o = acc[:tq] / l[:tq] - lam * (acc[tq:] / l[tq:])
    return _rms(o, gsub) * (1.0 - lam_init)


def _split3_bf16(x):
    hi = x.astype(BF16).astype(F32)
    r = x - hi
    mid = r.astype(BF16).astype(F32)
    lo = (r - mid).astype(BF16).astype(F32)
    return hi, mid, lo


def _attn_prompt_kernel(slopes_ref, lam_ref, qt_ref, k_ref, vt_ref, gsub_ref, o_ref,
                        aug_scr, qt_scr, corr_scr, acc_scr, m_scr, s_scr, *, tq, lam_init):
    nh = DIFF_HEADS
    tk = tq
    t = k_ref.shape[2]
    nq = t // tq
    slope2 = [slopes_ref[hd] * LOG2E for hd in range(nh)]
    j_lane = 3 * nh

    lane = lax.broadcasted_iota(jnp.int32, (tk, LANES), 1)
    k_loc = lax.broadcasted_iota(jnp.int32, (tk, LANES), 0).astype(F32)
    base = jnp.zeros((tk, LANES), F32)
    for hd in range(nh):
        for i, part in enumerate(_split3_bf16(slope2[hd] * k_loc)):
            base = jnp.where(lane == 3 * hd + i, part, base)
    blk_lanes = (lane >= j_lane) & (lane < j_lane + 3)
    for j in range(t // tk):
        aug_scr[j * tk:(j + 1) * tk, :] = jnp.where(blk_lanes, float(j), base).astype(BF16)

    row = lax.broadcasted_iota(jnp.int32, (LANES, 2 * tq), 0)
    kl = lax.broadcasted_iota(jnp.int32, (tk, tq), 0)
    ql = lax.broadcasted_iota(jnp.int32, (tk, tq), 1)
    same_chunk_or_earlier = _div_pow2(kl, CHUNK) <= _div_pow2(ql, CHUNK)
    ahead = jnp.maximum(kl - ql, 0).astype(F32)
    for hd in range(nh):
        rows = jnp.where((row >= 3 * hd) & (row < 3 * hd + 3), 1.0, 0.0)
        for i, part in enumerate(_split3_bf16(jnp.full((LANES, 2 * tq), slope2[hd] * tk, F32))):
            rows = jnp.where(row == j_lane + i, part, rows)
        qt_scr[hd, LANES:2 * LANES, :] = rows.astype(BF16)
        qt_scr[hd, DIFF_DH:2 * DIFF_DH, 0:tq] = jnp.zeros((DIFF_DH, tq), BF16)
        qt_scr[hd, 0:DIFF_DH, tq:2 * tq] = jnp.zeros((DIFF_DH, tq), BF16)
        corr_scr[hd] = jnp.where(same_chunk_or_earlier, (-2.0 * slope2[hd]) * ahead, -jnp.inf)

    lam = _lambda(lam_ref, lam_init)
    gsub = gsub_ref[...]

    def ksteps(steps, write_output=None):
        items = [(j0, nblk, diag, hd) for j0, nblk, diag in steps for hd in range(nh)]

        def key_rows(j0, nblk):
            return pl.ds(pl.multiple_of(j0 * tk, tk), nblk * tk)

        def scores(i):
            j0, nblk, diag, hd = items[i]
            ks = key_rows(j0, nblk)
            k_aug = jnp.concatenate([k_ref[0, hd, ks, :], aug_scr[ks, :]], axis=1)
            s_scr[i % S_SLOTS, 0:nblk * tk, :] = jnp.dot(
                k_aug, qt_scr[hd], preferred_element_type=F32)

        def softmax(i):
            j0, nblk, diag, hd = items[i]
            p_parts, alpha_parts = [], []
            for g in range(2 * tq // LANES):
                lanes = slice(g * LANES, (g + 1) * LANES)
                sg = s_scr[i % S_SLOTS, 0:nblk * tk, lanes]
                if diag:
                    c0 = (g * LANES) % tq
                    sg = sg + corr_scr[hd, :, c0:c0 + LANES]
                m_old = m_scr[hd:hd + 1, lanes]
                m_new = jnp.maximum(m_old, jnp.max(sg, axis=0, keepdims=True))
                m_scr[hd:hd + 1, lanes] = m_new
                p_parts.append(jnp.exp2(sg - m_new).astype(BF16))
                alpha_parts.append(jnp.exp2(m_old - m_new))
            return jnp.concatenate(p_parts, axis=1), jnp.concatenate(alpha_parts, axis=1)

        def accumulate(i, p, alpha):
            j0, nblk, diag, hd = items[i]
            acc_scr[hd] = alpha * acc_scr[hd] + jnp.dot(
                vt_ref[0, hd, :, key_rows(j0, nblk)], p, preferred_element_type=F32)

        def retire(i, p, alpha):
            accumulate(i, p, alpha)
            if items[i][2]:
                write_output(items[i][3])

        n = len(items)
        for i in range(min(SCORE_AHEAD, n)):
            scores(i)
        pending = None
        for i in range(n):
            if i + SCORE_AHEAD < n:
                scores(i + SCORE_AHEAD)
            p_alpha = softmax(i)
            if pending is not None:
                retire(i - 1, *pending)
            pending = p_alpha
        retire(n - 1, *pending)

    def qblock(qi, carry):
        qs = pl.ds(pl.multiple_of(qi * tq, tq), tq)
        for hd in range(nh):
            qt = qt_ref[0, hd, :, qs]
            qt_scr[hd, 0:DIFF_DH, 0:tq] = qt[0:DIFF_DH, :]
            qt_scr[hd, DIFF_DH:2 * DIFF_DH, tq:2 * tq] = qt[DIFF_DH:2 * DIFF_DH, :]
        m_scr[...] = jnp.full(m_scr.shape, -jnp.inf, F32)
        acc_scr[...] = jnp.zeros(acc_scr.shape, F32)

        def write_output(hd):
            acc = acc_scr[hd]
            inv_l = 1.0 / acc[DIFF_DV:DIFF_DV + 1, :]
            o_t = (acc[0:DIFF_DV, 0:tq] * inv_l[:, 0:tq]
                   - lam * (acc[0:DIFF_DV, tq:2 * tq] * inv_l[:, tq:2 * tq]))
            ms = jnp.mean(o_t * o_t, axis=0, keepdims=True)
            o_t = o_t * lax.rsqrt(ms + EPS)
            o_ref[qs, hd * DIFF_DV:(hd + 1) * DIFF_DV] = (
                o_t.T * gsub * (1.0 - lam_init)).astype(BF16)

        per_trip = K_UNROLL * K_WIDE

        def off_diag(jj, c):
            ksteps([(jj * per_trip + u * K_WIDE, K_WIDE, False) for u in range(K_UNROLL)])
            return c

        lax.fori_loop(0, qi // per_trip, off_diag, 0)
        done = (qi // per_trip) * per_trip
        for left in range(per_trip):
            @pl.when(qi - done == left)
            def _(left=left):
                steps, off, size = [], 0, per_trip // 2
                while size >= 1:
                    if left & size:
                        nblk = min(size, K_WIDE)
                        steps += [(done + off + u * nblk, nblk, False) for u in range(size // nblk)]
                        off += size
                    size //= 2
                ksteps(steps + [(qi, 1, True)], write_output)

        return carry

    lax.fori_loop(0, nq, qblock, 0)


def _attn_prompt(slopes, lam_p, qdt, kdb, vdt, g_sub, *, tq, lam_init):
    nb, nh, t, _ = kdb.shape
    assert t % tq == 0 and tq % CHUNK == 0 and nh == DIFF_HEADS
    per_stream = lambda b, *_: (b, 0, 0, 0)
    grid_spec = pltpu.PrefetchScalarGridSpec(
        num_scalar_prefetch=1,
        grid=(nb,),
        in_specs=[pl.BlockSpec((4, DIFF_DH), lambda b, *_: (0, 0)),
                  pl.BlockSpec((1, nh, LANES, t), per_stream),
                  pl.BlockSpec((1, nh, t, LANES), per_stream),
                  pl.BlockSpec((1, nh, V_AUG_ROWS, t), per_stream),
                  pl.BlockSpec((1, DIFF_DV), lambda b, *_: (0, 0))],
        out_specs=pl.BlockSpec((t, nh * DIFF_DV), lambda b, *_: (b, 0)),
        scratch_shapes=[pltpu.VMEM((t, LANES), BF16),
                        pltpu.VMEM((nh, 2 * LANES, 2 * tq), BF16),
                        pltpu.VMEM((nh, tq, tq), F32),
                        pltpu.VMEM((nh, V_AUG_ROWS, 2 * tq), F32),
                        pltpu.VMEM((8, 2 * tq), F32),
                        pltpu.VMEM((S_SLOTS, K_WIDE * tq, 2 * tq), F32)],
    )
    return pl.pallas_call(
        functools.partial(_attn_prompt_kernel, tq=tq, lam_init=lam_init),
        grid_spec=grid_spec,
        out_shape=jax.ShapeDtypeStruct((nb * t, nh * DIFF_DV), BF16),
        compiler_params=pltpu.CompilerParams(
            dimension_semantics=("arbitrary",), vmem_limit_bytes=VMEM_LIMIT_BYTES),
        name="attn_prompt",
    )(slopes, lam_p, qdt, kdb, vdt, g_sub)


def _attn_sample_kernel(slopes_ref, lam_ref, q_ref, ck_ref, cv_ref, kn_ref, vn_ref, gsub_ref, o_ref,
                        *, tq, past, lam_init):
    nh = DIFF_HEADS
    nt = (((1,), (1,)), ((), ()))
    k_pos = lax.broadcasted_iota(jnp.int32, (1, past), 1).astype(F32)
    qi = lax.broadcasted_iota(jnp.int32, (tq, tq), 0)
    kj = lax.broadcasted_iota(jnp.int32, (tq, tq), 1)
    rel_new = (past + qi - jnp.abs(qi - kj)).astype(F32)
    lam = _lambda(lam_ref, lam_init)
    gsub = gsub_ref[...]
    old = [pl.ds(hd, past, stride=nh) for hd in range(nh)]
    new = [pl.ds(hd, tq, stride=nh) for hd in range(nh)]

    def scores(hd):
        slope = slopes_ref[hd]
        qq = _split_halves(q_ref[0, hd])
        s_c = lax.dot_general(qq, ck_ref[0, old[hd], :].astype(BF16), nt,
                              preferred_element_type=F32)
        s_n = lax.dot_general(qq, kn_ref[0, new[hd], :].astype(BF16), nt,
                              preferred_element_type=F32)
        bias_n = slope * rel_new
        return s_c + slope * k_pos, s_n + jnp.concatenate([bias_n, bias_n], axis=0)

    def softmax(s_c, s_n):
        m = jnp.maximum(jnp.max(s_c, axis=-1, keepdims=True), jnp.max(s_n, axis=-1, keepdims=True))
        p_c = jnp.exp(s_c - m)
        p_n = jnp.exp(s_n - m)
        l = jnp.sum(p_c, axis=-1, keepdims=True) + jnp.sum(p_n, axis=-1, keepdims=True)
        return p_c.astype(BF16), p_n.astype(BF16), l

    def output(hd, p_c, p_n, l):
        acc = (jnp.dot(p_c, cv_ref[0, old[hd], :].astype(BF16), preferred_element_type=F32)
               + jnp.dot(p_n, vn_ref[0, new[hd], :].astype(BF16), preferred_element_type=F32))
        o_ref[:, hd * DIFF_DV:(hd + 1) * DIFF_DV] = _finish_attention(
            acc, l, lam, gsub, lam_init, tq).astype(BF16)

    s_next = scores(0)
    for hd in range(nh):
        s_cur, s_next = s_next, (scores(hd + 1) if hd + 1 < nh else None)
        output(hd, *softmax(*s_cur))


def _attn_sample(slopes, lam_p, qd, cache_k, cache_v, k_new, v_new, g_sub, *, lam_init):
    nb, rows, _ = cache_k.shape
    nh = DIFF_HEADS
    past = rows // nh
    tq = k_new.shape[1] // (nb * nh)
    cache_spec = pl.BlockSpec((1, past * nh, LANES), lambda b, *_: (b, 0, 0))
    new_spec = pl.BlockSpec((1, tq * nh, LANES), lambda b, *_: (0, b, 0))
    grid_spec = pltpu.PrefetchScalarGridSpec(
        num_scalar_prefetch=1,
        grid=(nb,),
        in_specs=[pl.BlockSpec((4, DIFF_DH), lambda b, *_: (0, 0)),
                  pl.BlockSpec((1, nh, tq, LANES), lambda b, *_: (0, 0, b, 0)),
                  cache_spec, cache_spec, new_spec, new_spec,
                  pl.BlockSpec((1, DIFF_DV), lambda b, *_: (0, 0))],
        out_specs=pl.BlockSpec((tq, nh * DIFF_DV), lambda b, *_: (b, 0)),
    )
    return pl.pallas_call(
        functools.partial(_attn_sample_kernel, tq=tq, past=past, lam_init=lam_init),
        grid_spec=grid_spec,
        out_shape=jax.ShapeDtypeStruct((nb * tq, nh * DIFF_DV), BF16),
        compiler_params=pltpu.CompilerParams(
            dimension_semantics=("arbitrary",), vmem_limit_bytes=VMEM_LIMIT_BYTES),
        name="attn_sample",
    )(slopes, lam_p, qd, cache_k, cache_v, k_new, v_new, g_sub)


def _out_ffn_kernel(og_ref, od_ref, x_ref, wo_ref, wu_ref, wd_ref,
                    gpm_ref, gpf_ref, gqf_ref, y_ref, *, ff_chunk):
    mix = (jnp.dot(og_ref[...], wo_ref[:W_V, :], preferred_element_type=F32)
           + jnp.dot(od_ref[...], wo_ref[W_V:, :], preferred_element_type=F32))
    x1 = x_ref[...] + _rms(mix, gpm_ref[...])
    f = _rms(x1, gpf_ref[...]).astype(BF16)
    d_ff = wu_ref.shape[1]
    acc = jnp.zeros(x1.shape, F32)
    for c in range(d_ff // ff_chunk):
        sl = slice(c * ff_chunk, (c + 1) * ff_chunk)
        hid = jnp.dot(f, wu_ref[:, sl], preferred_element_type=F32)
        hid = jnp.square(jnp.maximum(hid, 0.0)).astype(BF16)
        acc = acc + jnp.dot(hid, wd_ref[sl, :], preferred_element_type=F32)
    y_ref[...] = x1 + _rms(acc, gqf_ref[...])


def _out_ffn(og, od, x2d, w_out, w_up, w_down, g_post_mix, g_pre_ffn, g_post_ffn, *, tm):
    n, d = x2d.shape
    assert n % tm == 0 and w_up.shape[1] % FF_CHUNK == 0
    row = lambda i: (i, 0)
    return pl.pallas_call(
        functools.partial(_out_ffn_kernel, ff_chunk=FF_CHUNK),
        grid=(n // tm,),
        in_specs=[pl.BlockSpec((tm, W_V), row), pl.BlockSpec((tm, W_V), row),
                  pl.BlockSpec((tm, d), row),
                  _const_spec(w_out.shape), _const_spec(w_up.shape), _const_spec(w_down.shape),
                  _const_spec((1, d)), _const_spec((1, d)), _const_spec((1, d))],
        out_specs=pl.BlockSpec((tm, d), row),
        out_shape=jax.ShapeDtypeStruct((n, d), F32),
        compiler_params=pltpu.CompilerParams(
            dimension_semantics=("arbitrary",), vmem_limit_bytes=VMEM_LIMIT_BYTES),
        name="out_ffn",
    )(og, od, x2d, w_out, w_up, w_down, g_post_mix, g_pre_ffn, g_post_ffn)


def _split_w_in(w):
    a0 = 2 * W_G + 2 * W_V
    a1 = a0 + GLA_RANK
    w_a = jnp.pad(w[:, a0:a1], ((0, 0), (0, LANES - GLA_RANK)))
    return w[:, :a0].astype(BF16), w[:, a1:].astype(BF16), w_a.astype(BF16)


def kernel(x_prompt, x_sample, cache_k, cache_v, state_gla, w_in, w_gate_up, b_gate, g_gla_out,
           lam_q1, lam_k1, lam_q2, lam_k2, g_subln, w_out, g_pre_mix, g_post_mix,
           g_pre_ffn, g_post_ffn, w_ff_up, w_ff_down):
    depth = w_in.shape[0]
    nb_p, t_p, d = x_prompt.shape
    nb_s, t_s, _ = x_sample.shape
    past = cache_k.shape[2]
    slopes = jnp.exp2(-8.0 / DIFF_HEADS * jnp.arange(1, DIFF_HEADS + 1, dtype=F32))
    yp = x_prompt.reshape(nb_p * t_p, d)
    ys = x_sample.reshape(nb_s * t_s, d)
    outs = [[] for _ in range(6)]
    for l in range(depth):
        lam_init = 0.8 - 0.6 * math.exp(-0.3 * l)
        w_parts = _split_w_in(w_in[l])
        wg_pad = jnp.concatenate(
            [w_gate_up[l], jnp.zeros((LANES - GLA_RANK, W_G), F32)], axis=0).astype(BF16)
        bg = b_gate[l][None, :]
        lam_p = jnp.stack([lam_q1[l], lam_k1[l], lam_q2[l], lam_k2[l]], axis=0)
        g_out = g_gla_out[l][None, :]
        g_sub = g_subln[l][None, :]
        wo = w_out[l].astype(BF16)
        wu = w_ff_up[l].astype(BF16)
        wd = w_ff_down[l].astype(BF16)
        gains = (g_post_mix[l][None, :], g_pre_ffn[l][None, :], g_post_ffn[l][None, :])
        g_pre = g_pre_mix[l][None, :]

        qg, kg, vg, rg, gate, qd, kd, kdb, vd, vdb = _in_proj(
            yp, g_pre, w_parts, wg_pad, bg, nb=nb_p, t=t_p, tm=min(IN_PROJ_ROWS, t_p),
            transposed=True)
        s0 = jnp.zeros((nb_p, GLA_HEADS, GLA_DK, GLA_DV), F32)
        og, s_p = _gla(qg, kg, vg, rg, gate, s0, g_out, nb=nb_p, t=t_p, L=CHUNK,
                       CB=GLA_CHUNKS_PER_TILE, carry_state=True)
        od = _attn_prompt(slopes, lam_p, qd, kdb, vdb, g_sub, tq=ATTN_BLOCK, lam_init=lam_init)
        yp = _out_ffn(og, od, yp, wo, wu, wd, *gains, tm=min(OUT_FFN_ROWS, nb_p * t_p))
        outs[0].append(kd.reshape(nb_p, t_p, DIFF_HEADS, 2 * DIFF_DH))
        outs[1].append(vd.reshape(nb_p, t_p, DIFF_HEADS, DIFF_DV))
        outs[2].append(s_p)

        n_s = nb_s * t_s
        qg, kg, vg, rg, gate, qd, kd, _, vd, _ = _in_proj(
            ys, g_pre, w_parts, wg_pad, bg, nb=1, t=n_s, tm=n_s, transposed=False)
        og, s_s = _gla(qg, kg, vg, rg, gate, state_gla[l], g_out, nb=nb_s, t=t_s, L=t_s,
                       CB=math.gcd(nb_s, GLA_CHUNKS_PER_TILE), carry_state=False)
        od = _attn_sample(slopes, lam_p, qd,
                          cache_k[l].reshape(nb_s, past * DIFF_HEADS, 2 * DIFF_DH),
                          cache_v[l].reshape(nb_s, past * DIFF_HEADS, DIFF_DV),
                          kd, vd, g_sub, lam_init=lam_init)
        ys = _out_ffn(og, od, ys, wo, wu, wd, *gains, tm=n_s)
        outs[3].append(kd.reshape(nb_s, t_s, DIFF_HEADS, 2 * DIFF_DH))
        outs[4].append(vd.reshape(nb_s, t_s, DIFF_HEADS, DIFF_DV))
        outs[5].append(s_s)

    stack = lambda xs: jnp.stack(xs, axis=0)
    return (yp.reshape(nb_p, t_p, d), ys.reshape(nb_s, t_s, d),
            stack(outs[0]), stack(outs[1]), stack(outs[2]),
            stack(outs[3]), stack(outs[4]), stack(outs[5]))
```

```python
import functools
import math

import jax
import jax.numpy as jnp
from jax import lax
from jax.experimental import pallas as pl
from jax.experimental.pallas import tpu as pltpu

F32 = jnp.float32
BF16 = jnp.bfloat16

EPS = 1e-6
CHUNK = 64
GLA_HEADS = 4
GLA_DK = 64
GLA_DV = 128
GLA_RANK = 16
GLA_TAU = 16.0
DIFF_HEADS = 4
DIFF_DH = 64
DIFF_DV = 128

W_G = GLA_HEADS * GLA_DK
W_V = GLA_HEADS * GLA_DV
W_D = DIFF_HEADS * 2 * DIFF_DH
LANES = 128
OFF_QG = 0
OFF_KG = OFF_QG + W_G
OFF_VG = OFF_KG + W_G
OFF_RG = OFF_VG + W_V
OFF_QD = 0
OFF_KD = OFF_QD + W_D
OFF_VD = OFF_KD + W_D

BF16_SUBLANES = 16
V_AUG_ROWS = DIFF_DV + BF16_SUBLANES
LOG2E = math.log2(math.e)
K_WIDE = 4
K_UNROLL = 1
SCORE_AHEAD = 2
S_SLOTS = SCORE_AHEAD + 1

VMEM_LIMIT_BYTES = 56 * 1024 * 1024
IN_PROJ_ROWS = 1024
OUT_FFN_ROWS = 1024
FF_CHUNK = 1024
GLA_CHUNKS_PER_TILE = 32
ATTN_BLOCK = 256
GLA_SAFE_LOG_DECAY = -60.0


def _rms(x, g):
    ms = jnp.mean(x * x, axis=-1, keepdims=True)
    return x * lax.rsqrt(ms + EPS) * g


def _log_sigmoid(x):
    return jnp.minimum(x, 0.0) - jnp.log(1.0 + jnp.exp(-jnp.abs(x)))


def _div_pow2(x, d):
    assert d & (d - 1) == 0
    return lax.shift_right_arithmetic(x, d.bit_length() - 1)


def _const_spec(shape):
    zeros = (0,) * len(shape)
    return pl.BlockSpec(shape, lambda *_: zeros, pipeline_mode=pl.Buffered(1))


def _in_proj_kernel(x_ref, g_ref, w_gla_ref, w_diff_ref, w_a_ref, wg_ref, bg_ref,
                    qg_ref, kg_ref, vg_ref, rg_ref, gate_ref,
                    qd_ref, kd_ref, kdb_ref, vd_ref, vdb_ref, *, transposed):
    h = _rms(x_ref[...], g_ref[...]).astype(BF16)

    def proj(w_ref, off, width):
        return jnp.dot(h, w_ref[:, off:off + width], preferred_element_type=F32)

    ag = proj(w_a_ref, 0, LANES).astype(BF16)
    pre = jnp.dot(ag, wg_ref[...], preferred_element_type=F32) + bg_ref[...]
    gate_ref[...] = _log_sigmoid(pre) * (1.0 / GLA_TAU)
    qd = proj(w_diff_ref, OFF_QD, W_D) * (DIFF_DH ** -0.5)
    vd = proj(w_diff_ref, OFF_VD, W_D)
    kd = proj(w_diff_ref, OFF_KD, W_D)
    tm = qd.shape[0]
    for hd in range(DIFF_HEADS):
        head_rows = pl.ds(hd, tm, stride=DIFF_HEADS)
        kd_ref[0, head_rows, :] = kd[:, hd * LANES:(hd + 1) * LANES]
        vd_ref[0, head_rows, :] = vd[:, hd * LANES:(hd + 1) * LANES]
    ones_rows = (lax.broadcasted_iota(jnp.int32, (V_AUG_ROWS - DIFF_DV, tm), 0) == 0).astype(BF16)
    for hd in range(DIFF_HEADS):
        sl = slice(hd * LANES, (hd + 1) * LANES)
        kdb_ref[0, hd] = kd[:, sl].astype(BF16)
        if transposed:
            qd_ref[0, hd] = (qd[:, sl] * LOG2E).T.astype(BF16)
            vdb_ref[0, hd, 0:DIFF_DV, :] = vd[:, sl].T.astype(BF16)
            vdb_ref[0, hd, DIFF_DV:V_AUG_ROWS, :] = ones_rows
        else:
            qd_ref[0, hd] = qd[:, sl].astype(BF16)
            vdb_ref[0, hd] = vd[:, sl].astype(BF16)
    qg_ref[...] = (proj(w_gla_ref, OFF_QG, W_G) * (GLA_DK ** -0.5)).astype(BF16)
    kg_ref[...] = proj(w_gla_ref, OFF_KG, W_G).astype(BF16)
    vg_ref[...] = proj(w_gla_ref, OFF_VG, W_V).astype(BF16)
    rg_ref[...] = proj(w_gla_ref, OFF_RG, W_V).astype(BF16)


def _in_proj(x2d, g_pre, w_parts, wg_pad, b_gate, *, nb, t, tm, transposed):
    n, d = x2d.shape
    assert t % tm == 0 and n == nb * t
    steps_per_b = t // tm
    row = lambda i: (i, 0)
    hm = lambda i: (i // steps_per_b, 0, i % steps_per_b, 0)
    hm_t = lambda i: (i // steps_per_b, 0, 0, i % steps_per_b)
    hm_shape = jax.ShapeDtypeStruct((nb, DIFF_HEADS, t, LANES), BF16)
    hm_spec = pl.BlockSpec((1, DIFF_HEADS, tm, LANES), hm)
    kv_shape = jax.ShapeDtypeStruct((nb, t * DIFF_HEADS, LANES), F32)
    kv_spec = pl.BlockSpec((1, tm * DIFF_HEADS, LANES),
                           lambda i: (i // steps_per_b, i % steps_per_b, 0))
    if transposed:
        q_shape = jax.ShapeDtypeStruct((nb, DIFF_HEADS, LANES, t), BF16)
        q_spec = pl.BlockSpec((1, DIFF_HEADS, LANES, tm), hm_t)
        v_shape = jax.ShapeDtypeStruct((nb, DIFF_HEADS, V_AUG_ROWS, t), BF16)
        v_spec = pl.BlockSpec((1, DIFF_HEADS, V_AUG_ROWS, tm), hm_t)
    else:
        q_shape, q_spec, v_shape, v_spec = hm_shape, hm_spec, hm_shape, hm_spec
    out_shape = (
        jax.ShapeDtypeStruct((n, W_G), BF16), jax.ShapeDtypeStruct((n, W_G), BF16),
        jax.ShapeDtypeStruct((n, W_V), BF16), jax.ShapeDtypeStruct((n, W_V), BF16),
        jax.ShapeDtypeStruct((n, W_G), F32),
        q_shape,
        kv_shape, hm_shape,
        kv_shape, v_shape,
    )
    out_specs = (
        pl.BlockSpec((tm, W_G), row), pl.BlockSpec((tm, W_G), row),
        pl.BlockSpec((tm, W_V), row), pl.BlockSpec((tm, W_V), row),
        pl.BlockSpec((tm, W_G), row),
        q_spec,
        kv_spec, hm_spec,
        kv_spec, v_spec,
    )
    return pl.pallas_call(
        functools.partial(_in_proj_kernel, transposed=transposed),
        grid=(n // tm,),
        in_specs=[pl.BlockSpec((tm, d), row), _const_spec((1, d)),
                  *[_const_spec(w.shape) for w in w_parts],
                  _const_spec(wg_pad.shape), _const_spec((1, W_G))],
        out_specs=out_specs,
        out_shape=out_shape,
        compiler_params=pltpu.CompilerParams(
            dimension_semantics=("arbitrary",), vmem_limit_bytes=VMEM_LIMIT_BYTES),
        name="in_proj",
    )(x2d, g_pre, *w_parts, wg_pad, b_gate)


def _gla_kernel(q_ref, k_ref, v_ref, r_ref, gate_ref, s0_ref, gout_ref,
                o_ref, sout_ref, s_scr, oi_scr, qf_scr, kf_scr, bf_scr, b_scr,
                *, L, CB, carry_state):
    H, DK, DV = GLA_HEADS, GLA_DK, GLA_DV
    HL = H * L
    step = pl.program_id(1)

    if carry_state:
        @pl.when(step == 0)
        def _():
            s_scr[...] = s0_ref[0].reshape(H * DK, DV).T

    lane_head = _div_pow2(lax.broadcasted_iota(jnp.int32, (L, W_G), 1), DK)
    row_i = lax.broadcasted_iota(jnp.int32, (HL, HL), 0)
    col_i = lax.broadcasted_iota(jnp.int32, (HL, HL), 1)
    causal = (col_i <= row_i) & (col_i >= (row_i & ~(L - 1)))
    row_t = lax.broadcasted_iota(jnp.int32, (L, 1), 0)
    nt = (((1,), (1,)), ((), ()))
    tn = (((0,), (0,)), ((), ()))
    gout = gout_ref[...]

    def split_hi_lo(x):
        hi = x.astype(BF16)
        return hi, (x - hi.astype(F32)).astype(BF16)

    tri = jnp.where(lax.broadcasted_iota(jnp.int32, (L, L), 1)
                    <= lax.broadcasted_iota(jnp.int32, (L, L), 0), 1.0, 0.0).astype(BF16)
    totals = []
    for c in range(CB):
        g_c = gate_ref[c * L:(c + 1) * L, :]
        c_hi, c_lo = split_hi_lo(g_c)
        b_scr[c * L:(c + 1) * L, :] = (jnp.dot(tri, c_hi, preferred_element_type=F32)
                                       + jnp.dot(tri, c_lo, preferred_element_type=F32))
        totals.append(jnp.sum(g_c, axis=0, keepdims=True))
    safe = jnp.min(jnp.concatenate(totals, axis=0)) >= GLA_SAFE_LOG_DECAY

    def stack_heads(a):
        return jnp.concatenate(
            [jnp.where(lane_head == hd, a, 0.0) for hd in range(H)], axis=0)

    def prepare(rows):
        b = b_scr[rows, :]
        b_last = b[L - 1:L, :]
        q = q_ref[rows, :].astype(F32)
        k = k_ref[rows, :].astype(F32)
        v = v_ref[rows, :]
        qs = stack_heads(q * jnp.exp(b)).astype(BF16)
        kends = stack_heads(k * jnp.exp(b_last - b)).astype(BF16)
        vs = jnp.concatenate([v[:, hd * DV:(hd + 1) * DV] for hd in range(H)], axis=0)
        kv_t = lax.dot_general(vs, kends, tn, preferred_element_type=F32)
        return dict(rows=rows, q=q, k=k, b=b, b_last=b_last, qs=qs, vs=vs, kv_t=kv_t)

    def intra_scores(p):
        ks = stack_heads(p["k"] * jnp.exp(-p["b"])).astype(BF16)
        return lax.dot_general(p["qs"], ks, nt, preferred_element_type=F32)

    def intra_apply(p, a):
        a = jnp.where(causal, a, 0.0).astype(BF16)
        return jnp.dot(a, p["vs"], preferred_element_type=F32)

    def intra_per_token(p):
        rows = p["rows"]
        qf_scr[...] = p["q"]
        kf_scr[...] = p["k"]
        bf_scr[...] = p["b"]

        def tok(t, carry_t):
            w = (qf_scr[pl.ds(t, 1), :] * kf_scr[...]
                 * jnp.exp(jnp.minimum(bf_scr[pl.ds(t, 1), :] - bf_scr[...], 0.0)))
            for hd in range(H):
                a_col = jnp.sum(jnp.where(lane_head == hd, w, 0.0), axis=-1, keepdims=True)
                a_col = jnp.where(row_t <= t, a_col, 0.0)
                vh = v_ref[rows, hd * DV:(hd + 1) * DV].astype(F32)
                oi_scr[pl.ds(hd * L + t, 1), :] = jnp.sum(a_col * vh, axis=0, keepdims=True)
            return carry_t

        lax.fori_loop(0, L, tok, 0)
        return oi_scr[...]

    def finish(p, o_intra, s_old):
        rows = p["rows"]
        o = (lax.dot_general(p["qs"], s_old.astype(BF16), nt, preferred_element_type=F32)
             + o_intra)
        o = _rms(o, gout)
        r = r_ref[rows, :].astype(F32)
        for hd in range(H):
            rh = r[:, hd * DV:(hd + 1) * DV]
            o_ref[rows, hd * DV:(hd + 1) * DV] = (
                o[hd * L:(hd + 1) * L, :] * (rh * jax.nn.sigmoid(rh))).astype(BF16)
        return jnp.exp(p["b_last"]) * s_old + p["kv_t"]

    def state_in(c):
        return s_scr[...] if carry_state else s0_ref[c].reshape(H * DK, DV).T

    def state_out(c, s):
        if carry_state:
            s_scr[...] = s
        else:
            sout_ref[c] = s.T.reshape(H, DK, DV)

    @pl.when(safe)
    def _():
        prep, raw, o_intra = {}, {}, {}
        s = state_in(0)
        for c in range(CB + 2):
            if c < CB:
                prep[c] = prepare(slice(c * L, (c + 1) * L))
                raw[c] = intra_scores(prep[c])
            if 1 <= c <= CB:
                o_intra[c - 1] = intra_apply(prep[c - 1], raw.pop(c - 1))
            if c >= 2:
                d = c - 2
                if not carry_state and d > 0:
                    s = state_in(d)
                s = finish(prep.pop(d), o_intra.pop(d), s)
                if not carry_state or d == CB - 1:
                    state_out(d, s)

    @pl.when(jnp.logical_not(safe))
    def _():
        def body(c, carry):
            p = prepare(pl.ds(pl.multiple_of(c * L, L), L))
            state_out(c, finish(p, intra_per_token(p), state_in(c)))
            return carry

        lax.fori_loop(0, CB, body, 0)

    if carry_state:
        @pl.when(step == pl.num_programs(1) - 1)
        def _():
            sout_ref[0] = s_scr[...].T.reshape(H, DK, DV)


def _gla(qg, kg, vg, rg, gate, s0, g_out, *, nb, t, L, CB, carry_state):
    n = qg.shape[0]
    tm = L * CB
    if carry_state:
        assert t % tm == 0
        grid = (nb, t // tm)
        steps = t // tm
        row = lambda b, s: (b * steps + s, 0)
        st = lambda b, s: (b, 0, 0, 0)
        state_block = (1, GLA_HEADS, GLA_DK, GLA_DV)
    else:
        assert t == L and nb % CB == 0
        grid = (nb // CB, 1)
        row = lambda b, s: (b, 0)
        st = lambda b, s: (b, 0, 0, 0)
        state_block = (CB, GLA_HEADS, GLA_DK, GLA_DV)
    return pl.pallas_call(
        functools.partial(_gla_kernel, L=L, CB=CB, carry_state=carry_state),
        grid=grid,
        in_specs=[pl.BlockSpec((tm, W_G), row), pl.BlockSpec((tm, W_G), row),
                  pl.BlockSpec((tm, W_V), row), pl.BlockSpec((tm, W_V), row),
                  pl.BlockSpec((tm, W_G), row), pl.BlockSpec(state_block, st),
                  pl.BlockSpec((1, GLA_DV), lambda b, s: (0, 0))],
        out_specs=(pl.BlockSpec((tm, W_V), row), pl.BlockSpec(state_block, st)),
        out_shape=(jax.ShapeDtypeStruct((n, W_V), BF16),
                   jax.ShapeDtypeStruct((nb, GLA_HEADS, GLA_DK, GLA_DV), F32)),
        scratch_shapes=[pltpu.VMEM((GLA_DV, GLA_HEADS * GLA_DK), F32),
                        pltpu.VMEM((GLA_HEADS * L, GLA_DV), F32),
                        pltpu.VMEM((L, W_G), F32), pltpu.VMEM((L, W_G), F32),
                        pltpu.VMEM((L, W_G), F32), pltpu.VMEM((tm, W_G), F32)],
        compiler_params=pltpu.CompilerParams(
            dimension_semantics=("arbitrary", "arbitrary"), vmem_limit_bytes=VMEM_LIMIT_BYTES),
        name="gla",
    )(qg, kg, vg, rg, gate, s0, g_out)


def _lambda(lam_ref, lam_init):
    lp = lam_ref[...]
    s1 = jnp.sum(lp[0:1, :] * lp[1:2, :], axis=-1, keepdims=True)
    s2 = jnp.sum(lp[2:3, :] * lp[3:4, :], axis=-1, keepdims=True)
    return jnp.exp(s1) - jnp.exp(s2) + lam_init


def _split_halves(q):
    lane = lax.broadcasted_iota(jnp.int32, q.shape, 1)
    zero = jnp.zeros_like(q)
    return jnp.concatenate(
        [jnp.where(lane < DIFF_DH, q, zero), jnp.where(lane >= DIFF_DH, q, zero)], axis=0)


def _finish_attention(acc, l, lam, gsub, lam_init, tq):
    o = acc[:tq] / l[:tq] - lam * (acc[tq:] / l[tq:])
    return _rms(o, gsub) * (1.0 - lam_init)


def _split3_bf16(x):
    hi = x.astype(BF16).astype(F32)
    r = x - hi
    mid = r.astype(BF16).astype(F32)
    lo = (r - mid).astype(BF16).astype(F32)
    return hi, mid, lo


def _attn_prompt_kernel(slopes_ref, lam_ref, qt_ref, k_ref, vt_ref, gsub_ref, o_ref,
                        aug_scr, qt_scr, corr_scr, acc_scr, m_scr, s_scr, *, tq, lam_init):
    nh = DIFF_HEADS
    tk = tq
    t = k_ref.shape[2]
    nq = t // tq
    slope2 = [slopes_ref[hd] * LOG2E for hd in range(nh)]
    j_lane = 3 * nh

    lane = lax.broadcasted_iota(jnp.int32, (tk, LANES), 1)
    k_loc = lax.broadcasted_iota(jnp.int32, (tk, LANES), 0).astype(F32)
    base = jnp.zeros((tk, LANES), F32)
    for hd in range(nh):
        for i, part in enumerate(_split3_bf16(slope2[hd] * k_loc)):
            base = jnp.where(lane == 3 * hd + i, part, base)
    blk_lanes = (lane >= j_lane) & (lane < j_lane + 3)
    for j in range(t // tk):
        aug_scr[j * tk:(j + 1) * tk, :] = jnp.where(blk_lanes, float(j), base).astype(BF16)

    row = lax.broadcasted_iota(jnp.int32, (LANES, 2 * tq), 0)
    kl = lax.broadcasted_iota(jnp.int32, (tk, tq), 0)
    ql = lax.broadcasted_iota(jnp.int32, (tk, tq), 1)
    same_chunk_or_earlier = _div_pow2(kl, CHUNK) <= _div_pow2(ql, CHUNK)
    ahead = jnp.maximum(kl - ql, 0).astype(F32)
    for hd in range(nh):
        rows = jnp.where((row >= 3 * hd) & (row < 3 * hd + 3), 1.0, 0.0)
        for i, part in enumerate(_split3_bf16(jnp.full((LANES, 2 * tq), slope2[hd] * tk, F32))):
            rows = jnp.where(row == j_lane + i, part, rows)
        qt_scr[hd, LANES:2 * LANES, :] = rows.astype(BF16)
        qt_scr[hd, DIFF_DH:2 * DIFF_DH, 0:tq] = jnp.zeros((DIFF_DH, tq), BF16)
        qt_scr[hd, 0:DIFF_DH, tq:2 * tq] = jnp.zeros((DIFF_DH, tq), BF16)
        corr_scr[hd] = jnp.where(same_chunk_or_earlier, (-2.0 * slope2[hd]) * ahead, -jnp.inf)

    lam = _lambda(lam_ref, lam_init)
    gsub = gsub_ref[...]

    def ksteps(steps, write_output=None):
        items = [(j0, nblk, diag, hd) for j0, nblk, diag in steps for hd in range(nh)]

        def key_rows(j0, nblk):
            return pl.ds(pl.multiple_of(j0 * tk, tk), nblk * tk)

        def scores(i):
            j0, nblk, diag, hd = items[i]
            ks = key_rows(j0, nblk)
            k_aug = jnp.concatenate([k_ref[0, hd, ks, :], aug_scr[ks, :]], axis=1)
            s_scr[i % S_SLOTS, 0:nblk * tk, :] = jnp.dot(
                k_aug, qt_scr[hd], preferred_element_type=F32)

        def softmax(i):
            j0, nblk, diag, hd = items[i]
            p_parts, alpha_parts = [], []
            for g in range(2 * tq // LANES):
                lanes = slice(g * LANES, (g + 1) * LANES)
                sg = s_scr[i % S_SLOTS, 0:nblk * tk, lanes]
                if diag:
                    c0 = (g * LANES) % tq
                    sg = sg + corr_scr[hd, :, c0:c0 + LANES]
                m_old = m_scr[hd:hd + 1, lanes]
                m_new = jnp.maximum(m_old, jnp.max(sg, axis=0, keepdims=True))
                m_scr[hd:hd + 1, lanes] = m_new
                p_parts.append(jnp.exp2(sg - m_new).astype(BF16))
                alpha_parts.append(jnp.exp2(m_old - m_new))
            return jnp.concatenate(p_parts, axis=1), jnp.concatenate(alpha_parts, axis=1)

        def accumulate(i, p, alpha):
            j0, nblk, diag, hd = items[i]
            acc_scr[hd] = alpha * acc_scr[hd] + jnp.dot(
                vt_ref[0, hd, :, key_rows(j0, nblk)], p, preferred_element_type=F32)

        def retire(i, p, alpha):
            accumulate(i, p, alpha)
            if items[i][2]:
                write_output(items[i][3])

        n = len(items)
        for i in range(min(SCORE_AHEAD, n)):
            scores(i)
        pending = None
        for i in range(n):
            if i + SCORE_AHEAD < n:
                scores(i + SCORE_AHEAD)
            p_alpha = softmax(i)
            if pending is not None:
                retire(i - 1, *pending)
            pending = p_alpha
        retire(n - 1, *pending)

    def qblock(qi, carry):
        qs = pl.ds(pl.multiple_of(qi * tq, tq), tq)
        for hd in range(nh):
            qt = qt_ref[0, hd, :, qs]
            qt_scr[hd, 0:DIFF_DH, 0:tq] = qt[0:DIFF_DH, :]
            qt_scr[hd, DIFF_DH:2 * DIFF_DH, tq:2 * tq] = qt[DIFF_DH:2 * DIFF_DH, :]
        m_scr[...] = jnp.full(m_scr.shape, -jnp.inf, F32)
        acc_scr[...] = jnp.zeros(acc_scr.shape, F32)

        def write_output(hd):
            acc = acc_scr[hd]
            inv_l = 1.0 / acc[DIFF_DV:DIFF_DV + 1, :]
            o_t = (acc[0:DIFF_DV, 0:tq] * inv_l[:, 0:tq]
                   - lam * (acc[0:DIFF_DV, tq:2 * tq] * inv_l[:, tq:2 * tq]))
            ms = jnp.mean(o_t * o_t, axis=0, keepdims=True)
            o_t = o_t * lax.rsqrt(ms + EPS)
            o_ref[qs, hd * DIFF_DV:(hd + 1) * DIFF_DV] = (
                o_t.T * gsub * (1.0 - lam_init)).astype(BF16)

        per_trip = K_UNROLL * K_WIDE

        def off_diag(jj, c):
            ksteps([(jj * per_trip + u * K_WIDE, K_WIDE, False) for u in range(K_UNROLL)])
            return c

        lax.fori_loop(0, qi // per_trip, off_diag, 0)
        done = (qi // per_trip) * per_trip
        for left in range(per_trip):
            @pl.when(qi - done == left)
            def _(left=left):
                steps, off, size = [], 0, per_trip // 2
                while size >= 1:
                    if left & size:
                        nblk = min(size, K_WIDE)
                        steps += [(done + off + u * nblk, nblk, False) for u in range(size // nblk)]
                        off += size
                    size //= 2
                ksteps(steps + [(qi, 1, True)], write_output)

        return carry

    lax.fori_loop(0, nq, qblock, 0)


def _attn_prompt(slopes, lam_p, qdt, kdb, vdt, g_sub, *, tq, lam_init):
    nb, nh, t, _ = kdb.shape
    assert t % tq == 0 and tq % CHUNK == 0 and nh == DIFF_HEADS
    per_stream = lambda b, *_: (b, 0, 0, 0)
    grid_spec = pltpu.PrefetchScalarGridSpec(
        num_scalar_prefetch=1,
        grid=(nb,),
        in_specs=[pl.BlockSpec((4, DIFF_DH), lambda b, *_: (0, 0)),
                  pl.BlockSpec((1, nh, LANES, t), per_stream),
                  pl.BlockSpec((1, nh, t, LANES), per_stream),
                  pl.BlockSpec((1, nh, V_AUG_ROWS, t), per_stream),
                  pl.BlockSpec((1, DIFF_DV), lambda b, *_: (0, 0))],
        out_specs=pl.BlockSpec((t, nh * DIFF_DV), lambda b, *_: (b, 0)),
        scratch_shapes=[pltpu.VMEM((t, LANES), BF16),
                        pltpu.VMEM((nh, 2 * LANES, 2 * tq), BF16),
                        pltpu.VMEM((nh, tq, tq), F32),
                        pltpu.VMEM((nh, V_AUG_ROWS, 2 * tq), F32),
                        pltpu.VMEM((8, 2 * tq), F32),
                        pltpu.VMEM((S_SLOTS, K_WIDE * tq, 2 * tq), F32)],
    )
    return pl.pallas_call(
        functools.partial(_attn_prompt_kernel, tq=tq, lam_init=lam_init),
        grid_spec=grid_spec,
        out_shape=jax.ShapeDtypeStruct((nb * t, nh * DIFF_DV), BF16),
        compiler_params=pltpu.CompilerParams(
            dimension_semantics=("arbitrary",), vmem_limit_bytes=VMEM_LIMIT_BYTES),
        name="attn_prompt",
    )(slopes, lam_p, qdt, kdb, vdt, g_sub)


def _attn_sample_kernel(slopes_ref, lam_ref, q_ref, ck_ref, cv_ref, kn_ref, vn_ref, gsub_ref, o_ref,
                        *, tq, past, lam_init):
    nh = DIFF_HEADS
    nt = (((1,), (1,)), ((), ()))
    k_pos = lax.broadcasted_iota(jnp.int32, (1, past), 1).astype(F32)
    qi = lax.broadcasted_iota(jnp.int32, (tq, tq), 0)
    kj = lax.broadcasted_iota(jnp.int32, (tq, tq), 1)
    rel_new = (past + qi - jnp.abs(qi - kj)).astype(F32)
    lam = _lambda(lam_ref, lam_init)
    gsub = gsub_ref[...]
    old = [pl.ds(hd, past, stride=nh) for hd in range(nh)]
    new = [pl.ds(hd, tq, stride=nh) for hd in range(nh)]

    def scores(hd):
        slope = slopes_ref[hd]
        qq = _split_halves(q_ref[0, hd])
        s_c = lax.dot_general(qq, ck_ref[0, old[hd], :].astype(BF16), nt,
                              preferred_element_type=F32)
        s_n = lax.dot_general(qq, kn_ref[0, new[hd], :].astype(BF16), nt,
                              preferred_element_type=F32)
        bias_n = slope * rel_new
        return s_c + slope * k_pos, s_n + jnp.concatenate([bias_n, bias_n], axis=0)

    def softmax(s_c, s_n):
        m = jnp.maximum(jnp.max(s_c, axis=-1, keepdims=True), jnp.max(s_n, axis=-1, keepdims=True))
        p_c = jnp.exp(s_c - m)
        p_n = jnp.exp(s_n - m)
        l = jnp.sum(p_c, axis=-1, keepdims=True) + jnp.sum(p_n, axis=-1, keepdims=True)
        return p_c.astype(BF16), p_n.astype(BF16), l

    def output(hd, p_c, p_n, l):
        acc = (jnp.dot(p_c, cv_ref[0, old[hd], :].astype(BF16), preferred_element_type=F32)
               + jnp.dot(p_n, vn_ref[0, new[hd], :].astype(BF16), preferred_element_type=F32))
        o_ref[:, hd * DIFF_DV:(hd + 1) * DIFF_DV] = _finish_attention(
            acc, l, lam, gsub, lam_init, tq).astype(BF16)

    s_next = scores(0)
    for hd in range(nh):
        s_cur, s_next = s_next, (scores(hd + 1) if hd + 1 < nh else None)
        output(hd, *softmax(*s_cur))


def _attn_sample(slopes, lam_p, qd, cache_k, cache_v, k_new, v_new, g_sub, *, lam_init):
    nb, rows, _ = cache_k.shape
    nh = DIFF_HEADS
    past = rows // nh
    tq = k_new.shape[1] // (nb * nh)
    cache_spec = pl.BlockSpec((1, past * nh, LANES), lambda b, *_: (b, 0, 0))
    new_spec = pl.BlockSpec((1, tq * nh, LANES), lambda b, *_: (0, b, 0))
    grid_spec = pltpu.PrefetchScalarGridSpec(
        num_scalar_prefetch=1,
        grid=(nb,),
        in_specs=[pl.BlockSpec((4, DIFF_DH), lambda b, *_: (0, 0)),
                  pl.BlockSpec((1, nh, tq, LANES), lambda b, *_: (0, 0, b, 0)),
                  cache_spec, cache_spec, new_spec, new_spec,
                  pl.BlockSpec((1, DIFF_DV), lambda b, *_: (0, 0))],
        out_specs=pl.BlockSpec((tq, nh * DIFF_DV), lambda b, *_: (b, 0)),
    )
    return pl.pallas_call(
        functools.partial(_attn_sample_kernel, tq=tq, past=past, lam_init=lam_init),
        grid_spec=grid_spec,
        out_shape=jax.ShapeDtypeStruct((nb * tq, nh * DIFF_DV), BF16),
        compiler_params=pltpu.CompilerParams(
            dimension_semantics=("arbitrary",), vmem_limit_bytes=VMEM_LIMIT_BYTES),
        name="attn_sample",
    )(slopes, lam_p, qd, cache_k, cache_v, k_new, v_new, g_sub)


def _out_ffn_kernel(og_ref, od_ref, x_ref, wo_ref, wu_ref, wd_ref,
                    gpm_ref, gpf_ref, gqf_ref, y_ref, *, ff_chunk):
    mix = (jnp.dot(og_ref[...], wo_ref[:W_V, :], preferred_element_type=F32)
           + jnp.dot(od_ref[...], wo_ref[W_V:, :], preferred_element_type=F32))
    x1 = x_ref[...] + _rms(mix, gpm_ref[...])
    f = _rms(x1, gpf_ref[...]).astype(BF16)
    d_ff = wu_ref.shape[1]
    acc = jnp.zeros(x1.shape, F32)
    for c in range(d_ff // ff_chunk):
        sl = slice(c * ff_chunk, (c + 1) * ff_chunk)
        hid = jnp.dot(f, wu_ref[:, sl], preferred_element_type=F32)
        hid = jnp.square(jnp.maximum(hid, 0.0)).astype(BF16)
        acc = acc + jnp.dot(hid, wd_ref[sl, :], preferred_element_type=F32)
    y_ref[...] = x1 + _rms(acc, gqf_ref[...])


def _out_ffn(og, od, x2d, w_out, w_up, w_down, g_post_mix, g_pre_ffn, g_post_ffn, *, tm):
    n, d = x2d.shape
    assert n % tm == 0 and w_up.shape[1] % FF_CHUNK == 0
    row = lambda i: (i, 0)
    return pl.pallas_call(
        functools.partial(_out_ffn_kernel, ff_chunk=FF_CHUNK),
        grid=(n // tm,),
        in_specs=[pl.BlockSpec((tm, W_V), row), pl.BlockSpec((tm, W_V), row),
                  pl.BlockSpec((tm, d), row),
                  _const_spec(w_out.shape), _const_spec(w_up.shape), _const_spec(w_down.shape),
                  _const_spec((1, d)), _const_spec((1, d)), _const_spec((1, d))],
        out_specs=pl.BlockSpec((tm, d), row),
        out_shape=jax.ShapeDtypeStruct((n, d), F32),
        compiler_params=pltpu.CompilerParams(
            dimension_semantics=("arbitrary",), vmem_limit_bytes=VMEM_LIMIT_BYTES),
        name="out_ffn",
    )(og, od, x2d, w_out, w_up, w_down, g_post_mix, g_pre_ffn, g_post_ffn)


def _split_w_in(w):
    a0 = 2 * W_G + 2 * W_V
    a1 = a0 + GLA_RANK
    w_a = jnp.pad(w[:, a0:a1], ((0, 0), (0, LANES - GLA_RANK)))
    return w[:, :a0].astype(BF16), w[:, a1:].astype(BF16), w_a.astype(BF16)


def kernel(x_prompt, x_sample, cache_k, cache_v, state_gla, w_in, w_gate_up, b_gate, g_gla_out,
           lam_q1, lam_k1, lam_q2, lam_k2, g_subln, w_out, g_pre_mix, g_post_mix,
           g_pre_ffn, g_post_ffn, w_ff_up, w_ff_down):
    depth = w_in.shape[0]
    nb_p, t_p, d = x_prompt.shape
    nb_s, t_s, _ = x_sample.shape
    past = cache_k.shape[2]
    slopes = jnp.exp2(-8.0 / DIFF_HEADS * jnp.arange(1, DIFF_HEADS + 1, dtype=F32))
    yp = x_prompt.reshape(nb_p * t_p, d)
    ys = x_sample.reshape(nb_s * t_s, d)
    outs = [[] for _ in range(6)]
    for l in range(depth):
        lam_init = 0.8 - 0.6 * math.exp(-0.3 * l)
        w_parts = _split_w_in(w_in[l])
        wg_pad = jnp.concatenate(
            [w_gate_up[l], jnp.zeros((LANES - GLA_RANK, W_G), F32)], axis=0).astype(BF16)
        bg = b_gate[l][None, :]
        lam_p = jnp.stack([lam_q1[l], lam_k1[l], lam_q2[l], lam_k2[l]], axis=0)
        g_out = g_gla_out[l][None, :]
        g_sub = g_subln[l][None, :]
        wo = w_out[l].astype(BF16)
        wu = w_ff_up[l].astype(BF16)
        wd = w_ff_down[l].astype(BF16)
        gains = (g_post_mix[l][None, :], g_pre_ffn[l][None, :], g_post_ffn[l][None, :])
        g_pre = g_pre_mix[l][None, :]

        qg, kg, vg, rg, gate, qd, kd, kdb, vd, vdb = _in_proj(
            yp, g_pre, w_parts, wg_pad, bg, nb=nb_p, t=t_p, tm=min(IN_PROJ_ROWS, t_p),
            transposed=True)
        s0 = jnp.zeros((nb_p, GLA_HEADS, GLA_DK, GLA_DV), F32)
        og, s_p = _gla(qg, kg, vg, rg, gate, s0, g_out, nb=nb_p, t=t_p, L=CHUNK,
                       CB=GLA_CHUNKS_PER_TILE, carry_state=True)
        od = _attn_prompt(slopes, lam_p, qd, kdb, vdb, g_sub, tq=ATTN_BLOCK, lam_init=lam_init)
        yp = _out_ffn(og, od, yp, wo, wu, wd, *gains, tm=min(OUT_FFN_ROWS, nb_p * t_p))
        outs[0].append(kd.reshape(nb_p, t_p, DIFF_HEADS, 2 * DIFF_DH))
        outs[1].append(vd.reshape(nb_p, t_p, DIFF_HEADS, DIFF_DV))
        outs[2].append(s_p)

        n_s = nb_s * t_s
        qg, kg, vg, rg, gate, qd, kd, _, vd, _ = _in_proj(
            ys, g_pre, w_parts, wg_pad, bg, nb=1, t=n_s, tm=n_s, transposed=False)
        og, s_s = _gla(qg, kg, vg, rg, gate, state_gla[l], g_out, nb=nb_s, t=t_s, L=t_s,
                       CB=math.gcd(nb_s, GLA_CHUNKS_PER_TILE), carry_state=False)
        od = _attn_sample(slopes, lam_p, qd,
                          cache_k[l].reshape(nb_s, past * DIFF_HEADS, 2 * DIFF_DH),
                          cache_v[l].reshape(nb_s, past * DIFF_HEADS, DIFF_DV),
                          kd, vd, g_sub, lam_init=lam_init)
        ys = _out_ffn(og, od, ys, wo, wu, wd, *gains, tm=n_s)
        outs[3].append(kd.reshape(nb_s, t_s, DIFF_HEADS, 2 * DIFF_DH))
        outs[4].append(vd.reshape(nb_s, t_s, DIFF_HEADS, DIFF_DV))
        outs[5].append(s_s)

    stack = lambda xs: jnp.stack(xs, axis=0)
    return (yp.reshape(nb_p, t_p, d), ys.reshape(nb_s, t_s, d),
            stack(outs[0]), stack(outs[1]), stack(outs[2]),
            stack(outs[3]), stack(outs[4]), stack(outs[5]))
```

```python
import functools
import math

import jax
import jax.numpy as jnp
from jax import lax
from jax.experimental import pallas as pl
from jax.experimental.pallas import tpu as pltpu

F32 = jnp.float32
BF16 = jnp.bfloat16

EPS = 1e-6
CHUNK = 64
GLA_HEADS = 4
GLA_DK = 64
GLA_DV = 128
GLA_RANK = 16
GLA_TAU = 16.0
DIFF_HEADS = 4
DIFF_DH = 64
DIFF_DV = 128

W_G = GLA_HEADS * GLA_DK
W_V = GLA_HEADS * GLA_DV
W_D = DIFF_HEADS * 2 * DIFF_DH
LANES = 128
OFF_QG = 0
OFF_KG = OFF_QG + W_G
OFF_VG = OFF_KG + W_G
OFF_RG = OFF_VG + W_V
OFF_QD = 0
OFF_KD = OFF_QD + W_D
OFF_VD = OFF_KD + W_D

BF16_SUBLANES = 16
V_AUG_ROWS = DIFF_DV + BF16_SUBLANES
LOG2E = math.log2(math.e)
K_WIDE = 4
K_UNROLL = 1
SCORE_AHEAD = 2
S_SLOTS = SCORE_AHEAD + 1

VMEM_LIMIT_BYTES = 56 * 1024 * 1024
IN_PROJ_ROWS = 1024
OUT_FFN_ROWS = 1024
FF_CHUNK = 1024
GLA_CHUNKS_PER_TILE = 32
ATTN_BLOCK = 256
SAMPLE_STREAMS_PER_STEP = 2
GLA_SAFE_LOG_DECAY = -60.0


def _rms(x, g):
    ms = jnp.mean(x * x, axis=-1, keepdims=True)
    return x * lax.rsqrt(ms + EPS) * g


def _log_sigmoid(x):
    return jnp.minimum(x, 0.0) - jnp.log(1.0 + jnp.exp(-jnp.abs(x)))


def _div_pow2(x, d):
    assert d & (d - 1) == 0
    return lax.shift_right_arithmetic(x, d.bit_length() - 1)


def _const_spec(shape):
    zeros = (0,) * len(shape)
    return pl.BlockSpec(shape, lambda *_: zeros, pipeline_mode=pl.Buffered(1))


def _in_proj_kernel(x_ref, g_ref, w_gla_ref, w_diff_ref, w_a_ref, wg_ref, bg_ref,
                    qg_ref, kg_ref, vg_ref, rg_ref, gate_ref,
                    qd_ref, kd_ref, kdb_ref, vd_ref, vdb_ref, *, transposed):
    h = _rms(x_ref[...], g_ref[...]).astype(BF16)

    def proj(w_ref, off, width):
        return jnp.dot(h, w_ref[:, off:off + width], preferred_element_type=F32)

    ag = proj(w_a_ref, 0, LANES).astype(BF16)
    pre = jnp.dot(ag, wg_ref[...], preferred_element_type=F32) + bg_ref[...]
    gate_ref[...] = _log_sigmoid(pre) * (1.0 / GLA_TAU)
    qd = proj(w_diff_ref, OFF_QD, W_D) * (DIFF_DH ** -0.5)
    vd = proj(w_diff_ref, OFF_VD, W_D)
    kd = proj(w_diff_ref, OFF_KD, W_D)
    tm = qd.shape[0]
    for hd in range(DIFF_HEADS):
        head_rows = pl.ds(hd, tm, stride=DIFF_HEADS)
        kd_ref[0, head_rows, :] = kd[:, hd * LANES:(hd + 1) * LANES]
        vd_ref[0, head_rows, :] = vd[:, hd * LANES:(hd + 1) * LANES]
    ones_rows = (lax.broadcasted_iota(jnp.int32, (V_AUG_ROWS - DIFF_DV, tm), 0) == 0).astype(BF16)
    for hd in range(DIFF_HEADS):
        sl = slice(hd * LANES, (hd + 1) * LANES)
        kdb_ref[0, hd] = kd[:, sl].astype(BF16)
        if transposed:
            qd_ref[0, hd] = (qd[:, sl] * LOG2E).T.astype(BF16)
            vdb_ref[0, hd, 0:DIFF_DV, :] = vd[:, sl].T.astype(BF16)
            vdb_ref[0, hd, DIFF_DV:V_AUG_ROWS, :] = ones_rows
        else:
            qd_ref[0, hd] = qd[:, sl].astype(BF16)
            vdb_ref[0, hd] = vd[:, sl].astype(BF16)
    qg_ref[...] = (proj(w_gla_ref, OFF_QG, W_G) * (GLA_DK ** -0.5)).astype(BF16)
    kg_ref[...] = proj(w_gla_ref, OFF_KG, W_G).astype(BF16)
    vg_ref[...] = proj(w_gla_ref, OFF_VG, W_V).astype(BF16)
    rg_ref[...] = proj(w_gla_ref, OFF_RG, W_V).astype(BF16)


def _in_proj(x2d, g_pre, w_parts, wg_pad, b_gate, *, nb, t, tm, transposed):
    n, d = x2d.shape
    assert t % tm == 0 and n == nb * t
    steps_per_b = t // tm
    row = lambda i: (i, 0)
    hm = lambda i: (i // steps_per_b, 0, i % steps_per_b, 0)
    hm_t = lambda i: (i // steps_per_b, 0, 0, i % steps_per_b)
    hm_shape = jax.ShapeDtypeStruct((nb, DIFF_HEADS, t, LANES), BF16)
    hm_spec = pl.BlockSpec((1, DIFF_HEADS, tm, LANES), hm)
    kv_shape = jax.ShapeDtypeStruct((nb, t * DIFF_HEADS, LANES), F32)
    kv_spec = pl.BlockSpec((1, tm * DIFF_HEADS, LANES),
                           lambda i: (i // steps_per_b, i % steps_per_b, 0))
    if transposed:
        q_shape = jax.ShapeDtypeStruct((nb, DIFF_HEADS, LANES, t), BF16)
        q_spec = pl.BlockSpec((1, DIFF_HEADS, LANES, tm), hm_t)
        v_shape = jax.ShapeDtypeStruct((nb, DIFF_HEADS, V_AUG_ROWS, t), BF16)
        v_spec = pl.BlockSpec((1, DIFF_HEADS, V_AUG_ROWS, tm), hm_t)
    else:
        q_shape, q_spec, v_shape, v_spec = hm_shape, hm_spec, hm_shape, hm_spec
    out_shape = (
        jax.ShapeDtypeStruct((n, W_G), BF16), jax.ShapeDtypeStruct((n, W_G), BF16),
        jax.ShapeDtypeStruct((n, W_V), BF16), jax.ShapeDtypeStruct((n, W_V), BF16),
        jax.ShapeDtypeStruct((n, W_G), F32),
        q_shape,
        kv_shape, hm_shape,
        kv_shape, v_shape,
    )
    out_specs = (
        pl.BlockSpec((tm, W_G), row), pl.BlockSpec((tm, W_G), row),
        pl.BlockSpec((tm, W_V), row), pl.BlockSpec((tm, W_V), row),
        pl.BlockSpec((tm, W_G), row),
        q_spec,
        kv_spec, hm_spec,
        kv_spec, v_spec,
    )
    return pl.pallas_call(
        functools.partial(_in_proj_kernel, transposed=transposed),
        grid=(n // tm,),
        in_specs=[pl.BlockSpec((tm, d), row), _const_spec((1, d)),
                  *[_const_spec(w.shape) for w in w_parts],
                  _const_spec(wg_pad.shape), _const_spec((1, W_G))],
        out_specs=out_specs,
        out_shape=out_shape,
        compiler_params=pltpu.CompilerParams(
            dimension_semantics=("arbitrary",), vmem_limit_bytes=VMEM_LIMIT_BYTES),
        name="in_proj",
    )(x2d, g_pre, *w_parts, wg_pad, b_gate)


def _gla_kernel(q_ref, k_ref, v_ref, r_ref, gate_ref, s0_ref, gout_ref,
                o_ref, sout_ref, s_scr, oi_scr, qf_scr, kf_scr, bf_scr, b_scr,
                *, L, CB, carry_state):
    H, DK, DV = GLA_HEADS, GLA_DK, GLA_DV
    HL = H * L
    step = pl.program_id(1)

    if carry_state:
        @pl.when(step == 0)
        def _():
            s_scr[...] = s0_ref[0].reshape(H * DK, DV).T

    lane_head = _div_pow2(lax.broadcasted_iota(jnp.int32, (L, W_G), 1), DK)
    row_i = lax.broadcasted_iota(jnp.int32, (HL, HL), 0)
    col_i = lax.broadcasted_iota(jnp.int32, (HL, HL), 1)
    causal = (col_i <= row_i) & (col_i >= (row_i & ~(L - 1)))
    row_t = lax.broadcasted_iota(jnp.int32, (L, 1), 0)
    nt = (((1,), (1,)), ((), ()))
    tn = (((0,), (0,)), ((), ()))
    gout = gout_ref[...]

    def split_hi_lo(x):
        hi = x.astype(BF16)
        return hi, (x - hi.astype(F32)).astype(BF16)

    tri = jnp.where(lax.broadcasted_iota(jnp.int32, (L, L), 1)
                    <= lax.broadcasted_iota(jnp.int32, (L, L), 0), 1.0, 0.0).astype(BF16)
    totals = []
    for c in range(CB):
        g_c = gate_ref[c * L:(c + 1) * L, :]
        c_hi, c_lo = split_hi_lo(g_c)
        b_scr[c * L:(c + 1) * L, :] = (jnp.dot(tri, c_hi, preferred_element_type=F32)
                                       + jnp.dot(tri, c_lo, preferred_element_type=F32))
        totals.append(jnp.sum(g_c, axis=0, keepdims=True))
    safe = jnp.min(jnp.concatenate(totals, axis=0)) >= GLA_SAFE_LOG_DECAY

    def stack_heads(a):
        return jnp.concatenate(
            [jnp.where(lane_head == hd, a, 0.0) for hd in range(H)], axis=0)

    def prepare(rows):
        b = b_scr[rows, :]
        b_last = b[L - 1:L, :]
        q = q_ref[rows, :].astype(F32)
        k = k_ref[rows, :].astype(F32)
        v = v_ref[rows, :]
        qs = stack_heads(q * jnp.exp(b)).astype(BF16)
        kends = stack_heads(k * jnp.exp(b_last - b)).astype(BF16)
        vs = jnp.concatenate([v[:, hd * DV:(hd + 1) * DV] for hd in range(H)], axis=0)
        kv_t = lax.dot_general(vs, kends, tn, preferred_element_type=F32)
        return dict(rows=rows, q=q, k=k, b=b, b_last=b_last, qs=qs, vs=vs, kv_t=kv_t)

    def intra_scores(p):
        ks = stack_heads(p["k"] * jnp.exp(-p["b"])).astype(BF16)
        return lax.dot_general(p["qs"], ks, nt, preferred_element_type=F32)

    def intra_apply(p, a):
        a = jnp.where(causal, a, 0.0).astype(BF16)
        return jnp.dot(a, p["vs"], preferred_element_type=F32)

    def intra_per_token(p):
        rows = p["rows"]
        qf_scr[...] = p["q"]
        kf_scr[...] = p["k"]
        bf_scr[...] = p["b"]

        def tok(t, carry_t):
            w = (qf_scr[pl.ds(t, 1), :] * kf_scr[...]
                 * jnp.exp(jnp.minimum(bf_scr[pl.ds(t, 1), :] - bf_scr[...], 0.0)))
            for hd in range(H):
                a_col = jnp.sum(jnp.where(lane_head == hd, w, 0.0), axis=-1, keepdims=True)
                a_col = jnp.where(row_t <= t, a_col, 0.0)
                vh = v_ref[rows, hd * DV:(hd + 1) * DV].astype(F32)
                oi_scr[pl.ds(hd * L + t, 1), :] = jnp.sum(a_col * vh, axis=0, keepdims=True)
            return carry_t

        lax.fori_loop(0, L, tok, 0)
        return oi_scr[...]

    def finish(p, o_intra, s_old):
        rows = p["rows"]
        o = (lax.dot_general(p["qs"], s_old.astype(BF16), nt, preferred_element_type=F32)
             + o_intra)
        o = _rms(o, gout)
        r = r_ref[rows, :].astype(F32)
        for hd in range(H):
            rh = r[:, hd * DV:(hd + 1) * DV]
            o_ref[rows, hd * DV:(hd + 1) * DV] = (
                o[hd * L:(hd + 1) * L, :] * (rh * jax.nn.sigmoid(rh))).astype(BF16)
        return jnp.exp(p["b_last"]) * s_old + p["kv_t"]

    def state_in(c):
        return s_scr[...] if carry_state else s0_ref[c].reshape(H * DK, DV).T

    def state_out(c, s):
        if carry_state:
            s_scr[...] = s
        else:
            sout_ref[c] = s.T.reshape(H, DK, DV)

    @pl.when(safe)
    def _():
        prep, raw, o_intra = {}, {}, {}
        s = state_in(0)
        for c in range(CB + 2):
            if c < CB:
                prep[c] = prepare(slice(c * L, (c + 1) * L))
                raw[c] = intra_scores(prep[c])
            if 1 <= c <= CB:
                o_intra[c - 1] = intra_apply(prep[c - 1], raw.pop(c - 1))
            if c >= 2:
                d = c - 2
                if not carry_state and d > 0:
                    s = state_in(d)
                s = finish(prep.pop(d), o_intra.pop(d), s)
                if not carry_state or d == CB - 1:
                    state_out(d, s)

    @pl.when(jnp.logical_not(safe))
    def _():
        def body(c, carry):
            p = prepare(pl.ds(pl.multiple_of(c * L, L), L))
            state_out(c, finish(p, intra_per_token(p), state_in(c)))
            return carry

        lax.fori_loop(0, CB, body, 0)

    if carry_state:
        @pl.when(step == pl.num_programs(1) - 1)
        def _():
            sout_ref[0] = s_scr[...].T.reshape(H, DK, DV)


def _gla(qg, kg, vg, rg, gate, s0, g_out, *, nb, t, L, CB, carry_state):
    n = qg.shape[0]
    tm = L * CB
    if carry_state:
        assert t % tm == 0
        grid = (nb, t // tm)
        steps = t // tm
        row = lambda b, s: (b * steps + s, 0)
        st = lambda b, s: (b, 0, 0, 0)
        state_block = (1, GLA_HEADS, GLA_DK, GLA_DV)
    else:
        assert t == L and nb % CB == 0
        grid = (nb // CB, 1)
        row = lambda b, s: (b, 0)
        st = lambda b, s: (b, 0, 0, 0)
        state_block = (CB, GLA_HEADS, GLA_DK, GLA_DV)
    return pl.pallas_call(
        functools.partial(_gla_kernel, L=L, CB=CB, carry_state=carry_state),
        grid=grid,
        in_specs=[pl.BlockSpec((tm, W_G), row), pl.BlockSpec((tm, W_G), row),
                  pl.BlockSpec((tm, W_V), row), pl.BlockSpec((tm, W_V), row),
                  pl.BlockSpec((tm, W_G), row), pl.BlockSpec(state_block, st),
                  pl.BlockSpec((1, GLA_DV), lambda b, s: (0, 0))],
        out_specs=(pl.BlockSpec((tm, W_V), row), pl.BlockSpec(state_block, st)),
        out_shape=(jax.ShapeDtypeStruct((n, W_V), BF16),
                   jax.ShapeDtypeStruct((nb, GLA_HEADS, GLA_DK, GLA_DV), F32)),
        scratch_shapes=[pltpu.VMEM((GLA_DV, GLA_HEADS * GLA_DK), F32),
                        pltpu.VMEM((GLA_HEADS * L, GLA_DV), F32),
                        pltpu.VMEM((L, W_G), F32), pltpu.VMEM((L, W_G), F32),
                        pltpu.VMEM((L, W_G), F32), pltpu.VMEM((tm, W_G), F32)],
        compiler_params=pltpu.CompilerParams(
            dimension_semantics=("arbitrary", "arbitrary"), vmem_limit_bytes=VMEM_LIMIT_BYTES),
        name="gla",
    )(qg, kg, vg, rg, gate, s0, g_out)


def _lambda(lam_ref, lam_init):
    lp = lam_ref[...]
    s1 = jnp.sum(lp[0:1, :] * lp[1:2, :], axis=-1, keepdims=True)
    s2 = jnp.sum(lp[2:3, :] * lp[3:4, :], axis=-1, keepdims=True)
    return jnp.exp(s1) - jnp.exp(s2) + lam_init


def _split_halves(q):
    lane = lax.broadcasted_iota(jnp.int32, q.shape, 1)
    zero = jnp.zeros_like(q)
    return jnp.concatenate(
        [jnp.where(lane < DIFF_DH, q, zero), jnp.where(lane >= DIFF_DH, q, zero)], axis=0)


def _finish_attention(acc, l, lam, gsub, lam_init, tq):
    o = acc[:tq] / l[:tq] - lam * (acc[tq:] / l[tq:])
    return _rms(o, gsub) * (1.0 - lam_init)


def _split3_bf16(x):
    hi = x.astype(BF16).astype(F32)
    r = x - hi
    mid = r.astype(BF16).astype(F32)
    lo = (r - mid).astype(BF16).astype(F32)
    return hi, mid, lo


def _attn_prompt_kernel(slopes_ref, lam_ref, qt_ref, k_ref, vt_ref, gsub_ref, o_ref,
                        aug_scr, qt_scr, corr_scr, acc_scr, m_scr, s_scr, *, tq, lam_init):
    nh = DIFF_HEADS
    tk = tq
    t = k_ref.shape[2]
    nq = t // tq
    slope2 = [slopes_ref[hd] * LOG2E for hd in range(nh)]
    j_lane = 3 * nh

    lane = lax.broadcasted_iota(jnp.int32, (tk, LANES), 1)
    k_loc = lax.broadcasted_iota(jnp.int32, (tk, LANES), 0).astype(F32)
    base = jnp.zeros((tk, LANES), F32)
    for hd in range(nh):
        for i, part in enumerate(_split3_bf16(slope2[hd] * k_loc)):
            base = jnp.where(lane == 3 * hd + i, part, base)
    blk_lanes = (lane >= j_lane) & (lane < j_lane + 3)
    for j in range(t // tk):
        aug_scr[j * tk:(j + 1) * tk, :] = jnp.where(blk_lanes, float(j), base).astype(BF16)

    row = lax.broadcasted_iota(jnp.int32, (LANES, 2 * tq), 0)
    kl = lax.broadcasted_iota(jnp.int32, (tk, tq), 0)
    ql = lax.broadcasted_iota(jnp.int32, (tk, tq), 1)
    same_chunk_or_earlier = _div_pow2(kl, CHUNK) <= _div_pow2(ql, CHUNK)
    ahead = jnp.maximum(kl - ql, 0).astype(F32)
    for hd in range(nh):
        rows = jnp.where((row >= 3 * hd) & (row < 3 * hd + 3), 1.0, 0.0)
        for i, part in enumerate(_split3_bf16(jnp.full((LANES, 2 * tq), slope2[hd] * tk, F32))):
            rows = jnp.where(row == j_lane + i, part, rows)
        qt_scr[hd, LANES:2 * LANES, :] = rows.astype(BF16)
        qt_scr[hd, DIFF_DH:2 * DIFF_DH, 0:tq] = jnp.zeros((DIFF_DH, tq), BF16)
        qt_scr[hd, 0:DIFF_DH, tq:2 * tq] = jnp.zeros((DIFF_DH, tq), BF16)
        corr_scr[hd] = jnp.where(same_chunk_or_earlier, (-2.0 * slope2[hd]) * ahead, -jnp.inf)

    lam = _lambda(lam_ref, lam_init)
    gsub = gsub_ref[...]

    def ksteps(steps, write_output=None):
        items = [(j0, nblk, diag, hd) for j0, nblk, diag in steps for hd in range(nh)]

        def key_rows(j0, nblk):
            return pl.ds(pl.multiple_of(j0 * tk, tk), nblk * tk)

        def scores(i):
            j0, nblk, diag, hd = items[i]
            ks = key_rows(j0, nblk)
            k_aug = jnp.concatenate([k_ref[0, hd, ks, :], aug_scr[ks, :]], axis=1)
            s_scr[i % S_SLOTS, 0:nblk * tk, :] = jnp.dot(
                k_aug, qt_scr[hd], preferred_element_type=F32)

        def softmax(i):
            j0, nblk, diag, hd = items[i]
            p_parts, alpha_parts = [], []
            for g in range(2 * tq // LANES):
                lanes = slice(g * LANES, (g + 1) * LANES)
                sg = s_scr[i % S_SLOTS, 0:nblk * tk, lanes]
                if diag:
                    c0 = (g * LANES) % tq
                    sg = sg + corr_scr[hd, :, c0:c0 + LANES]
                m_old = m_scr[hd:hd + 1, lanes]
                m_new = jnp.maximum(m_old, jnp.max(sg, axis=0, keepdims=True))
                m_scr[hd:hd + 1, lanes] = m_new
                p_parts.append(jnp.exp2(sg - m_new).astype(BF16))
                alpha_parts.append(jnp.exp2(m_old - m_new))
            return jnp.concatenate(p_parts, axis=1), jnp.concatenate(alpha_parts, axis=1)

        def accumulate(i, p, alpha):
            j0, nblk, diag, hd = items[i]
            acc_scr[hd] = alpha * acc_scr[hd] + jnp.dot(
                vt_ref[0, hd, :, key_rows(j0, nblk)], p, preferred_element_type=F32)

        def retire(i, p, alpha):
            accumulate(i, p, alpha)
            if items[i][2]:
                write_output(items[i][3])

        n = len(items)
        for i in range(min(SCORE_AHEAD, n)):
            scores(i)
        pending = None
        for i in range(n):
            if i + SCORE_AHEAD < n:
                scores(i + SCORE_AHEAD)
            p_alpha = softmax(i)
            if pending is not None:
                retire(i - 1, *pending)
            pending = p_alpha
        retire(n - 1, *pending)

    def qblock(qi, carry):
        qs = pl.ds(pl.multiple_of(qi * tq, tq), tq)
        for hd in range(nh):
            qt = qt_ref[0, hd, :, qs]
            qt_scr[hd, 0:DIFF_DH, 0:tq] = qt[0:DIFF_DH, :]
            qt_scr[hd, DIFF_DH:2 * DIFF_DH, tq:2 * tq] = qt[DIFF_DH:2 * DIFF_DH, :]
        m_scr[...] = jnp.full(m_scr.shape, -jnp.inf, F32)
        acc_scr[...] = jnp.zeros(acc_scr.shape, F32)

        def write_output(hd):
            acc = acc_scr[hd]
            inv_l = 1.0 / acc[DIFF_DV:DIFF_DV + 1, :]
            o_t = (acc[0:DIFF_DV, 0:tq] * inv_l[:, 0:tq]
                   - lam * (acc[0:DIFF_DV, tq:2 * tq] * inv_l[:, tq:2 * tq]))
            ms = jnp.mean(o_t * o_t, axis=0, keepdims=True)
            o_t = o_t * lax.rsqrt(ms + EPS)
            o_ref[qs, hd * DIFF_DV:(hd + 1) * DIFF_DV] = (
                o_t.T * gsub * (1.0 - lam_init)).astype(BF16)

        per_trip = K_UNROLL * K_WIDE

        def off_diag(jj, c):
            ksteps([(jj * per_trip + u * K_WIDE, K_WIDE, False) for u in range(K_UNROLL)])
            return c

        lax.fori_loop(0, qi // per_trip, off_diag, 0)
        done = (qi // per_trip) * per_trip
        for left in range(per_trip):
            @pl.when(qi - done == left)
            def _(left=left):
                steps, off, size = [], 0, per_trip // 2
                while size >= 1:
                    if left & size:
                        nblk = min(size, K_WIDE)
                        steps += [(done + off + u * nblk, nblk, False) for u in range(size // nblk)]
                        off += size
                    size //= 2
                ksteps(steps + [(qi, 1, True)], write_output)

        return carry

    lax.fori_loop(0, nq, qblock, 0)


def _attn_prompt(slopes, lam_p, qdt, kdb, vdt, g_sub, *, tq, lam_init):
    nb, nh, t, _ = kdb.shape
    assert t % tq == 0 and tq % CHUNK == 0 and nh == DIFF_HEADS
    per_stream = lambda b, *_: (b, 0, 0, 0)
    grid_spec = pltpu.PrefetchScalarGridSpec(
        num_scalar_prefetch=1,
        grid=(nb,),
        in_specs=[pl.BlockSpec((4, DIFF_DH), lambda b, *_: (0, 0)),
                  pl.BlockSpec((1, nh, LANES, t), per_stream),
                  pl.BlockSpec((1, nh, t, LANES), per_stream),
                  pl.BlockSpec((1, nh, V_AUG_ROWS, t), per_stream),
                  pl.BlockSpec((1, DIFF_DV), lambda b, *_: (0, 0))],
        out_specs=pl.BlockSpec((t, nh * DIFF_DV), lambda b, *_: (b, 0)),
        scratch_shapes=[pltpu.VMEM((t, LANES), BF16),
                        pltpu.VMEM((nh, 2 * LANES, 2 * tq), BF16),
                        pltpu.VMEM((nh, tq, tq), F32),
                        pltpu.VMEM((nh, V_AUG_ROWS, 2 * tq), F32),
                        pltpu.VMEM((8, 2 * tq), F32),
                        pltpu.VMEM((S_SLOTS, K_WIDE * tq, 2 * tq), F32)],
    )
    return pl.pallas_call(
        functools.partial(_attn_prompt_kernel, tq=tq, lam_init=lam_init),
        grid_spec=grid_spec,
        out_shape=jax.ShapeDtypeStruct((nb * t, nh * DIFF_DV), BF16),
        compiler_params=pltpu.CompilerParams(
            dimension_semantics=("arbitrary",), vmem_limit_bytes=VMEM_LIMIT_BYTES),
        name="attn_prompt",
    )(slopes, lam_p, qdt, kdb, vdt, g_sub)


def _attn_sample_kernel(slopes_ref, lam_ref, q_ref, ck_ref, cv_ref, kn_ref, vn_ref, gsub_ref, o_ref,
                        *, tq, past, sb, lam_init):
    nh = DIFF_HEADS
    nt = (((1,), (1,)), ((), ()))
    k_pos = lax.broadcasted_iota(jnp.int32, (1, past), 1).astype(F32)
    qi = lax.broadcasted_iota(jnp.int32, (tq, tq), 0)
    kj = lax.broadcasted_iota(jnp.int32, (tq, tq), 1)
    rel_new = (past + qi - jnp.abs(qi - kj)).astype(F32)
    lam = _lambda(lam_ref, lam_init)
    gsub = gsub_ref[...]
    items = [(s, hd) for s in range(sb) for hd in range(nh)]
    old = [pl.ds(hd, past, stride=nh) for hd in range(nh)]

    def new(s, hd):
        return pl.ds(s * tq * nh + hd, tq, stride=nh)

    def scores(s, hd):
        slope = slopes_ref[hd]
        qq = _split_halves(q_ref[0, hd, s * tq:(s + 1) * tq, :])
        s_c = lax.dot_general(qq, ck_ref[s, old[hd], :].astype(BF16), nt,
                              preferred_element_type=F32)
        s_n = lax.dot_general(qq, kn_ref[0, new(s, hd), :].astype(BF16), nt,
                              preferred_element_type=F32)
        bias_n = slope * rel_new
        return s_c + slope * k_pos, s_n + jnp.concatenate([bias_n, bias_n], axis=0)

    def softmax(s_c, s_n):
        m = jnp.maximum(jnp.max(s_c, axis=-1, keepdims=True), jnp.max(s_n, axis=-1, keepdims=True))
        p_c = jnp.exp(s_c - m)
        p_n = jnp.exp(s_n - m)
        l = jnp.sum(p_c, axis=-1, keepdims=True) + jnp.sum(p_n, axis=-1, keepdims=True)
        return p_c.astype(BF16), p_n.astype(BF16), l

    def output(s, hd, p_c, p_n, l):
        acc = (jnp.dot(p_c, cv_ref[s, old[hd], :].astype(BF16), preferred_element_type=F32)
               + jnp.dot(p_n, vn_ref[0, new(s, hd), :].astype(BF16), preferred_element_type=F32))
        o_ref[s * tq:(s + 1) * tq, hd * DIFF_DV:(hd + 1) * DIFF_DV] = _finish_attention(
            acc, l, lam, gsub, lam_init, tq).astype(BF16)

    s_next = scores(*items[0])
    for i, item in enumerate(items):
        s_cur, s_next = s_next, (scores(*items[i + 1]) if i + 1 < len(items) else None)
        output(*item, *softmax(*s_cur))


def _attn_sample(slopes, lam_p, qd, cache_k, cache_v, k_new, v_new, g_sub, *, lam_init):
    nb, rows, _ = cache_k.shape
    nh = DIFF_HEADS
    past = rows // nh
    tq = k_new.shape[1] // (nb * nh)
    sb = math.gcd(nb, SAMPLE_STREAMS_PER_STEP)
    cache_spec = pl.BlockSpec((sb, past * nh, LANES), lambda b, *_: (b, 0, 0))
    new_spec = pl.BlockSpec((1, sb * tq * nh, LANES), lambda b, *_: (0, b, 0))
    grid_spec = pltpu.PrefetchScalarGridSpec(
        num_scalar_prefetch=1,
        grid=(nb // sb,),
        in_specs=[pl.BlockSpec((4, DIFF_DH), lambda b, *_: (0, 0)),
                  pl.BlockSpec((1, nh, sb * tq, LANES), lambda b, *_: (0, 0, b, 0)),
                  cache_spec, cache_spec, new_spec, new_spec,
                  pl.BlockSpec((1, DIFF_DV), lambda b, *_: (0, 0))],
        out_specs=pl.BlockSpec((sb * tq, nh * DIFF_DV), lambda b, *_: (b, 0)),
    )
    return pl.pallas_call(
        functools.partial(_attn_sample_kernel, tq=tq, past=past, sb=sb, lam_init=lam_init),
        grid_spec=grid_spec,
        out_shape=jax.ShapeDtypeStruct((nb * tq, nh * DIFF_DV), BF16),
        compiler_params=pltpu.CompilerParams(
            dimension_semantics=("arbitrary",), vmem_limit_bytes=VMEM_LIMIT_BYTES),
        name="attn_sample",
    )(slopes, lam_p, qd, cache_k, cache_v, k_new, v_new, g_sub)


def _out_ffn_kernel(og_ref, od_ref, x_ref, wo_ref, wu_ref, wd_ref,
                    gpm_ref, gpf_ref, gqf_ref, y_ref, *, ff_chunk):
    mix = (jnp.dot(og_ref[...], wo_ref[:W_V, :], preferred_element_type=F32)
           + jnp.dot(od_ref[...], wo_ref[W_V:, :], preferred_element_type=F32))
    x1 = x_ref[...] + _rms(mix, gpm_ref[...])
    f = _rms(x1, gpf_ref[...]).astype(BF16)
    d_ff = wu_ref.shape[1]
    acc = jnp.zeros(x1.shape, F32)
    for c in range(d_ff // ff_chunk):
        sl = slice(c * ff_chunk, (c + 1) * ff_chunk)
        hid = jnp.dot(f, wu_ref[:, sl], preferred_element_type=F32)
        hid = jnp.square(jnp.maximum(hid, 0.0)).astype(BF16)
        acc = acc + jnp.dot(hid, wd_ref[sl, :], preferred_element_type=F32)
    y_ref[...] = x1 + _rms(acc, gqf_ref[...])


def _out_ffn(og, od, x2d, w_out, w_up, w_down, g_post_mix, g_pre_ffn, g_post_ffn, *, tm):
    n, d = x2d.shape
    assert n % tm == 0 and w_up.shape[1] % FF_CHUNK == 0
    row = lambda i: (i, 0)
    return pl.pallas_call(
        functools.partial(_out_ffn_kernel, ff_chunk=FF_CHUNK),
        grid=(n // tm,),
        in_specs=[pl.BlockSpec((tm, W_V), row), pl.BlockSpec((tm, W_V), row),
                  pl.BlockSpec((tm, d), row),
                  _const_spec(w_out.shape), _const_spec(w_up.shape), _const_spec(w_down.shape),
                  _const_spec((1, d)), _const_spec((1, d)), _const_spec((1, d))],
        out_specs=pl.BlockSpec((tm, d), row),
        out_shape=jax.ShapeDtypeStruct((n, d), F32),
        compiler_params=pltpu.CompilerParams(
            dimension_semantics=("arbitrary",), vmem_limit_bytes=VMEM_LIMIT_BYTES),
        name="out_ffn",
    )(og, od, x2d, w_out, w_up, w_down, g_post_mix, g_pre_ffn, g_post_ffn)


def _split_w_in(w):
    a0 = 2 * W_G + 2 * W_V
    a1 = a0 + GLA_RANK
    w_a = jnp.pad(w[:, a0:a1], ((0, 0), (0, LANES - GLA_RANK)))
    return w[:, :a0].astype(BF16), w[:, a1:].astype(BF16), w_a.astype(BF16)


def kernel(x_prompt, x_sample, cache_k, cache_v, state_gla, w_in, w_gate_up, b_gate, g_gla_out,
           lam_q1, lam_k1, lam_q2, lam_k2, g_subln, w_out, g_pre_mix, g_post_mix,
           g_pre_ffn, g_post_ffn, w_ff_up, w_ff_down):
    depth = w_in.shape[0]
    nb_p, t_p, d = x_prompt.shape
    nb_s, t_s, _ = x_sample.shape
    past = cache_k.shape[2]
    slopes = jnp.exp2(-8.0 / DIFF_HEADS * jnp.arange(1, DIFF_HEADS + 1, dtype=F32))
    yp = x_prompt.reshape(nb_p * t_p, d)
    ys = x_sample.reshape(nb_s * t_s, d)
    outs = [[] for _ in range(6)]
    for l in range(depth):
        lam_init = 0.8 - 0.6 * math.exp(-0.3 * l)
        w_parts = _split_w_in(w_in[l])
        wg_pad = jnp.concatenate(
            [w_gate_up[l], jnp.zeros((LANES - GLA_RANK, W_G), F32)], axis=0).astype(BF16)
        bg = b_gate[l][None, :]
        lam_p = jnp.stack([lam_q1[l], lam_k1[l], lam_q2[l], lam_k2[l]], axis=0)
        g_out = g_gla_out[l][None, :]
        g_sub = g_subln[l][None, :]
        wo = w_out[l].astype(BF16)
        wu = w_ff_up[l].astype(BF16)
        wd = w_ff_down[l].astype(BF16)
        gains = (g_post_mix[l][None, :], g_pre_ffn[l][None, :], g_post_ffn[l][None, :])
        g_pre = g_pre_mix[l][None, :]

        qg, kg, vg, rg, gate, qd, kd, kdb, vd, vdb = _in_proj(
            yp, g_pre, w_parts, wg_pad, bg, nb=nb_p, t=t_p, tm=min(IN_PROJ_ROWS, t_p),
            transposed=True)
        s0 = jnp.zeros((nb_p, GLA_HEADS, GLA_DK, GLA_DV), F32)
        og, s_p = _gla(qg, kg, vg, rg, gate, s0, g_out, nb=nb_p, t=t_p, L=CHUNK,
                       CB=GLA_CHUNKS_PER_TILE, carry_state=True)
        od = _attn_prompt(slopes, lam_p, qd, kdb, vdb, g_sub, tq=ATTN_BLOCK, lam_init=lam_init)
        yp = _out_ffn(og, od, yp, wo, wu, wd, *gains, tm=min(OUT_FFN_ROWS, nb_p * t_p))
        outs[0].append(kd.reshape(nb_p, t_p, DIFF_HEADS, 2 * DIFF_DH))
        outs[1].append(vd.reshape(nb_p, t_p, DIFF_HEADS, DIFF_DV))
        outs[2].append(s_p)

        n_s = nb_s * t_s
        qg, kg, vg, rg, gate, qd, kd, _, vd, _ = _in_proj(
            ys, g_pre, w_parts, wg_pad, bg, nb=1, t=n_s, tm=n_s, transposed=False)
        og, s_s = _gla(qg, kg, vg, rg, gate, state_gla[l], g_out, nb=nb_s, t=t_s, L=t_s,
                       CB=math.gcd(nb_s, GLA_CHUNKS_PER_TILE), carry_state=False)
        od = _attn_sample(slopes, lam_p, qd,
                          cache_k[l].reshape(nb_s, past * DIFF_HEADS, 2 * DIFF_DH),
                          cache_v[l].reshape(nb_s, past * DIFF_HEADS, DIFF_DV),
                          kd, vd, g_sub, lam_init=lam_init)
        ys = _out_ffn(og, od, ys, wo, wu, wd, *gains, tm=n_s)
        outs[3].append(kd.reshape(nb_s, t_s, DIFF_HEADS, 2 * DIFF_DH))
        outs[4].append(vd.reshape(nb_s, t_s, DIFF_HEADS, DIFF_DV))
        outs[5].append(s_s)

    stack = lambda xs: jnp.stack(xs, axis=0)
    return (yp.reshape(nb_p, t_p, d), ys.reshape(nb_s, t_s, d),
            stack(outs[0]), stack(outs[1]), stack(outs[2]),
            stack(outs[3]), stack(outs[4]), stack(outs[5]))
```
